```python
import math
import jax, jax.numpy as jnp
from jax import lax
import numpy as np

D_MODEL = 1024
BATCH = 2
SEQ = 8192
DEPTH = 1

HEAD_DIM = 64
ATTN_GROUPS = ((128, 1), (512, 4), (2048, 16))
N_ATTN_GROUPS = len(ATTN_GROUPS)
HEADS_PER_GROUP = 8
N_ATTN_HEADS = N_ATTN_GROUPS * HEADS_PER_GROUP
ATTN_OUT_WIDTH = HEADS_PER_GROUP * HEAD_DIM
QKV_WIDTH = 3 * N_ATTN_HEADS * HEAD_DIM
BLOCK = 128
REL_BUCKETS = 32
REL_MAX_DIST = 2048
SSM_WIDTH = D_MODEL // 2
SSM_GROUP = 16
SSM_GROUPS = SSM_WIDTH // SSM_GROUP
SSM_STATE = 64
IN_WIDTH = QKV_WIDTH + SSM_WIDTH + 2 * D_MODEL
N_EXPERTS = 32
TOP_K = 4
D_FF = D_MODEL
SWIGLU_LIMIT = 7.0
SWIGLU_ALPHA = 1.702
EPS = 1e-6

kernel_name = "hybrid_dilated_attn_s5_moe_block"


def rms_norm(x, g):
    xf = x.astype(jnp.float32)
    y = xf * lax.rsqrt(jnp.mean(xf * xf, axis=-1, keepdims=True) + EPS) * g.astype(jnp.float32)
    return y.astype(x.dtype)


def head_rms_norm(t, g):
    tf = t.astype(jnp.float32)
    return tf * lax.rsqrt(jnp.mean(tf * tf, axis=-1, keepdims=True) + EPS) * g.astype(jnp.float32)[:, None, :]


def rel_bucket_np(dist):
    max_exact = REL_BUCKETS // 2
    d = np.maximum(dist, 1).astype(np.float32)
    large = max_exact + (np.log(d / max_exact) / math.log(REL_MAX_DIST / max_exact)
                         * (REL_BUCKETS - max_exact)).astype(np.int32)
    large = np.minimum(large, REL_BUCKETS - 1)
    return np.where(dist < max_exact, dist, large).astype(np.int32)


def dilated_window_attention(q, k, v, bias_table, window, dilation):
    b, s, h, dh = q.shape
    span = window // dilation
    L = s // dilation
    blk = min(BLOCK, L)
    nblk = -(-L // blk)
    Lp = nblk * blk

    def to_blocks(t):
        t = t.reshape(b, L, dilation, h, dh).transpose(0, 2, 1, 3, 4).reshape(b * dilation, L, h, dh)
        t = jnp.pad(t, ((0, 0), (0, Lp - L), (0, 0), (0, 0)))
        return t.reshape(b * dilation, nblk, blk, h, dh)

    def with_prev(t):
        prev = jnp.pad(t, ((0, 0), (1, 0), (0, 0), (0, 0), (0, 0)))[:, :-1]
        return jnp.concatenate([prev, t], axis=2)

    qb = to_blocks(q)
    kc = with_prev(to_blocks(k))
    vc = with_prev(to_blocks(v))

    qi = np.arange(blk)[:, None]
    ki = np.arange(2 * blk)[None, :]
    steps = qi + blk - ki
    band = (steps >= 0) & (steps <= span)
    key_pos = np.arange(nblk)[:, None] * blk - blk + np.arange(2 * blk)[None, :]
    mask = band[None] & (key_pos >= 0)[:, None, :]
    bucket = rel_bucket_np(np.clip(steps, 0, None) * dilation)
    bias = jnp.transpose(bias_table[bucket], (2, 0, 1)).astype(jnp.float32)

    logits = jnp.einsum('nbqhd,nbkhd->nbhqk', qb, kc) + bias
    logits = jnp.where(mask[None, :, None], logits, -jnp.inf)
    m = jnp.max(logits, axis=-1, keepdims=True)
    p = jnp.exp(logits - m)
    den = jnp.sum(p, axis=-1)
    o = jnp.einsum('nbhqk,nbkhd->nbqhd', p, vc) / jnp.transpose(den, (0, 1, 3, 2))[..., None]
    lse = jnp.transpose(m[..., 0] + jnp.log(den), (0, 1, 3, 2))

    o = o.reshape(b, dilation, Lp, h, dh)[:, :, :L].transpose(0, 2, 1, 3, 4).reshape(b, s, h, dh)
    lse = lse.reshape(b, dilation, Lp, h)[:, :, :L].transpose(0, 2, 1, 3).reshape(b, s, h)
    return o, lse


def s5_ssm(u, a_re, a_im, log_dt, b_re, b_im, c_re, c_im, d_skip):
    bsz, s, _ = u.shape
    uf = u.astype(jnp.float32).reshape(bsz, s, SSM_GROUPS, SSM_GROUP)
    lam = lax.complex(a_re.astype(jnp.float32), a_im.astype(jnp.float32))
    dt = jnp.exp(log_dt.astype(jnp.float32))[:, None]
    a_bar = jnp.exp(lam * dt)
    b_mat = lax.complex(b_re.astype(jnp.float32), b_im.astype(jnp.float32))
    b_bar = ((a_bar - 1.0) / lam)[..., None] * b_mat
    bu = jnp.einsum('bsgc,gnc->bsgn', uf.astype(jnp.complex64), b_bar)
    a_full = jnp.broadcast_to(a_bar, bu.shape)

    def combine(earlier, later):
        a1, b1 = earlier
        a2, b2 = later
        return a2 * a1, a2 * b1 + b2

    _, states = lax.associative_scan(combine, (a_full, bu), axis=1)
    c_mat = lax.complex(c_re.astype(jnp.float32), c_im.astype(jnp.float32))
    y = jnp.einsum('bsgn,gcn->bsgc', states, c_mat).real
    y = y + d_skip.astype(jnp.float32).reshape(SSM_GROUPS, SSM_GROUP) * uf
    return y.reshape(bsz, s, SSM_WIDTH).astype(u.dtype)


def moe_ffn(h, w_router, b_router, w_gu, b_gu, w_down, b_down):
    b, s, d = h.shape
    t = h.reshape(-1, d)
    logits = (t @ w_router + b_router).astype(jnp.float32)
    top_v, top_i = lax.top_k(logits, TOP_K)
    gates = jax.nn.softmax(top_v, axis=-1)
    flat_e = top_i.reshape(-1)
    order = jnp.argsort(flat_e)
    e_sorted = flat_e[order]
    tok = order // TOP_K
    xs = t[tok]
    sizes = jnp.bincount(flat_e, length=N_EXPERTS).astype(jnp.int32)
    gu = lax.ragged_dot(xs, w_gu, sizes) + b_gu[e_sorted]
    x_glu = jnp.minimum(gu[:, :D_FF], SWIGLU_LIMIT)
    x_lin = jnp.clip(gu[:, D_FF:], -SWIGLU_LIMIT, SWIGLU_LIMIT)
    act = x_glu * jax.nn.sigmoid(SWIGLU_ALPHA * x_glu) * (x_lin + 1.0)
    out = lax.ragged_dot(act, w_down, sizes) + b_down[e_sorted]
    out = out * gates.reshape(-1)[order][:, None].astype(out.dtype)
    y = jnp.zeros_like(t).at[tok].add(out)
    return y.reshape(b, s, d)


def setup_inputs(seed: int = 0) -> dict:
    key = jax.random.key(seed)
    ks = jax.random.split(key, 26)
    f32 = jnp.float32
    L = DEPTH
    nrm = lambda k, shape, scale: jax.random.normal(k, shape, f32) * scale
    n_idx = jnp.arange(SSM_STATE, dtype=f32)
    return {
        "x": nrm(ks[0], (BATCH, SEQ, D_MODEL), 1.0),
        "g_mix": 1.0 + nrm(ks[1], (L, D_MODEL), 0.02),
        "w_in": nrm(ks[2], (L, D_MODEL, IN_WIDTH), D_MODEL ** -0.5),
        "q_gain": 1.0 + nrm(ks[3], (L, N_ATTN_GROUPS, HEAD_DIM), 0.02),
        "k_gain": 1.0 + nrm(ks[4], (L, N_ATTN_GROUPS, HEAD_DIM), 0.02),
        "rel_bias": nrm(ks[5], (REL_BUCKETS, N_ATTN_HEADS), 0.5),
        "ssm_a_re": -0.5 + nrm(ks[6], (L, SSM_GROUPS, SSM_STATE), 0.01),
        "ssm_a_im": math.pi * n_idx + nrm(ks[7], (L, SSM_GROUPS, SSM_STATE), 0.01),
        "ssm_log_dt": jax.random.uniform(ks[8], (L, SSM_GROUPS), f32, math.log(1e-3), math.log(1e-1)),
        "ssm_b_re": nrm(ks[9], (L, SSM_GROUPS, SSM_STATE, SSM_GROUP), (2 * SSM_GROUP) ** -0.5),
        "ssm_b_im": nrm(ks[10], (L, SSM_GROUPS, SSM_STATE, SSM_GROUP), (2 * SSM_GROUP) ** -0.5),
        "ssm_c_re": nrm(ks[11], (L, SSM_GROUPS, SSM_GROUP, SSM_STATE), (2 * SSM_STATE) ** -0.5),
        "ssm_c_im": nrm(ks[12], (L, SSM_GROUPS, SSM_GROUP, SSM_STATE), (2 * SSM_STATE) ** -0.5),
        "ssm_d": nrm(ks[13], (L, SSM_WIDTH), 1.0),
        "w_glu": nrm(ks[14], (L, SSM_WIDTH, SSM_WIDTH), SSM_WIDTH ** -0.5),
        "b_glu": nrm(ks[15], (L, SSM_WIDTH), 0.01),
        "w_attn_proj": nrm(ks[16], (L, ATTN_OUT_WIDTH, D_MODEL), ATTN_OUT_WIDTH ** -0.5),
        "w_ssm_proj": nrm(ks[17], (L, SSM_WIDTH, D_MODEL), SSM_WIDTH ** -0.5),
        "w_out": nrm(ks[18], (L, D_MODEL, D_MODEL), D_MODEL ** -0.5),
        "g_ffn": 1.0 + nrm(ks[19], (L, D_MODEL), 0.02),
        "w_router": nrm(ks[20], (L, D_MODEL, N_EXPERTS), D_MODEL ** -0.5),
        "b_router": nrm(ks[21], (L, N_EXPERTS), 0.01),
        "w_gate_up": nrm(ks[22], (L, N_EXPERTS, D_MODEL, 2 * D_FF), D_MODEL ** -0.5),
        "b_gate_up": nrm(ks[23], (L, N_EXPERTS, 2 * D_FF), 0.01),
        "w_down": nrm(ks[24], (L, N_EXPERTS, D_FF, D_MODEL), D_FF ** -0.5),
        "b_down": nrm(ks[25], (L, N_EXPERTS, D_MODEL), 0.01),
    }


def reference(x, g_mix, w_in, q_gain, k_gain, rel_bias, ssm_a_re, ssm_a_im, ssm_log_dt,
              ssm_b_re, ssm_b_im, ssm_c_re, ssm_c_im, ssm_d, w_glu, b_glu, w_attn_proj,
              w_ssm_proj, w_out, g_ffn, w_router, b_router, w_gate_up, b_gate_up, w_down, b_down):
    b, s, _ = x.shape
    for l in range(DEPTH):
        h = rms_norm(x, g_mix[l])
        proj = h @ w_in[l]
        qkv = proj[..., :QKV_WIDTH].reshape(b, s, 3, N_ATTN_GROUPS, HEADS_PER_GROUP, HEAD_DIM)
        u = proj[..., QKV_WIDTH:QKV_WIDTH + SSM_WIDTH]
        gate_a = proj[..., QKV_WIDTH + SSM_WIDTH:QKV_WIDTH + SSM_WIDTH + D_MODEL]
        gate_b = proj[..., QKV_WIDTH + SSM_WIDTH + D_MODEL:]

        q = head_rms_norm(qkv[:, :, 0], q_gain[l]) * (HEAD_DIM ** -0.5)
        k = head_rms_norm(qkv[:, :, 1], k_gain[l])
        v = qkv[:, :, 2].astype(jnp.float32)
        outs, lses = [], []
        for gi, (window, dilation) in enumerate(ATTN_GROUPS):
            table = rel_bias[:, gi * HEADS_PER_GROUP:(gi + 1) * HEADS_PER_GROUP]
            o_g, lse_g = dilated_window_attention(q[:, :, gi], k[:, :, gi], v[:, :, gi], table, window, dilation)
            outs.append(o_g)
            lses.append(lse_g)
        w_den = jax.nn.softmax(jnp.stack(lses, axis=0), axis=0)
        attn = jnp.einsum('gbsh,gbshd->bshd', w_den, jnp.stack(outs, axis=0))
        y_a = attn.reshape(b, s, ATTN_OUT_WIDTH).astype(x.dtype) @ w_attn_proj[l]

        y_s = s5_ssm(u, ssm_a_re[l], ssm_a_im[l], ssm_log_dt[l], ssm_b_re[l], ssm_b_im[l],
                     ssm_c_re[l], ssm_c_im[l], ssm_d[l])
        y_s = jax.nn.gelu(y_s)
        y_s = y_s * jax.nn.sigmoid(y_s @ w_glu[l] + b_glu[l])
        y_b = y_s @ w_ssm_proj[l]

        mixed = jax.nn.sigmoid(gate_a) * y_a + jax.nn.sigmoid(gate_b) * y_b
        x = x + mixed @ w_out[l]

        h2 = rms_norm(x, g_ffn[l])
        x = x + moe_ffn(h2, w_router[l], b_router[l], w_gate_up[l], b_gate_up[l], w_down[l], b_down[l])
    return x
```

```python
import functools
import math

import jax
import jax.numpy as jnp
import numpy as np
from jax import lax
from jax.experimental import pallas as pl
from jax.experimental.pallas import tpu as pltpu

F32 = jnp.float32
BF16 = jnp.bfloat16

D_MODEL = 1024
HEAD_DIM = 64
ATTN_GROUPS = ((128, 1), (512, 4), (2048, 16))
HEADS_PER_GROUP = 8
GROUP_WIDTH = HEADS_PER_GROUP * HEAD_DIM
N_ATTN_HEADS = len(ATTN_GROUPS) * HEADS_PER_GROUP
QKV_WIDTH = 3 * N_ATTN_HEADS * HEAD_DIM
ATTN_BLOCK = 128
REL_BUCKETS = 32
REL_MAX_DIST = 2048
SSM_WIDTH = 512
SSM_GROUP = 16
SSM_GROUPS = 32
SSM_STATE = 64
IN_WIDTH = QKV_WIDTH + SSM_WIDTH + 2 * D_MODEL
N_EXPERTS = 32
TOP_K = 4
D_FF = 1024
SWIGLU_LIMIT = 7.0
SWIGLU_ALPHA = 1.702
EPS = 1e-6

SUBLANES = 8
LANES = 128
MASK_VALUE = -1e30
VMEM_LIMIT = 56 * 1024 * 1024

U_BLOCK = QKV_WIDTH // SSM_WIDTH
GATE_A_BLOCK = (QKV_WIDTH + SSM_WIDTH) // D_MODEL
GATE_B_BLOCK = GATE_A_BLOCK + 1
PROJ_BLOCKS = IN_WIDTH // GROUP_WIDTH

ROW_TILE = 256
SSM_TIME_TILE = 512
MOE_TILE = 256


def _params(n_axes, vmem=VMEM_LIMIT):
    return pltpu.CompilerParams(dimension_semantics=("arbitrary",) * n_axes, vmem_limit_bytes=vmem)


def _inproj_kernel(x_ref, g_ref, w_ref, o_ref):
    x = x_ref[...]
    ms = jnp.mean(x * x, axis=-1, keepdims=True)
    h = x * lax.rsqrt(ms + EPS) * g_ref[...]
    o_ref[...] = jnp.dot(h.astype(BF16), w_ref[...], preferred_element_type=F32)


def _inproj(x2d, g, w_bf16):
    t = x2d.shape[0]
    return pl.pallas_call(
        _inproj_kernel,
        grid=(t // ROW_TILE,),
        in_specs=[
            pl.BlockSpec((ROW_TILE, D_MODEL), lambda i: (i, 0)),
            pl.BlockSpec((1, D_MODEL), lambda i: (0, 0)),
            pl.BlockSpec((D_MODEL, IN_WIDTH), lambda i: (0, 0), pipeline_mode=pl.Buffered(1)),
        ],
        out_specs=pl.BlockSpec((ROW_TILE, IN_WIDTH), lambda i: (i, 0)),
        out_shape=jax.ShapeDtypeStruct((t, IN_WIDTH), F32),
        compiler_params=_params(1),
        name="inproj",
    )(x2d, g.reshape(1, D_MODEL), w_bf16)


def _rel_bucket_np(dist):
    max_exact = REL_BUCKETS // 2
    d = np.maximum(dist, 1).astype(np.float32)
    large = max_exact + (np.log(d / max_exact) / math.log(REL_MAX_DIST / max_exact)
                         * (REL_BUCKETS - max_exact)).astype(np.int32)
    large = np.minimum(large, REL_BUCKETS - 1)
    return np.where(dist < max_exact, dist, large).astype(np.int32)


def _attn_bias(table, window, dilation):
    blk = ATTN_BLOCK
    span = window // dilation
    qi = np.arange(blk)[:, None]
    ki = np.arange(2 * blk)[None, :]
    steps = qi + blk - ki
    band = (steps >= 0) & (steps <= span)
    bucket = _rel_bucket_np(np.clip(steps, 0, None) * dilation)
    bias = jnp.transpose(table[bucket], (2, 0, 1)).astype(F32)
    b0 = jnp.where(band[None], bias, MASK_VALUE)
    b1 = jnp.where((band & (ki >= blk))[None], bias, MASK_VALUE)
    return jnp.stack([b0, b1], axis=0)


def _attn_kernel(q_ref, k_ref, v_ref, bias_ref, qg_ref, kg_ref, seg_ref, o_ref, lse_ref,
                 kcat_ref, vcat_ref):
    blk = ATTN_BLOCK
    i = pl.program_id(2)
    seg = seg_ref[...]

    def head_norm(t, gain):
        ss = _split_dot(t * t, seg)
        return t * lax.rsqrt(ss * (1.0 / HEAD_DIM) + EPS) * gain

    qn = (head_norm(q_ref[0], qg_ref[...]) * (HEAD_DIM ** -0.5)).astype(BF16)

    @pl.when(i == 0)
    def _():
        kcat_ref[0:blk, :] = jnp.zeros((blk, GROUP_WIDTH), BF16)
        vcat_ref[0:blk, :] = jnp.zeros((blk, GROUP_WIDTH), BF16)

    @pl.when(i > 0)
    def _():
        kcat_ref[0:blk, :] = kcat_ref[blk:2 * blk, :]
        vcat_ref[0:blk, :] = vcat_ref[blk:2 * blk, :]

    kcat_ref[blk:2 * blk, :] = head_norm(k_ref[0], kg_ref[...]).astype(BF16)
    vcat_ref[blk:2 * blk, :] = v_ref[0].astype(BF16)
    first = (i == 0).astype(jnp.int32)

    lane = lax.broadcasted_iota(jnp.int32, (blk, LANES), 1)
    low_half = lane < HEAD_DIM
    lse_tile = jnp.zeros((blk, LANES), F32)
    outs = []
    for p in range(GROUP_WIDTH // LANES):
        qp = qn[:, p * LANES:(p + 1) * LANES]
        kp = kcat_ref[:, p * LANES:(p + 1) * LANES]
        vp = vcat_ref[:, p * LANES:(p + 1) * LANES]
        pair = []
        for hh in range(2):
            head = 2 * p + hh
            sel = low_half if hh == 0 else jnp.logical_not(low_half)
            qm = jnp.where(sel, qp, jnp.zeros_like(qp))
            s = lax.dot_general(qm, kp, (((1,), (1,)), ((), ())), preferred_element_type=F32)
            s = s + bias_ref[first, head]
            mx = jnp.max(s, axis=-1, keepdims=True)
            pe = jnp.exp(s - mx)
            den = jnp.sum(pe, axis=-1, keepdims=True)
            o = jnp.dot(pe.astype(BF16), vp, preferred_element_type=F32)
            pair.append(o * (1.0 / den))
            lse_tile = jnp.where(lane == head, mx + jnp.log(den), lse_tile)
        outs.append(jnp.where(low_half, pair[0], pair[1]))
    o_ref[0] = jnp.concatenate(outs, axis=-1)
    lse_ref[0] = lse_tile


def _attention_group(proj3, gi, bias, q_gain, k_gain, seg):
    b, s, _ = proj3.shape
    _, dil = ATTN_GROUPS[gi]
    blk = ATTN_BLOCK
    length = s // dil
    nblk = length // blk
    view = proj3.reshape(b, length, dil * IN_WIDTH)
    n_g = len(ATTN_GROUPS)

    def col(c):
        return lambda bi, r, i: (bi, i, r * PROJ_BLOCKS + c)

    o, lse = pl.pallas_call(
        _attn_kernel,
        grid=(b, dil, nblk),
        in_specs=[
            pl.BlockSpec((1, blk, GROUP_WIDTH), col(gi)),
            pl.BlockSpec((1, blk, GROUP_WIDTH), col(n_g + gi)),
            pl.BlockSpec((1, blk, GROUP_WIDTH), col(2 * n_g + gi)),
            pl.BlockSpec((2, HEADS_PER_GROUP, blk, 2 * blk), lambda bi, r, i: (0, 0, 0, 0)),
            pl.BlockSpec((1, GROUP_WIDTH), lambda bi, r, i: (0, 0)),
            pl.BlockSpec((1, GROUP_WIDTH), lambda bi, r, i: (0, 0)),
            pl.BlockSpec((GROUP_WIDTH, GROUP_WIDTH), lambda bi, r, i: (0, 0)),
        ],
        out_specs=[
            pl.BlockSpec((1, blk, GROUP_WIDTH), lambda bi, r, i: (bi, i, r)),
            pl.BlockSpec((1, blk, LANES), lambda bi, r, i: (bi, i, r)),
        ],
        out_shape=[
            jax.ShapeDtypeStruct((b, length, dil * GROUP_WIDTH), F32),
            jax.ShapeDtypeStruct((b, length, dil * LANES), F32),
        ],
        scratch_shapes=[
            pltpu.VMEM((2 * blk, GROUP_WIDTH), BF16),
            pltpu.VMEM((2 * blk, GROUP_WIDTH), BF16),
        ],
        compiler_params=_params(3),
        name=f"attn_g{gi}",
    )(view, view, view, bias,
      jnp.tile(q_gain, HEADS_PER_GROUP).reshape(1, GROUP_WIDTH),
      jnp.tile(k_gain, HEADS_PER_GROUP).reshape(1, GROUP_WIDTH), seg)
    return o.reshape(b * s, GROUP_WIDTH), lse.reshape(b * s, LANES)


HALF_STATE = SSM_GROUPS * SSM_STATE // 2
SCAN_COLS = 4


def _ssm_tables(a_re, a_im, log_dt, b_re, b_im, c_re, c_im):
    lam = lax.complex(a_re.astype(F32), a_im.astype(F32))
    dt = jnp.exp(log_dt.astype(F32))[:, None]
    a_bar = jnp.exp(lam * dt)
    b_bar = ((a_bar - 1.0) / lam)[..., None] * lax.complex(b_re.astype(F32), b_im.astype(F32))

    rows = jnp.arange(SUBLANES)
    tabs = []
    for s in (1, 2, 4):
        p = jnp.exp(lam * dt * s).reshape(1, -1)
        tabs.append(jnp.where((rows >= s)[:, None], p, 0.0))
    tabs.append(jnp.exp((lam * dt).reshape(1, -1) * (rows + 1.0)[:, None]))
    tab = jnp.stack(tabs, axis=0)
    tab = jnp.stack([tab.real, tab.imag], axis=1).astype(F32)

    gh = SSM_GROUPS // 2
    eye = jnp.eye(gh, dtype=F32)

    def in_mat(m):
        return jnp.einsum('gnc,gh->gchn', m, eye).reshape(gh * SSM_GROUP, gh * SSM_STATE)

    def out_mat(m):
        return jnp.einsum('gcn,gh->gnhc', m, eye).reshape(gh * SSM_STATE, gh * SSM_GROUP)

    b_mats, c_mats = [], []
    for h in range(2):
        sl = slice(h * gh, (h + 1) * gh)
        b_mats.append(jnp.concatenate([in_mat(b_bar.real[sl]), in_mat(b_bar.imag[sl])], axis=1))
        c_mats.append(jnp.concatenate([out_mat(c_re.astype(F32)[sl]),
                                       -out_mat(c_im.astype(F32)[sl])], axis=0))
    return tab, jnp.stack(b_mats).astype(BF16), jnp.stack(c_mats).astype(BF16)


def _ssm_kernel(u_ref, tab_ref, bmat_ref, cmat_ref, d_ref, y_ref, xs_ref, carry_ref):
    tt = u_ref.shape[1]
    half_w = 2 * HALF_STATE

    @pl.when(pl.program_id(1) == 0)
    def _():
        carry_ref[...] = jnp.zeros_like(carry_ref)

    u = u_ref[0]
    ub = u.astype(BF16)
    hw = SSM_WIDTH // 2
    for h in range(2):
        xs_ref[:, h * half_w:(h + 1) * half_w] = jnp.dot(
            ub[:, h * hw:(h + 1) * hw], bmat_ref[h], preferred_element_type=F32)

    n_cols = 2 * HALF_STATE // LANES
    for c0 in range(0, n_cols, SCAN_COLS):
        cols = []
        for c in range(c0, c0 + SCAN_COLS):
            h, j = divmod(c, HALF_STATE // LANES)
            re_col = h * half_w + j * LANES
            cols.append((c * LANES, re_col, re_col + HALF_STATE))

        def body(rb, carry, cols=cols):
            r0 = pl.multiple_of(rb * SUBLANES, SUBLANES)
            new = []
            for (tc, rc, ic), (cr, ci) in zip(cols, carry):
                re = xs_ref[pl.ds(r0, SUBLANES), rc:rc + LANES]
                im = xs_ref[pl.ds(r0, SUBLANES), ic:ic + LANES]
                for lvl, s in enumerate((1, 2, 4)):
                    ar = tab_ref[lvl, 0, :, tc:tc + LANES]
                    ai = tab_ref[lvl, 1, :, tc:tc + LANES]
                    sr = pltpu.roll(re, s, 0)
                    si = pltpu.roll(im, s, 0)
                    re, im = re + ar * sr - ai * si, im + ar * si + ai * sr
                pr = tab_ref[3, 0, :, tc:tc + LANES]
                pi = tab_ref[3, 1, :, tc:tc + LANES]
                re, im = re + pr * cr - pi * ci, im + pr * ci + pi * cr
                xs_ref[pl.ds(r0, SUBLANES), rc:rc + LANES] = re
                xs_ref[pl.ds(r0, SUBLANES), ic:ic + LANES] = im
                new.append((jnp.broadcast_to(re[SUBLANES - 1:, :], (SUBLANES, LANES)),
                            jnp.broadcast_to(im[SUBLANES - 1:, :], (SUBLANES, LANES))))
            return tuple(new)

        init = tuple((carry_ref[:, rc:rc + LANES], carry_ref[:, ic:ic + LANES]) for _, rc, ic in cols)
        fin = lax.fori_loop(0, tt // SUBLANES, body, init)
        for (_, rc, ic), (cr, ci) in zip(cols, fin):
            carry_ref[:, rc:rc + LANES] = cr
            carry_ref[:, ic:ic + LANES] = ci

    ys = []
    for h in range(2):
        xh = xs_ref[:, h * half_w:(h + 1) * half_w].astype(BF16)
        ys.append(jnp.dot(xh, cmat_ref[h], preferred_element_type=F32))
    y_ref[0] = jnp.concatenate(ys, axis=-1) + d_ref[...] * u


def _ssm(proj3, tab, bmat, cmat, d_skip):
    b, s, _ = proj3.shape
    tt = min(SSM_TIME_TILE, s)
    n_state = 4 * HALF_STATE
    return pl.pallas_call(
        _ssm_kernel,
        grid=(b, s // tt),
        in_specs=[
            pl.BlockSpec((1, tt, SSM_WIDTH), lambda bi, j: (bi, j, U_BLOCK)),
            pl.BlockSpec(tab.shape, lambda bi, j: (0, 0, 0, 0)),
            pl.BlockSpec(bmat.shape, lambda bi, j: (0, 0, 0)),
            pl.BlockSpec(cmat.shape, lambda bi, j: (0, 0, 0)),
            pl.BlockSpec((1, SSM_WIDTH), lambda bi, j: (0, 0)),
        ],
        out_specs=pl.BlockSpec((1, tt, SSM_WIDTH), lambda bi, j: (bi, j, 0)),
        out_shape=jax.ShapeDtypeStruct((b, s, SSM_WIDTH), F32),
        scratch_shapes=[
            pltpu.VMEM((tt, n_state), F32),
            pltpu.VMEM((SUBLANES, n_state), F32),
        ],
        compiler_params=_params(2),
        name="ssm",
    )(proj3, tab, bmat, cmat, d_skip.reshape(1, SSM_WIDTH)).reshape(b * s, SSM_WIDTH)


ROUTE_IDS, ROUTE_GATES, ROUTE_RANKS = 0, TOP_K, 2 * TOP_K


def _split_dot(x, w_bf16):
    hi = x.astype(BF16)
    lo = (x - hi.astype(F32)).astype(BF16)
    return (jnp.dot(hi, w_bf16, preferred_element_type=F32)
            + jnp.dot(lo, w_bf16, preferred_element_type=F32))


def _mix_kernel(o0_ref, o1_ref, o2_ref, l0_ref, l1_ref, l2_ref, ys_ref, ga_ref, gb_ref, x_ref,
                expand_ref, wa_ref, wglu_ref, bglu_ref, ws_ref, wo_ref, gffn_ref, wr_ref, br_ref,
                x1_ref, h2_ref, route_ref, counts_ref, run_ref):
    tm = x_ref.shape[0]

    @pl.when(pl.program_id(0) == 0)
    def _():
        run_ref[...] = jnp.zeros_like(run_ref)

    lses = [l0_ref[...], l1_ref[...], l2_ref[...]]
    mx = jnp.maximum(jnp.maximum(lses[0], lses[1]), lses[2])
    es = [jnp.exp(l - mx) for l in lses]
    inv = 1.0 / (es[0] + es[1] + es[2])
    expand = expand_ref[...]
    attn = jnp.zeros((tm, GROUP_WIDTH), F32)
    for e, o_ref in zip(es, (o0_ref, o1_ref, o2_ref)):
        attn = attn + _split_dot(e * inv, expand) * o_ref[...]
    y_a = jnp.dot(attn.astype(BF16), wa_ref[...], preferred_element_type=F32)

    ys = ys_ref[...]
    ys = 0.5 * ys * (1.0 + jnp.tanh(math.sqrt(2.0 / math.pi) * (ys + 0.044715 * (ys * ys * ys))))
    glu = jnp.dot(ys.astype(BF16), wglu_ref[...], preferred_element_type=F32) + bglu_ref[...]
    ys = ys * jax.nn.sigmoid(glu)
    y_b = jnp.dot(ys.astype(BF16), ws_ref[...], preferred_element_type=F32)

    mixed = jax.nn.sigmoid(ga_ref[...]) * y_a + jax.nn.sigmoid(gb_ref[...]) * y_b
    x1 = x_ref[...] + jnp.dot(mixed.astype(BF16), wo_ref[...], preferred_element_type=F32)
    x1_ref[...] = x1

    ms = jnp.mean(x1 * x1, axis=-1, keepdims=True)
    h2 = x1 * lax.rsqrt(ms + EPS) * gffn_ref[...]
    for s in range(D_MODEL // LANES):
        h2_ref[pl.ds(s, tm, stride=SUBLANES), :] = h2[:, s * LANES:(s + 1) * LANES]

    logits = jnp.dot(h2.astype(BF16), wr_ref[...], preferred_element_type=F32) + br_ref[...]
    lane = lax.broadcasted_iota(jnp.int32, (tm, LANES), 1)
    work = jnp.where(lane < N_EXPERTS, logits, -jnp.inf)
    sel_mask = jnp.zeros((tm, LANES), F32)
    route = jnp.zeros((tm, LANES), F32)
    vals, hots = [], []
    for k in range(TOP_K):
        v = jnp.max(work, axis=-1, keepdims=True)
        idx = jnp.min(jnp.where(work == v, lane, LANES), axis=-1, keepdims=True)
        hot = lane == idx
        work = jnp.where(hot, -jnp.inf, work)
        sel_mask = jnp.where(hot, 1.0, sel_mask)
        route = jnp.where(lane == ROUTE_IDS + k, idx.astype(F32), route)
        vals.append(v)
        hots.append(hot)
    exps = [jnp.exp(v - vals[0]) for v in vals]
    inv_den = 1.0 / (exps[0] + exps[1] + exps[2] + exps[3])

    r_i = lax.broadcasted_iota(jnp.int32, (tm, tm), 0)
    c_i = lax.broadcasted_iota(jnp.int32, (tm, tm), 1)
    tri = jnp.where(c_i < r_i, 1.0, 0.0).astype(BF16)
    before = jnp.dot(tri, sel_mask.astype(BF16), preferred_element_type=F32) + run_ref[0:1, :]
    for k in range(TOP_K):
        rank = jnp.sum(jnp.where(hots[k], before, 0.0), axis=-1, keepdims=True)
        route = jnp.where(lane == ROUTE_GATES + k, exps[k] * inv_den, route)
        route = jnp.where(lane == ROUTE_RANKS + k, rank, route)
    route_ref[...] = route
    total = run_ref[0:1, :] + jnp.sum(sel_mask, axis=0, keepdims=True)
    run_ref[...] = jnp.broadcast_to(total, run_ref.shape)
    counts_ref[...] = jnp.broadcast_to(total, counts_ref.shape)


def _mix(os_, lses, ys, proj, x2d, w):
    t = x2d.shape[0]
    tm = ROW_TILE
    row = lambda width: pl.BlockSpec((tm, width), lambda i: (i, 0))
    const = lambda a: pl.BlockSpec(a.shape, lambda i: (0,) * a.ndim)
    consts = [w["expand"], w["wa"], w["wglu"], w["bglu"], w["ws"], w["wo"], w["gffn"], w["wr"], w["br"]]
    return pl.pallas_call(
        _mix_kernel,
        grid=(t // tm,),
        in_specs=[row(GROUP_WIDTH)] * 3 + [row(LANES)] * 3 + [
            row(SSM_WIDTH),
            pl.BlockSpec((tm, D_MODEL), lambda i: (i, GATE_A_BLOCK)),
            pl.BlockSpec((tm, D_MODEL), lambda i: (i, GATE_B_BLOCK)),
            row(D_MODEL),
        ] + [const(a) for a in consts],
        out_specs=[
            row(D_MODEL),
            pl.BlockSpec((tm * SUBLANES, LANES), lambda i: (i, 0)),
            row(LANES),
            pl.BlockSpec((SUBLANES, LANES), lambda i: (0, 0)),
        ],
        out_shape=[
            jax.ShapeDtypeStruct((t, D_MODEL), F32),
            jax.ShapeDtypeStruct((t * SUBLANES, LANES), F32),
            jax.ShapeDtypeStruct((t, LANES), F32),
            jax.ShapeDtypeStruct((SUBLANES, LANES), F32),
        ],
        scratch_shapes=[pltpu.VMEM((SUBLANES, LANES), F32)],
        compiler_params=_params(1),
        name="mix_router",
    )(*os_, *lses, ys, proj, proj, x2d, *consts)


def _n_moe_tiles(t):
    return t * TOP_K // MOE_TILE + N_EXPERTS


def _dispatch_kernel(counts_ref, ids_ref, ranks_ref, h2_ref, xs_ref, pos_ref, texp_ref,
                     off_ref, sem):
    tm = h2_ref.shape[0] // SUBLANES
    n_tiles = texp_ref.shape[0] - 1

    @pl.when(pl.program_id(0) == 0)
    def _():
        def per_expert(e, start):
            off_ref[e] = start
            n_t = (counts_ref[e] + (MOE_TILE - 1)) // MOE_TILE

            def mark(ti, c):
                texp_ref[start // MOE_TILE + ti] = e
                return c

            lax.fori_loop(0, n_t, mark, 0)
            return start + n_t * MOE_TILE

        end = lax.fori_loop(0, N_EXPERTS, per_expert, 0)
        n_active = end // MOE_TILE
        texp_ref[n_tiles] = n_active

        def fill(ti, c):
            texp_ref[ti] = texp_ref[n_active - 1]
            return c

        lax.fori_loop(n_active, n_tiles, fill, 0)

    def copy(j, k):
        p = off_ref[ids_ref[j * TOP_K + k]] + ranks_ref[j * TOP_K + k]
        return p, pltpu.make_async_copy(
            h2_ref.at[pl.ds(pl.multiple_of(j * SUBLANES, SUBLANES), SUBLANES), :],
            xs_ref.at[pl.ds(pl.multiple_of(p * SUBLANES, SUBLANES), SUBLANES), :], sem)

    def issue(j, c):
        for k in range(TOP_K):
            p, cp = copy(j, k)
            pos_ref[j * TOP_K + k] = p
            cp.start()
        return c

    lax.fori_loop(0, tm, issue, 0)

    def drain(j, c):
        for k in range(TOP_K):
            copy(j, k)[1].wait()
        return c

    lax.fori_loop(0, tm, drain, 0)


def _dispatch(counts, ids, ranks, h2_tiles):
    t = ids.shape[0] // TOP_K
    tm = ROW_TILE
    n_tiles = _n_moe_tiles(t)
    smem_blk = lambda n: pl.BlockSpec((n,), lambda i: (i,), memory_space=pltpu.SMEM)
    return pl.pallas_call(
        _dispatch_kernel,
        grid=(t // tm,),
        in_specs=[
            pl.BlockSpec((N_EXPERTS,), lambda i: (0,), memory_space=pltpu.SMEM),
            smem_blk(tm * TOP_K),
            smem_blk(tm * TOP_K),
            pl.BlockSpec((tm * SUBLANES, LANES), lambda i: (i, 0)),
        ],
        out_specs=[
            pl.BlockSpec(memory_space=pl.ANY),
            smem_blk(tm * TOP_K),
            pl.BlockSpec((n_tiles + 1,), lambda i: (0,), memory_space=pltpu.SMEM),
        ],
        out_shape=[
            jax.ShapeDtypeStruct((n_tiles * MOE_TILE * SUBLANES, LANES), F32),
            jax.ShapeDtypeStruct((t * TOP_K,), jnp.int32),
            jax.ShapeDtypeStruct((n_tiles + 1,), jnp.int32),
        ],
        scratch_shapes=[pltpu.SMEM((N_EXPERTS,), jnp.int32), pltpu.SemaphoreType.DMA],
        compiler_params=_params(1),
        name="moe_dispatch",
    )(counts, ids, ranks, h2_tiles)


def _ffn_kernel(texp_ref, xs_ref, wgu_ref, bgu_ref, wd_ref, bd_ref, out_ref, wgu_bf, wd_bf, last_ref):
    i = pl.program_id(0)
    tm = MOE_TILE
    n_active = texp_ref[pl.num_programs(0)]
    e = texp_ref[i]

    @pl.when(i == 0)
    def _():
        last_ref[0] = -1

    @pl.when(jnp.logical_and(i < n_active, e != last_ref[0]))
    def _():
        wgu_bf[...] = wgu_ref[0].astype(BF16)
        wd_bf[...] = wd_ref[0].astype(BF16)
        last_ref[0] = e

    @pl.when(i < n_active)
    def _():
        x = jnp.concatenate(
            [xs_ref[pl.ds(s, tm, stride=SUBLANES), :] for s in range(D_MODEL // LANES)], axis=-1)
        gu = jnp.dot(x.astype(BF16), wgu_bf[...], preferred_element_type=F32) + bgu_ref[0]
        x_glu = jnp.minimum(gu[:, :D_FF], SWIGLU_LIMIT)
        x_lin = jnp.clip(gu[:, D_FF:], -SWIGLU_LIMIT, SWIGLU_LIMIT)
        act = x_glu * jax.nn.sigmoid(SWIGLU_ALPHA * x_glu) * (x_lin + 1.0)
        out = jnp.dot(act.astype(BF16), wd_bf[...], preferred_element_type=F32) + bd_ref[0]
        for s in range(D_MODEL // LANES):
            out_ref[pl.ds(s, tm, stride=SUBLANES), :] = out[:, s * LANES:(s + 1) * LANES]


def _expert_ffn(texp, xs_tiles, w_gu, b_gu, w_down, b_down):
    n_tiles = texp.shape[0] - 1
    tm = MOE_TILE
    grid_spec = pltpu.PrefetchScalarGridSpec(
        num_scalar_prefetch=1,
        grid=(n_tiles,),
        in_specs=[
            pl.BlockSpec((tm * SUBLANES, LANES), lambda i, te: (jnp.minimum(i, te[n_tiles] - 1), 0)),
            pl.BlockSpec((1, D_MODEL, 2 * D_FF), lambda i, te: (te[i], 0, 0)),
            pl.BlockSpec((1, 1, 2 * D_FF), lambda i, te: (te[i], 0, 0)),
            pl.BlockSpec((1, D_FF, D_MODEL), lambda i, te: (te[i], 0, 0)),
            pl.BlockSpec((1, 1, D_MODEL), lambda i, te: (te[i], 0, 0)),
        ],
        out_specs=pl.BlockSpec((tm * SUBLANES, LANES), lambda i, te: (i, 0)),
        scratch_shapes=[
            pltpu.VMEM((D_MODEL, 2 * D_FF), BF16),
            pltpu.VMEM((D_FF, D_MODEL), BF16),
            pltpu.SMEM((1,), jnp.int32),
        ],
    )
    return pl.pallas_call(
        _ffn_kernel,
        grid_spec=grid_spec,
        out_shape=jax.ShapeDtypeStruct((n_tiles * tm * SUBLANES, LANES), F32),
        compiler_params=_params(1),
        name="moe_ffn",
    )(texp, xs_tiles, w_gu, b_gu.reshape(N_EXPERTS, 1, 2 * D_FF), w_down,
      b_down.reshape(N_EXPERTS, 1, D_MODEL))


def _combine_kernel(pos_ref, route_ref, x1_ref, ys_ref, out_ref, buf_ref, sem):
    tm = x1_ref.shape[0]

    def copy(j, k):
        p = pos_ref[j * TOP_K + k]
        return pltpu.make_async_copy(
            ys_ref.at[pl.ds(pl.multiple_of(p * SUBLANES, SUBLANES), SUBLANES), :],
            buf_ref.at[k, pl.ds(pl.multiple_of(j * SUBLANES, SUBLANES), SUBLANES), :], sem)

    def issue(j, c):
        for k in range(TOP_K):
            copy(j, k).start()
        return c

    lax.fori_loop(0, tm, issue, 0)

    def drain(j, c):
        for k in range(TOP_K):
            copy(j, k).wait()
        return c

    lax.fori_loop(0, tm, drain, 0)

    route = route_ref[...]
    acc = x1_ref[...]
    for k in range(TOP_K):
        gate = route[:, ROUTE_GATES + k:ROUTE_GATES + k + 1]
        rows = jnp.concatenate(
            [buf_ref[k, pl.ds(s, tm, stride=SUBLANES), :] for s in range(D_MODEL // LANES)], axis=-1)
        acc = acc + gate * rows
    out_ref[...] = acc


def _combine(pos, route, x1, ys_tiles):
    t = x1.shape[0]
    tm = ROW_TILE
    return pl.pallas_call(
        _combine_kernel,
        grid=(t // tm,),
        in_specs=[
            pl.BlockSpec((tm * TOP_K,), lambda i: (i,), memory_space=pltpu.SMEM),
            pl.BlockSpec((tm, LANES), lambda i: (i, 0)),
            pl.BlockSpec((tm, D_MODEL), lambda i: (i, 0)),
            pl.BlockSpec(memory_space=pl.ANY),
        ],
        out_specs=pl.BlockSpec((tm, D_MODEL), lambda i: (i, 0)),
        out_shape=jax.ShapeDtypeStruct((t, D_MODEL), F32),
        scratch_shapes=[pltpu.VMEM((TOP_K, tm * SUBLANES, LANES), F32), pltpu.SemaphoreType.DMA],
        compiler_params=_params(1),
        name="moe_combine",
    )(pos, route, x1, ys_tiles)


def _layer(x, g_mix, w_in, q_gain, k_gain, rel_bias, ssm_a_re, ssm_a_im, ssm_log_dt, ssm_b_re,
           ssm_b_im, ssm_c_re, ssm_c_im, ssm_d, w_glu, b_glu, w_attn_proj, w_ssm_proj, w_out,
           g_ffn, w_router, b_router, w_gate_up, b_gate_up, w_down, b_down):
    b, s, _ = x.shape
    x2d = x.reshape(b * s, D_MODEL)

    proj = _inproj(x2d, g_mix, w_in.astype(BF16))
    proj3 = proj.reshape(b, s, IN_WIDTH)

    head_of = np.arange(GROUP_WIDTH) // HEAD_DIM
    seg = jnp.asarray(head_of[:, None] == head_of[None, :], BF16)
    os_, lses = [], []
    for gi, (window, dil) in enumerate(ATTN_GROUPS):
        table = rel_bias[:, gi * HEADS_PER_GROUP:(gi + 1) * HEADS_PER_GROUP]
        o, lse = _attention_group(proj3, gi, _attn_bias(table, window, dil), q_gain[gi], k_gain[gi], seg)
        os_.append(o)
        lses.append(lse)

    tab, bmat, cmat = _ssm_tables(ssm_a_re, ssm_a_im, ssm_log_dt, ssm_b_re, ssm_b_im, ssm_c_re, ssm_c_im)
    ys = _ssm(proj3, tab, bmat, cmat, ssm_d)

    expand = jnp.asarray(np.arange(LANES)[:, None] == head_of[None, :], BF16)
    pad_e = LANES - N_EXPERTS
    weights = dict(
        expand=expand, wa=w_attn_proj.astype(BF16), wglu=w_glu.astype(BF16),
        bglu=b_glu.reshape(1, SSM_WIDTH), ws=w_ssm_proj.astype(BF16), wo=w_out.astype(BF16),
        gffn=g_ffn.reshape(1, D_MODEL), wr=jnp.pad(w_router, ((0, 0), (0, pad_e))).astype(BF16),
        br=jnp.pad(b_router, (0, pad_e)).reshape(1, LANES))
    x1, h2_tiles, route, counts = _mix(os_, lses, ys, proj, x2d, weights)

    ids = route[:, ROUTE_IDS:ROUTE_IDS + TOP_K].astype(jnp.int32).reshape(-1)
    ranks = route[:, ROUTE_RANKS:ROUTE_RANKS + TOP_K].astype(jnp.int32).reshape(-1)
    counts_i = counts[0, :N_EXPERTS].astype(jnp.int32)
    xs_tiles, pos, texp = _dispatch(counts_i, ids, ranks, h2_tiles)
    ys_tiles = _expert_ffn(texp, xs_tiles, w_gate_up, b_gate_up, w_down, b_down)
    out = _combine(pos, route, x1, ys_tiles)
    return out.reshape(b, s, D_MODEL)


_layer_jit = jax.jit(_layer)


def kernel(x, g_mix, w_in, q_gain, k_gain, rel_bias, ssm_a_re, ssm_a_im, ssm_log_dt, ssm_b_re, ssm_b_im, ssm_c_re, ssm_c_im, ssm_d, w_glu, b_glu, w_attn_proj, w_ssm_proj, w_out, g_ffn, w_router, b_router, w_gate_up, b_gate_up, w_down, b_down):
    return _layer_jit(x, g_mix[0], w_in[0], q_gain[0], k_gain[0], rel_bias, ssm_a_re[0], ssm_a_im[0],
                      ssm_log_dt[0], ssm_b_re[0], ssm_b_im[0], ssm_c_re[0], ssm_c_im[0], ssm_d[0],
                      w_glu[0], b_glu[0], w_attn_proj[0], w_ssm_proj[0], w_out[0], g_ffn[0],
                      w_router[0], b_router[0], w_gate_up[0], b_gate_up[0], w_down[0], b_down[0])
```

```python
import functools
import math

import jax
import jax.numpy as jnp
import numpy as np
from jax import lax
from jax.experimental import pallas as pl
from jax.experimental.pallas import tpu as pltpu

F32 = jnp.float32
BF16 = jnp.bfloat16

D_MODEL = 1024
HEAD_DIM = 64
ATTN_GROUPS = ((128, 1), (512, 4), (2048, 16))
HEADS_PER_GROUP = 8
GROUP_WIDTH = HEADS_PER_GROUP * HEAD_DIM
N_ATTN_HEADS = len(ATTN_GROUPS) * HEADS_PER_GROUP
QKV_WIDTH = 3 * N_ATTN_HEADS * HEAD_DIM
ATTN_BLOCK = 128
REL_BUCKETS = 32
REL_MAX_DIST = 2048
SSM_WIDTH = 512
SSM_GROUP = 16
SSM_GROUPS = 32
SSM_STATE = 64
IN_WIDTH = QKV_WIDTH + SSM_WIDTH + 2 * D_MODEL
N_EXPERTS = 32
TOP_K = 4
D_FF = 1024
SWIGLU_LIMIT = 7.0
SWIGLU_ALPHA = 1.702
EPS = 1e-6

SUBLANES = 8
LANES = 128
MASK_VALUE = -1e30
VMEM_LIMIT = 56 * 1024 * 1024

QKV_GROUP_WIDTH = 3 * GROUP_WIDTH
REST_WIDTH = 2 * D_MODEL + SSM_WIDTH
GATE_A_BLOCK = 0
GATE_B_BLOCK = 1
U_BLOCK = 2 * D_MODEL // SSM_WIDTH

ROW_TILE = 256
SSM_TIME_TILE = 512
MOE_TILE = 256


def _params(n_axes, vmem=VMEM_LIMIT):
    return pltpu.CompilerParams(dimension_semantics=("arbitrary",) * n_axes, vmem_limit_bytes=vmem)


def _inproj_columns():
    n_g = len(ATTN_GROUPS)
    cols = []
    for gi in range(n_g):
        for part in range(3):
            start = (part * n_g + gi) * GROUP_WIDTH
            cols.append(np.arange(start, start + GROUP_WIDTH))
    cols.append(np.arange(QKV_WIDTH + SSM_WIDTH, IN_WIDTH))
    cols.append(np.arange(QKV_WIDTH, QKV_WIDTH + SSM_WIDTH))
    return np.concatenate(cols)


def _inproj_kernel(x_ref, g_ref, w_ref, q0_ref, q1_ref, q2_ref, rest_ref, stage_ref):
    x = x_ref[0]
    tm = x.shape[0]
    ms = jnp.mean(x * x, axis=-1, keepdims=True)
    h = x * lax.rsqrt(ms + EPS) * g_ref[...]
    res = jnp.dot(h.astype(BF16), w_ref[...], preferred_element_type=F32)
    w = QKV_GROUP_WIDTH
    q0_ref[0, 0] = res[:, 0:w]
    rest_ref[0] = res[:, 3 * w:]
    n_blk = w // LANES
    for c in range(2 * n_blk):
        stage_ref[c] = res[:, w + c * LANES:w + (c + 1) * LANES]
    for out_ref, gi in ((q1_ref, 1), (q2_ref, 2)):
        dil = ATTN_GROUPS[gi][1]
        for r in range(dil):
            for c in range(n_blk):
                out_ref[0, r, :, c * LANES:(c + 1) * LANES] = stage_ref[
                    (gi - 1) * n_blk + c, pl.ds(r, tm // dil, stride=dil), :]


def _inproj(x, g, w_bf16):
    b, s, _ = x.shape
    tm = ROW_TILE
    dils = [d for _, d in ATTN_GROUPS]
    out_specs = [pl.BlockSpec((1, d, tm // d, QKV_GROUP_WIDTH), lambda bi, i: (bi, 0, i, 0)) for d in dils]
    out_shape = [jax.ShapeDtypeStruct((b, d, s // d, QKV_GROUP_WIDTH), F32) for d in dils]
    return pl.pallas_call(
        _inproj_kernel,
        grid=(b, s // tm),
        in_specs=[
            pl.BlockSpec((1, tm, D_MODEL), lambda bi, i: (bi, i, 0)),
            pl.BlockSpec((1, D_MODEL), lambda bi, i: (0, 0)),
            pl.BlockSpec((D_MODEL, IN_WIDTH), lambda bi, i: (0, 0), pipeline_mode=pl.Buffered(1)),
        ],
        out_specs=out_specs + [pl.BlockSpec((1, tm, REST_WIDTH), lambda bi, i: (bi, i, 0))],
        out_shape=out_shape + [jax.ShapeDtypeStruct((b, s, REST_WIDTH), F32)],
        scratch_shapes=[pltpu.VMEM((2 * QKV_GROUP_WIDTH // LANES, tm, LANES), F32)],
        compiler_params=_params(2),
        name="inproj",
    )(x, g.reshape(1, D_MODEL), w_bf16)


def _rel_bucket_np(dist):
    max_exact = REL_BUCKETS // 2
    d = np.maximum(dist, 1).astype(np.float32)
    large = max_exact + (np.log(d / max_exact) / math.log(REL_MAX_DIST / max_exact)
                         * (REL_BUCKETS - max_exact)).astype(np.int32)
    large = np.minimum(large, REL_BUCKETS - 1)
    return np.where(dist < max_exact, dist, large).astype(np.int32)


def _attn_bias(table, window, dilation):
    blk = ATTN_BLOCK
    span = window // dilation
    qi = np.arange(blk)[:, None]
    ki = np.arange(2 * blk)[None, :]
    steps = qi + blk - ki
    band = (steps >= 0) & (steps <= span)
    bucket = _rel_bucket_np(np.clip(steps, 0, None) * dilation)
    onehot = np.eye(REL_BUCKETS, dtype=np.float32)[bucket]
    bias = jnp.einsum('qkb,bh->hqk', onehot, table.astype(F32), precision=lax.Precision.HIGHEST)
    b0 = jnp.where(band[None], bias, MASK_VALUE)
    b1 = jnp.where((band & (ki >= blk))[None], bias, MASK_VALUE)
    return jnp.stack([b0, b1], axis=0)


def _attn_kernel(q_ref, k_ref, v_ref, bias_ref, qg_ref, kg_ref, seg_ref, o_ref, lse_ref,
                 kcat_ref, vcat_ref):
    blk = ATTN_BLOCK
    i = pl.program_id(2)
    seg = seg_ref[...]

    def head_norm(t, gain):
        ss = _split_dot(t * t, seg)
        return t * lax.rsqrt(ss * (1.0 / HEAD_DIM) + EPS) * gain

    qn = (head_norm(q_ref[0, 0], qg_ref[...]) * (HEAD_DIM ** -0.5)).astype(BF16)

    @pl.when(i == 0)
    def _():
        kcat_ref[0:blk, :] = jnp.zeros((blk, GROUP_WIDTH), BF16)
        vcat_ref[0:blk, :] = jnp.zeros((blk, GROUP_WIDTH), BF16)

    @pl.when(i > 0)
    def _():
        kcat_ref[0:blk, :] = kcat_ref[blk:2 * blk, :]
        vcat_ref[0:blk, :] = vcat_ref[blk:2 * blk, :]

    kcat_ref[blk:2 * blk, :] = head_norm(k_ref[0, 0], kg_ref[...]).astype(BF16)
    vcat_ref[blk:2 * blk, :] = v_ref[0, 0].astype(BF16)
    first = (i == 0).astype(jnp.int32)

    lane = lax.broadcasted_iota(jnp.int32, (blk, LANES), 1)
    low_half = lane < HEAD_DIM
    lse_tile = jnp.zeros((blk, LANES), F32)
    outs = []
    for p in range(GROUP_WIDTH // LANES):
        qp = qn[:, p * LANES:(p + 1) * LANES]
        kp = kcat_ref[:, p * LANES:(p + 1) * LANES]
        vp = vcat_ref[:, p * LANES:(p + 1) * LANES]
        pair = []
        for hh in range(2):
            head = 2 * p + hh
            sel = low_half if hh == 0 else jnp.logical_not(low_half)
            qm = jnp.where(sel, qp, jnp.zeros_like(qp))
            s = lax.dot_general(qm, kp, (((1,), (1,)), ((), ())), preferred_element_type=F32)
            s = s + bias_ref[first, head]
            mx = jnp.max(s, axis=-1, keepdims=True)
            pe = jnp.exp(s - mx)
            den = jnp.sum(pe, axis=-1, keepdims=True)
            o = jnp.dot(pe.astype(BF16), vp, preferred_element_type=F32)
            pair.append(o * (1.0 / den))
            lse_tile = jnp.where(lane == head, mx + jnp.log(den), lse_tile)
        outs.append(jnp.where(low_half, pair[0], pair[1]))
    o_ref[0, 0] = jnp.concatenate(outs, axis=-1)
    lse_ref[0, 0] = lse_tile


def _attention_group(qkv, gi, bias, q_gain, k_gain, seg):
    b, dil, length, _ = qkv.shape
    blk = ATTN_BLOCK
    nblk = length // blk

    def col(c):
        return pl.BlockSpec((1, 1, blk, GROUP_WIDTH), lambda bi, r, i: (bi, r, i, c))

    const = lambda shape: pl.BlockSpec(shape, lambda bi, r, i: (0,) * len(shape))
    return pl.pallas_call(
        _attn_kernel,
        grid=(b, dil, nblk),
        in_specs=[
            col(0), col(1), col(2),
            const((2, HEADS_PER_GROUP, blk, 2 * blk)),
            const((1, GROUP_WIDTH)),
            const((1, GROUP_WIDTH)),
            const((GROUP_WIDTH, GROUP_WIDTH)),
        ],
        out_specs=[
            pl.BlockSpec((1, 1, blk, GROUP_WIDTH), lambda bi, r, i: (bi, r, i, 0)),
            pl.BlockSpec((1, 1, blk, LANES), lambda bi, r, i: (bi, r, i, 0)),
        ],
        out_shape=[
            jax.ShapeDtypeStruct((b, dil, length, GROUP_WIDTH), F32),
            jax.ShapeDtypeStruct((b, dil, length, LANES), F32),
        ],
        scratch_shapes=[
            pltpu.VMEM((2 * blk, GROUP_WIDTH), BF16),
            pltpu.VMEM((2 * blk, GROUP_WIDTH), BF16),
        ],
        compiler_params=_params(3),
        name=f"attn_g{gi}",
    )(qkv, qkv, qkv, bias,
      jnp.tile(q_gain, HEADS_PER_GROUP).reshape(1, GROUP_WIDTH),
      jnp.tile(k_gain, HEADS_PER_GROUP).reshape(1, GROUP_WIDTH), seg)


HALF_STATE = SSM_GROUPS * SSM_STATE // 2
SCAN_COLS = 4


def _ssm_tables(a_re, a_im, log_dt, b_re, b_im, c_re, c_im):
    lam = lax.complex(a_re.astype(F32), a_im.astype(F32))
    dt = jnp.exp(log_dt.astype(F32))[:, None]
    a_bar = jnp.exp(lam * dt)
    b_bar = ((a_bar - 1.0) / lam)[..., None] * lax.complex(b_re.astype(F32), b_im.astype(F32))

    rows = jnp.arange(SUBLANES)
    tabs = []
    for s in (1, 2, 4):
        p = jnp.exp(lam * dt * s).reshape(1, -1)
        tabs.append(jnp.where((rows >= s)[:, None], p, 0.0))
    tabs.append(jnp.exp((lam * dt).reshape(1, -1) * (rows + 1.0)[:, None]))
    tab = jnp.stack(tabs, axis=0)
    tab = jnp.stack([tab.real, tab.imag], axis=1).astype(F32)

    gh = SSM_GROUPS // 2
    eye = jnp.eye(gh, dtype=F32)

    def in_mat(m):
        return jnp.einsum('gnc,gh->gchn', m, eye).reshape(gh * SSM_GROUP, gh * SSM_STATE)

    def out_mat(m):
        return jnp.einsum('gcn,gh->gnhc', m, eye).reshape(gh * SSM_STATE, gh * SSM_GROUP)

    b_mats, c_mats = [], []
    for h in range(2):
        sl = slice(h * gh, (h + 1) * gh)
        b_mats.append(jnp.concatenate([in_mat(b_bar.real[sl]), in_mat(b_bar.imag[sl])], axis=1))
        c_mats.append(jnp.concatenate([out_mat(c_re.astype(F32)[sl]),
                                       -out_mat(c_im.astype(F32)[sl])], axis=0))
    return tab, jnp.stack(b_mats).astype(BF16), jnp.stack(c_mats).astype(BF16)


def _ssm_kernel(u_ref, tab_ref, bmat_ref, cmat_ref, d_ref, y_ref, xs_ref, carry_ref):
    tt = u_ref.shape[1]
    half_w = 2 * HALF_STATE

    @pl.when(pl.program_id(1) == 0)
    def _():
        carry_ref[...] = jnp.zeros_like(carry_ref)

    u = u_ref[0]
    ub = u.astype(BF16)
    hw = SSM_WIDTH // 2
    for h in range(2):
        xs_ref[:, h * half_w:(h + 1) * half_w] = jnp.dot(
            ub[:, h * hw:(h + 1) * hw], bmat_ref[h], preferred_element_type=F32)

    n_cols = 2 * HALF_STATE // LANES
    for c0 in range(0, n_cols, SCAN_COLS):
        cols = []
        for c in range(c0, c0 + SCAN_COLS):
            h, j = divmod(c, HALF_STATE // LANES)
            re_col = h * half_w + j * LANES
            cols.append((c * LANES, re_col, re_col + HALF_STATE))

        def body(rb, carry, cols=cols):
            r0 = pl.multiple_of(rb * SUBLANES, SUBLANES)
            new = []
            for (tc, rc, ic), (cr, ci) in zip(cols, carry):
                re = xs_ref[pl.ds(r0, SUBLANES), rc:rc + LANES]
                im = xs_ref[pl.ds(r0, SUBLANES), ic:ic + LANES]
                for lvl, s in enumerate((1, 2, 4)):
                    ar = tab_ref[lvl, 0, :, tc:tc + LANES]
                    ai = tab_ref[lvl, 1, :, tc:tc + LANES]
                    sr = pltpu.roll(re, s, 0)
                    si = pltpu.roll(im, s, 0)
                    re, im = re + ar * sr - ai * si, im + ar * si + ai * sr
                pr = tab_ref[3, 0, :, tc:tc + LANES]
                pi = tab_ref[3, 1, :, tc:tc + LANES]
                re, im = re + pr * cr - pi * ci, im + pr * ci + pi * cr
                xs_ref[pl.ds(r0, SUBLANES), rc:rc + LANES] = re
                xs_ref[pl.ds(r0, SUBLANES), ic:ic + LANES] = im
                new.append((jnp.broadcast_to(re[SUBLANES - 1:, :], (SUBLANES, LANES)),
                            jnp.broadcast_to(im[SUBLANES - 1:, :], (SUBLANES, LANES))))
            return tuple(new)

        init = tuple((carry_ref[:, rc:rc + LANES], carry_ref[:, ic:ic + LANES]) for _, rc, ic in cols)
        fin = lax.fori_loop(0, tt // SUBLANES, body, init)
        for (_, rc, ic), (cr, ci) in zip(cols, fin):
            carry_ref[:, rc:rc + LANES] = cr
            carry_ref[:, ic:ic + LANES] = ci

    ys = []
    for h in range(2):
        xh = xs_ref[:, h * half_w:(h + 1) * half_w].astype(BF16)
        ys.append(jnp.dot(xh, cmat_ref[h], preferred_element_type=F32))
    y_ref[0] = jnp.concatenate(ys, axis=-1) + d_ref[...] * u


def _ssm(rest, tab, bmat, cmat, d_skip):
    b, s, _ = rest.shape
    tt = min(SSM_TIME_TILE, s)
    n_state = 4 * HALF_STATE
    return pl.pallas_call(
        _ssm_kernel,
        grid=(b, s // tt),
        in_specs=[
            pl.BlockSpec((1, tt, SSM_WIDTH), lambda bi, j: (bi, j, U_BLOCK)),
            pl.BlockSpec(tab.shape, lambda bi, j: (0, 0, 0, 0)),
            pl.BlockSpec(bmat.shape, lambda bi, j: (0, 0, 0)),
            pl.BlockSpec(cmat.shape, lambda bi, j: (0, 0, 0)),
            pl.BlockSpec((1, SSM_WIDTH), lambda bi, j: (0, 0)),
        ],
        out_specs=pl.BlockSpec((1, tt, SSM_WIDTH), lambda bi, j: (bi, j, 0)),
        out_shape=jax.ShapeDtypeStruct((b, s, SSM_WIDTH), F32),
        scratch_shapes=[
            pltpu.VMEM((tt, n_state), F32),
            pltpu.VMEM((SUBLANES, n_state), F32),
        ],
        compiler_params=_params(2),
        name="ssm",
    )(rest, tab, bmat, cmat, d_skip.reshape(1, SSM_WIDTH)).reshape(b * s, SSM_WIDTH)


ROUTE_IDS, ROUTE_GATES, ROUTE_RANKS = 0, TOP_K, 2 * TOP_K


def _split_dot(x, w_bf16):
    hi = x.astype(BF16)
    lo = (x - hi.astype(F32)).astype(BF16)
    return (jnp.dot(hi, w_bf16, preferred_element_type=F32)
            + jnp.dot(lo, w_bf16, preferred_element_type=F32))


def _mix_kernel(o0_ref, o1_ref, o2_ref, l0_ref, l1_ref, l2_ref, ys_ref, ga_ref, gb_ref, x_ref,
                expand_ref, wa_ref, wglu_ref, bglu_ref, ws_ref, wo_ref, gffn_ref, wr_ref, br_ref,
                x1_ref, h2_ref, route_ref, counts_ref, run_ref, ostage_ref, lstage_ref):
    tm = x_ref.shape[0]

    @pl.when(jnp.logical_and(pl.program_id(0) == 0, pl.program_id(1) == 0))
    def _():
        run_ref[...] = jnp.zeros_like(run_ref)

    for slot, (o_ref, l_ref) in enumerate(((o1_ref, l1_ref), (o2_ref, l2_ref))):
        dil = ATTN_GROUPS[slot + 1][1]
        for r in range(dil):
            for c in range(GROUP_WIDTH // LANES):
                ostage_ref[slot, c, pl.ds(r, tm // dil, stride=dil), :] = o_ref[
                    0, r, :, c * LANES:(c + 1) * LANES]
            lstage_ref[slot, pl.ds(r, tm // dil, stride=dil), :] = l_ref[0, r]
    group_out = [o0_ref[0, 0]] + [
        jnp.concatenate([ostage_ref[slot, c] for c in range(GROUP_WIDTH // LANES)], axis=-1)
        for slot in range(2)]

    lses = [l0_ref[0, 0], lstage_ref[0], lstage_ref[1]]
    mx = jnp.maximum(jnp.maximum(lses[0], lses[1]), lses[2])
    es = [jnp.exp(l - mx) for l in lses]
    inv = 1.0 / (es[0] + es[1] + es[2])
    expand = expand_ref[...]
    attn = jnp.zeros((tm, GROUP_WIDTH), F32)
    for e, o_g in zip(es, group_out):
        attn = attn + _split_dot(e * inv, expand) * o_g
    y_a = jnp.dot(attn.astype(BF16), wa_ref[...], preferred_element_type=F32)

    ys = ys_ref[...]
    ys = 0.5 * ys * (1.0 + jnp.tanh(math.sqrt(2.0 / math.pi) * (ys + 0.044715 * (ys * ys * ys))))
    glu = jnp.dot(ys.astype(BF16), wglu_ref[...], preferred_element_type=F32) + bglu_ref[...]
    ys = ys * jax.nn.sigmoid(glu)
    y_b = jnp.dot(ys.astype(BF16), ws_ref[...], preferred_element_type=F32)

    mixed = jax.nn.sigmoid(ga_ref[...]) * y_a + jax.nn.sigmoid(gb_ref[...]) * y_b
    x1 = x_ref[...] + jnp.dot(mixed.astype(BF16), wo_ref[...], preferred_element_type=F32)
    x1_ref[...] = x1

    ms = jnp.mean(x1 * x1, axis=-1, keepdims=True)
    h2 = x1 * lax.rsqrt(ms + EPS) * gffn_ref[...]
    for s in range(D_MODEL // LANES):
        h2_ref[pl.ds(s, tm, stride=SUBLANES), :] = h2[:, s * LANES:(s + 1) * LANES]

    logits = jnp.dot(h2.astype(BF16), wr_ref[...], preferred_element_type=F32) + br_ref[...]
    lane = lax.broadcasted_iota(jnp.int32, (tm, LANES), 1)
    work = jnp.where(lane < N_EXPERTS, logits, -jnp.inf)
    sel_mask = jnp.zeros((tm, LANES), F32)
    route = jnp.zeros((tm, LANES), F32)
    vals, hots = [], []
    for k in range(TOP_K):
        v = jnp.max(work, axis=-1, keepdims=True)
        idx = jnp.min(jnp.where(work == v, lane, LANES), axis=-1, keepdims=True)
        hot = lane == idx
        work = jnp.where(hot, -jnp.inf, work)
        sel_mask = jnp.where(hot, 1.0, sel_mask)
        route = jnp.where(lane == ROUTE_IDS + k, idx.astype(F32), route)
        vals.append(v)
        hots.append(hot)
    exps = [jnp.exp(v - vals[0]) for v in vals]
    inv_den = 1.0 / (exps[0] + exps[1] + exps[2] + exps[3])

    r_i = lax.broadcasted_iota(jnp.int32, (tm, tm), 0)
    c_i = lax.broadcasted_iota(jnp.int32, (tm, tm), 1)
    tri = jnp.where(c_i < r_i, 1.0, 0.0).astype(BF16)
    before = jnp.dot(tri, sel_mask.astype(BF16), preferred_element_type=F32) + run_ref[0:1, :]
    for k in range(TOP_K):
        rank = jnp.sum(jnp.where(hots[k], before, 0.0), axis=-1, keepdims=True)
        route = jnp.where(lane == ROUTE_GATES + k, exps[k] * inv_den, route)
        route = jnp.where(lane == ROUTE_RANKS + k, rank, route)
    route_ref[...] = route
    total = run_ref[0:1, :] + jnp.sum(sel_mask, axis=0, keepdims=True)
    run_ref[...] = jnp.broadcast_to(total, run_ref.shape)
    counts_ref[...] = jnp.broadcast_to(total, counts_ref.shape)


def _mix(os_, lses, ys, rest, x2d, w):
    b, s, _ = rest.shape
    t = b * s
    tm = ROW_TILE
    n_i = s // tm
    rest2d = rest.reshape(t, REST_WIDTH)
    row = lambda width, blk=0: pl.BlockSpec((tm, width), lambda bi, i: (bi * n_i + i, blk))
    const = lambda a: pl.BlockSpec(a.shape, lambda bi, i: (0,) * a.ndim)
    grouped = lambda width: [pl.BlockSpec((1, d, tm // d, width), lambda bi, i: (bi, 0, i, 0))
                             for _, d in ATTN_GROUPS]
    consts = [w["expand"], w["wa"], w["wglu"], w["bglu"], w["ws"], w["wo"], w["gffn"], w["wr"], w["br"]]
    n_dilated = len(ATTN_GROUPS) - 1
    return pl.pallas_call(
        _mix_kernel,
        grid=(b, n_i),
        in_specs=grouped(GROUP_WIDTH) + grouped(LANES) + [
            row(SSM_WIDTH),
            row(D_MODEL, GATE_A_BLOCK),
            row(D_MODEL, GATE_B_BLOCK),
            row(D_MODEL),
        ] + [const(a) for a in consts],
        out_specs=[
            row(D_MODEL),
            pl.BlockSpec((tm * SUBLANES, LANES), lambda bi, i: (bi * n_i + i, 0)),
            row(LANES),
            pl.BlockSpec((SUBLANES, LANES), lambda bi, i: (0, 0)),
        ],
        out_shape=[
            jax.ShapeDtypeStruct((t, D_MODEL), F32),
            jax.ShapeDtypeStruct((t * SUBLANES, LANES), F32),
            jax.ShapeDtypeStruct((t, LANES), F32),
            jax.ShapeDtypeStruct((SUBLANES, LANES), F32),
        ],
        scratch_shapes=[
            pltpu.VMEM((SUBLANES, LANES), F32),
            pltpu.VMEM((n_dilated, GROUP_WIDTH // LANES, tm, LANES), F32),
            pltpu.VMEM((n_dilated, tm, LANES), F32),
        ],
        compiler_params=_params(2),
        name="mix_router",
    )(*os_, *lses, ys, rest2d, rest2d, x2d, *consts)


def _n_moe_tiles(t):
    return t * TOP_K // MOE_TILE + N_EXPERTS


def _dispatch_kernel(counts_ref, ids_ref, ranks_ref, h2_ref, xs_ref, pos_ref, texp_ref,
                     off_ref, sem):
    tm = h2_ref.shape[0] // SUBLANES
    n_tiles = texp_ref.shape[0] - 1

    @pl.when(pl.program_id(0) == 0)
    def _():
        def per_expert(e, start):
            off_ref[e] = start
            n_t = (counts_ref[e] + (MOE_TILE - 1)) // MOE_TILE

            def mark(ti, c):
                texp_ref[start // MOE_TILE + ti] = e
                return c

            lax.fori_loop(0, n_t, mark, 0)
            return start + n_t * MOE_TILE

        end = lax.fori_loop(0, N_EXPERTS, per_expert, 0)
        n_active = end // MOE_TILE
        texp_ref[n_tiles] = n_active

        def fill(ti, c):
            texp_ref[ti] = texp_ref[n_active - 1]
            return c

        lax.fori_loop(n_active, n_tiles, fill, 0)

    def copy(j, k):
        p = off_ref[ids_ref[j * TOP_K + k]] + ranks_ref[j * TOP_K + k]
        return p, pltpu.make_async_copy(
            h2_ref.at[pl.ds(pl.multiple_of(j * SUBLANES, SUBLANES), SUBLANES), :],
            xs_ref.at[pl.ds(pl.multiple_of(p * SUBLANES, SUBLANES), SUBLANES), :], sem)

    def issue(j, c):
        for k in range(TOP_K):
            p, cp = copy(j, k)
            pos_ref[j * TOP_K + k] = p
            cp.start()
        return c

    lax.fori_loop(0, tm, issue, 0)

    def drain(j, c):
        for k in range(TOP_K):
            copy(j, k)[1].wait()
        return c

    lax.fori_loop(0, tm, drain, 0)


def _dispatch(counts, ids, ranks, h2_tiles):
    t = ids.shape[0] // TOP_K
    tm = ROW_TILE
    n_tiles = _n_moe_tiles(t)
    smem_blk = lambda n: pl.BlockSpec((n,), lambda i: (i,), memory_space=pltpu.SMEM)
    return pl.pallas_call(
        _dispatch_kernel,
        grid=(t // tm,),
        in_specs=[
            pl.BlockSpec((N_EXPERTS,), lambda i: (0,), memory_space=pltpu.SMEM),
            smem_blk(tm * TOP_K),
            smem_blk(tm * TOP_K),
            pl.BlockSpec((tm * SUBLANES, LANES), lambda i: (i, 0)),
        ],
        out_specs=[
            pl.BlockSpec(memory_space=pl.ANY),
            smem_blk(tm * TOP_K),
            pl.BlockSpec((n_tiles + 1,), lambda i: (0,), memory_space=pltpu.SMEM),
        ],
        out_shape=[
            jax.ShapeDtypeStruct((n_tiles * MOE_TILE * SUBLANES, LANES), F32),
            jax.ShapeDtypeStruct((t * TOP_K,), jnp.int32),
            jax.ShapeDtypeStruct((n_tiles + 1,), jnp.int32),
        ],
        scratch_shapes=[pltpu.SMEM((N_EXPERTS,), jnp.int32), pltpu.SemaphoreType.DMA],
        compiler_params=_params(1),
        name="moe_dispatch",
    )(counts, ids, ranks, h2_tiles)


def _ffn_kernel(texp_ref, xs_ref, wgu_ref, bgu_ref, wd_ref, bd_ref, out_ref, wgu_bf, wd_bf, last_ref):
    i = pl.program_id(0)
    tm = MOE_TILE
    n_active = texp_ref[pl.num_programs(0)]
    e = texp_ref[i]

    @pl.when(i == 0)
    def _():
        last_ref[0] = -1

    @pl.when(jnp.logical_and(i < n_active, e != last_ref[0]))
    def _():
        wgu_bf[...] = wgu_ref[0].astype(BF16)
        wd_bf[...] = wd_ref[0].astype(BF16)
        last_ref[0] = e

    @pl.when(i < n_active)
    def _():
        x = jnp.concatenate(
            [xs_ref[pl.ds(s, tm, stride=SUBLANES), :] for s in range(D_MODEL // LANES)], axis=-1)
        gu = jnp.dot(x.astype(BF16), wgu_bf[...], preferred_element_type=F32) + bgu_ref[0]
        x_glu = jnp.minimum(gu[:, :D_FF], SWIGLU_LIMIT)
        x_lin = jnp.clip(gu[:, D_FF:], -SWIGLU_LIMIT, SWIGLU_LIMIT)
        act = x_glu * jax.nn.sigmoid(SWIGLU_ALPHA * x_glu) * (x_lin + 1.0)
        out = jnp.dot(act.astype(BF16), wd_bf[...], preferred_element_type=F32) + bd_ref[0]
        for s in range(D_MODEL // LANES):
            out_ref[pl.ds(s, tm, stride=SUBLANES), :] = out[:, s * LANES:(s + 1) * LANES]


def _expert_ffn(texp, xs_tiles, w_gu, b_gu, w_down, b_down):
    n_tiles = texp.shape[0] - 1
    tm = MOE_TILE
    grid_spec = pltpu.PrefetchScalarGridSpec(
        num_scalar_prefetch=1,
        grid=(n_tiles,),
        in_specs=[
            pl.BlockSpec((tm * SUBLANES, LANES), lambda i, te: (jnp.minimum(i, te[n_tiles] - 1), 0)),
            pl.BlockSpec((1, D_MODEL, 2 * D_FF), lambda i, te: (te[i], 0, 0)),
            pl.BlockSpec((1, 1, 2 * D_FF), lambda i, te: (te[i], 0, 0)),
            pl.BlockSpec((1, D_FF, D_MODEL), lambda i, te: (te[i], 0, 0)),
            pl.BlockSpec((1, 1, D_MODEL), lambda i, te: (te[i], 0, 0)),
        ],
        out_specs=pl.BlockSpec((tm * SUBLANES, LANES), lambda i, te: (i, 0)),
        scratch_shapes=[
            pltpu.VMEM((D_MODEL, 2 * D_FF), BF16),
            pltpu.VMEM((D_FF, D_MODEL), BF16),
            pltpu.SMEM((1,), jnp.int32),
        ],
    )
    return pl.pallas_call(
        _ffn_kernel,
        grid_spec=grid_spec,
        out_shape=jax.ShapeDtypeStruct((n_tiles * tm * SUBLANES, LANES), F32),
        compiler_params=_params(1),
        name="moe_ffn",
    )(texp, xs_tiles, w_gu, b_gu.reshape(N_EXPERTS, 1, 2 * D_FF), w_down,
      b_down.reshape(N_EXPERTS, 1, D_MODEL))


def _combine_kernel(pos_ref, route_ref, x1_ref, ys_ref, out_ref, buf_ref, sem):
    tm = x1_ref.shape[0]

    def copy(j, k):
        p = pos_ref[j * TOP_K + k]
        return pltpu.make_async_copy(
            ys_ref.at[pl.ds(pl.multiple_of(p * SUBLANES, SUBLANES), SUBLANES), :],
            buf_ref.at[k, pl.ds(pl.multiple_of(j * SUBLANES, SUBLANES), SUBLANES), :], sem)

    def issue(j, c):
        for k in range(TOP_K):
            copy(j, k).start()
        return c

    lax.fori_loop(0, tm, issue, 0)

    def drain(j, c):
        for k in range(TOP_K):
            copy(j, k).wait()
        return c

    lax.fori_loop(0, tm, drain, 0)

    route = route_ref[...]
    acc = x1_ref[...]
    for k in range(TOP_K):
        gate = route[:, ROUTE_GATES + k:ROUTE_GATES + k + 1]
        rows = jnp.concatenate(
            [buf_ref[k, pl.ds(s, tm, stride=SUBLANES), :] for s in range(D_MODEL // LANES)], axis=-1)
        acc = acc + gate * rows
    out_ref[...] = acc


def _combine(pos, route, x1, ys_tiles):
    t = x1.shape[0]
    tm = ROW_TILE
    return pl.pallas_call(
        _combine_kernel,
        grid=(t // tm,),
        in_specs=[
            pl.BlockSpec((tm * TOP_K,), lambda i: (i,), memory_space=pltpu.SMEM),
            pl.BlockSpec((tm, LANES), lambda i: (i, 0)),
            pl.BlockSpec((tm, D_MODEL), lambda i: (i, 0)),
            pl.BlockSpec(memory_space=pl.ANY),
        ],
        out_specs=pl.BlockSpec((tm, D_MODEL), lambda i: (i, 0)),
        out_shape=jax.ShapeDtypeStruct((t, D_MODEL), F32),
        scratch_shapes=[pltpu.VMEM((TOP_K, tm * SUBLANES, LANES), F32), pltpu.SemaphoreType.DMA],
        compiler_params=_params(1),
        name="moe_combine",
    )(pos, route, x1, ys_tiles)


def _layer(x, g_mix, w_in, q_gain, k_gain, rel_bias, ssm_a_re, ssm_a_im, ssm_log_dt, ssm_b_re,
           ssm_b_im, ssm_c_re, ssm_c_im, ssm_d, w_glu, b_glu, w_attn_proj, w_ssm_proj, w_out,
           g_ffn, w_router, b_router, w_gate_up, b_gate_up, w_down, b_down):
    b, s, _ = x.shape
    x2d = x.reshape(b * s, D_MODEL)

    *qkvs, rest = _inproj(x, g_mix, w_in[:, _inproj_columns()].astype(BF16))

    head_of = np.arange(GROUP_WIDTH) // HEAD_DIM
    seg = jnp.asarray(head_of[:, None] == head_of[None, :], BF16)
    os_, lses = [], []
    for gi, (window, dil) in enumerate(ATTN_GROUPS):
        table = rel_bias[:, gi * HEADS_PER_GROUP:(gi + 1) * HEADS_PER_GROUP]
        o, lse = _attention_group(qkvs[gi], gi, _attn_bias(table, window, dil), q_gain[gi], k_gain[gi], seg)
        os_.append(o)
        lses.append(lse)

    tab, bmat, cmat = _ssm_tables(ssm_a_re, ssm_a_im, ssm_log_dt, ssm_b_re, ssm_b_im, ssm_c_re, ssm_c_im)
    ys = _ssm(rest, tab, bmat, cmat, ssm_d)

    expand = jnp.asarray(np.arange(LANES)[:, None] == head_of[None, :], BF16)
    pad_e = LANES - N_EXPERTS
    weights = dict(
        expand=expand, wa=w_attn_proj.astype(BF16), wglu=w_glu.astype(BF16),
        bglu=b_glu.reshape(1, SSM_WIDTH), ws=w_ssm_proj.astype(BF16), wo=w_out.astype(BF16),
        gffn=g_ffn.reshape(1, D_MODEL), wr=jnp.pad(w_router, ((0, 0), (0, pad_e))).astype(BF16),
        br=jnp.pad(b_router, (0, pad_e)).reshape(1, LANES))
    x1, h2_tiles, route, counts = _mix(os_, lses, ys, rest, x2d, weights)

    ids = route[:, ROUTE_IDS:ROUTE_IDS + TOP_K].astype(jnp.int32).reshape(-1)
    ranks = route[:, ROUTE_RANKS:ROUTE_RANKS + TOP_K].astype(jnp.int32).reshape(-1)
    counts_i = counts[0, :N_EXPERTS].astype(jnp.int32)
    xs_tiles, pos, texp = _dispatch(counts_i, ids, ranks, h2_tiles)
    ys_tiles = _expert_ffn(texp, xs_tiles, w_gate_up, b_gate_up, w_down, b_down)
    out = _combine(pos, route, x1, ys_tiles)
    return out.reshape(b, s, D_MODEL)


_layer_jit = jax.jit(_layer)


def kernel(x, g_mix, w_in, q_gain, k_gain, rel_bias, ssm_a_re, ssm_a_im, ssm_log_dt, ssm_b_re, ssm_b_im, ssm_c_re, ssm_c_im, ssm_d, w_glu, b_glu, w_attn_proj, w_ssm_proj, w_out, g_ffn, w_router, b_router, w_gate_up, b_gate_up, w_down, b_down):
    return _layer_jit(x, g_mix[0], w_in[0], q_gain[0], k_gain[0], rel_bias, ssm_a_re[0], ssm_a_im[0],
                      ssm_log_dt[0], ssm_b_re[0], ssm_b_im[0], ssm_c_re[0], ssm_c_im[0], ssm_d[0],
                      w_glu[0], b_glu[0], w_attn_proj[0], w_ssm_proj[0], w_out[0], g_ffn[0],
                      w_router[0], b_router[0], w_gate_up[0], b_gate_up[0], w_down[0], b_down[0])
```

```python
import functools
import math

import jax
import jax.numpy as jnp
import numpy as np
from jax import lax
from jax.experimental import pallas as pl
from jax.experimental.pallas import tpu as pltpu

F32 = jnp.float32
BF16 = jnp.bfloat16

D_MODEL = 1024
HEAD_DIM = 64
ATTN_GROUPS = ((128, 1), (512, 4), (2048, 16))
HEADS_PER_GROUP = 8
GROUP_WIDTH = HEADS_PER_GROUP * HEAD_DIM
N_ATTN_HEADS = len(ATTN_GROUPS) * HEADS_PER_GROUP
QKV_WIDTH = 3 * N_ATTN_HEADS * HEAD_DIM
ATTN_BLOCK = 128
REL_BUCKETS = 32
REL_MAX_DIST = 2048
SSM_WIDTH = 512
SSM_GROUP = 16
SSM_GROUPS = 32
SSM_STATE = 64
IN_WIDTH = QKV_WIDTH + SSM_WIDTH + 2 * D_MODEL
N_EXPERTS = 32
TOP_K = 4
D_FF = 1024
SWIGLU_LIMIT = 7.0
SWIGLU_ALPHA = 1.702
EPS = 1e-6

SUBLANES = 8
LANES = 128
MASK_VALUE = -1e30
VMEM_LIMIT = 56 * 1024 * 1024

QKV_GROUP_WIDTH = 3 * GROUP_WIDTH
REST_WIDTH = 2 * D_MODEL + SSM_WIDTH
GATE_A_BLOCK = 0
GATE_B_BLOCK = 1
U_BLOCK = 2 * D_MODEL // SSM_WIDTH

ROW_TILE = 256
SSM_TIME_TILE = 512
MOE_TILE = 256


def _params(n_axes, vmem=VMEM_LIMIT):
    return pltpu.CompilerParams(dimension_semantics=("arbitrary",) * n_axes, vmem_limit_bytes=vmem)


def _inproj_columns():
    n_g = len(ATTN_GROUPS)
    cols = []
    for gi in range(n_g):
        for part in range(3):
            start = (part * n_g + gi) * GROUP_WIDTH
            cols.append(np.arange(start, start + GROUP_WIDTH))
    cols.append(np.arange(QKV_WIDTH + SSM_WIDTH, IN_WIDTH))
    cols.append(np.arange(QKV_WIDTH, QKV_WIDTH + SSM_WIDTH))
    return np.concatenate(cols)


def _inproj_kernel(x_ref, g_ref, w_ref, q0_ref, q1_ref, q2_ref, rest_ref, stage_ref):
    x = x_ref[0]
    tm = x.shape[0]
    ms = jnp.mean(x * x, axis=-1, keepdims=True)
    h = x * lax.rsqrt(ms + EPS) * g_ref[...]
    res = jnp.dot(h.astype(BF16), w_ref[...], preferred_element_type=F32)
    w = QKV_GROUP_WIDTH
    q0_ref[0, 0] = res[:, 0:w]
    rest_ref[0] = res[:, 3 * w:]
    n_blk = w // LANES
    for c in range(2 * n_blk):
        stage_ref[c] = res[:, w + c * LANES:w + (c + 1) * LANES]
    for out_ref, gi in ((q1_ref, 1), (q2_ref, 2)):
        dil = ATTN_GROUPS[gi][1]
        for r in range(dil):
            for c in range(n_blk):
                out_ref[0, r, :, c * LANES:(c + 1) * LANES] = stage_ref[
                    (gi - 1) * n_blk + c, pl.ds(r, tm // dil, stride=dil), :]


def _inproj(x, g, w_bf16):
    b, s, _ = x.shape
    tm = ROW_TILE
    dils = [d for _, d in ATTN_GROUPS]
    out_specs = [pl.BlockSpec((1, d, tm // d, QKV_GROUP_WIDTH), lambda bi, i: (bi, 0, i, 0)) for d in dils]
    out_shape = [jax.ShapeDtypeStruct((b, d, s // d, QKV_GROUP_WIDTH), F32) for d in dils]
    return pl.pallas_call(
        _inproj_kernel,
        grid=(b, s // tm),
        in_specs=[
            pl.BlockSpec((1, tm, D_MODEL), lambda bi, i: (bi, i, 0)),
            pl.BlockSpec((1, D_MODEL), lambda bi, i: (0, 0)),
            pl.BlockSpec((D_MODEL, IN_WIDTH), lambda bi, i: (0, 0), pipeline_mode=pl.Buffered(1)),
        ],
        out_specs=out_specs + [pl.BlockSpec((1, tm, REST_WIDTH), lambda bi, i: (bi, i, 0))],
        out_shape=out_shape + [jax.ShapeDtypeStruct((b, s, REST_WIDTH), F32)],
        scratch_shapes=[pltpu.VMEM((2 * QKV_GROUP_WIDTH // LANES, tm, LANES), F32)],
        compiler_params=_params(2),
        name="inproj",
    )(x, g.reshape(1, D_MODEL), w_bf16)


def _rel_bucket_np(dist):
    max_exact = REL_BUCKETS // 2
    d = np.maximum(dist, 1).astype(np.float32)
    large = max_exact + (np.log(d / max_exact) / math.log(REL_MAX_DIST / max_exact)
                         * (REL_BUCKETS - max_exact)).astype(np.int32)
    large = np.minimum(large, REL_BUCKETS - 1)
    return np.where(dist < max_exact, dist, large).astype(np.int32)


def _attn_bias(table, window, dilation):
    blk = ATTN_BLOCK
    span = window // dilation
    qi = np.arange(blk)[:, None]
    ki = np.arange(2 * blk)[None, :]
    steps = qi + blk - ki
    band = (steps >= 0) & (steps <= span)
    bucket = _rel_bucket_np(np.clip(steps, 0, None) * dilation)
    onehot = np.eye(REL_BUCKETS, dtype=np.float32)[bucket]
    bias = jnp.einsum('qkb,bh->hqk', onehot, table.astype(F32), precision=lax.Precision.HIGHEST)
    b0 = jnp.where(band[None], bias, MASK_VALUE)
    b1 = jnp.where((band & (ki >= blk))[None], bias, MASK_VALUE)
    return jnp.stack([b0, b1], axis=0)


def _attn_kernel(q_ref, k_ref, v_ref, bias_ref, qg_ref, kg_ref, seg_ref, o_ref, lse_ref,
                 kcat_ref, vcat_ref):
    blk = ATTN_BLOCK
    i = pl.program_id(2)
    seg = seg_ref[...]

    def head_norm(t, gain):
        ss = _split_dot(t * t, seg)
        return t * lax.rsqrt(ss * (1.0 / HEAD_DIM) + EPS) * gain

    qn = (head_norm(q_ref[0, 0], qg_ref[...]) * (HEAD_DIM ** -0.5)).astype(BF16)

    @pl.when(i == 0)
    def _():
        kcat_ref[0:blk, :] = jnp.zeros((blk, GROUP_WIDTH), BF16)
        vcat_ref[0:blk, :] = jnp.zeros((blk, GROUP_WIDTH), BF16)

    @pl.when(i > 0)
    def _():
        kcat_ref[0:blk, :] = kcat_ref[blk:2 * blk, :]
        vcat_ref[0:blk, :] = vcat_ref[blk:2 * blk, :]

    kcat_ref[blk:2 * blk, :] = head_norm(k_ref[0, 0], kg_ref[...]).astype(BF16)
    vcat_ref[blk:2 * blk, :] = v_ref[0, 0].astype(BF16)
    first = (i == 0).astype(jnp.int32)

    lane = lax.broadcasted_iota(jnp.int32, (blk, LANES), 1)
    low_half = lane < HEAD_DIM
    lse_tile = jnp.zeros((blk, LANES), F32)
    outs = []
    for p in range(GROUP_WIDTH // LANES):
        qp = qn[:, p * LANES:(p + 1) * LANES]
        kp = kcat_ref[:, p * LANES:(p + 1) * LANES]
        vp = vcat_ref[:, p * LANES:(p + 1) * LANES]
        pair = []
        for hh in range(2):
            head = 2 * p + hh
            sel = low_half if hh == 0 else jnp.logical_not(low_half)
            qm = jnp.where(sel, qp, jnp.zeros_like(qp))
            s = lax.dot_general(qm, kp, (((1,), (1,)), ((), ())), preferred_element_type=F32)
            s = s + bias_ref[first, head]
            mx = jnp.max(s, axis=-1, keepdims=True)
            pe = jnp.exp(s - mx)
            den = jnp.sum(pe, axis=-1, keepdims=True)
            o = jnp.dot(pe.astype(BF16), vp, preferred_element_type=F32)
            pair.append(o * (1.0 / den))
            lse_tile = jnp.where(lane == head, mx + jnp.log(den), lse_tile)
        outs.append(jnp.where(low_half, pair[0], pair[1]))
    o_ref[0, 0] = jnp.concatenate(outs, axis=-1)
    lse_ref[0, 0] = lse_tile


def _attention_group(qkv, gi, bias, q_gain, k_gain, seg):
    b, dil, length, _ = qkv.shape
    blk = ATTN_BLOCK
    nblk = length // blk

    def col(c):
        return pl.BlockSpec((1, 1, blk, GROUP_WIDTH), lambda bi, r, i: (bi, r, i, c))

    const = lambda shape: pl.BlockSpec(shape, lambda bi, r, i: (0,) * len(shape))
    return pl.pallas_call(
        _attn_kernel,
        grid=(b, dil, nblk),
        in_specs=[
            col(0), col(1), col(2),
            const((2, HEADS_PER_GROUP, blk, 2 * blk)),
            const((1, GROUP_WIDTH)),
            const((1, GROUP_WIDTH)),
            const((GROUP_WIDTH, GROUP_WIDTH)),
        ],
        out_specs=[
            pl.BlockSpec((1, 1, blk, GROUP_WIDTH), lambda bi, r, i: (bi, r, i, 0)),
            pl.BlockSpec((1, 1, blk, LANES), lambda bi, r, i: (bi, r, i, 0)),
        ],
        out_shape=[
            jax.ShapeDtypeStruct((b, dil, length, GROUP_WIDTH), F32),
            jax.ShapeDtypeStruct((b, dil, length, LANES), F32),
        ],
        scratch_shapes=[
            pltpu.VMEM((2 * blk, GROUP_WIDTH), BF16),
            pltpu.VMEM((2 * blk, GROUP_WIDTH), BF16),
        ],
        compiler_params=_params(3),
        name=f"attn_g{gi}",
    )(qkv, qkv, qkv, bias,
      jnp.tile(q_gain, HEADS_PER_GROUP).reshape(1, GROUP_WIDTH),
      jnp.tile(k_gain, HEADS_PER_GROUP).reshape(1, GROUP_WIDTH), seg)


HALF_STATE = SSM_GROUPS * SSM_STATE // 2
SCAN_COLS = 4


def _ssm_tables(a_re, a_im, log_dt, b_re, b_im, c_re, c_im):
    lam = lax.complex(a_re.astype(F32), a_im.astype(F32))
    dt = jnp.exp(log_dt.astype(F32))[:, None]
    a_bar = jnp.exp(lam * dt)
    b_bar = ((a_bar - 1.0) / lam)[..., None] * lax.complex(b_re.astype(F32), b_im.astype(F32))

    rows = jnp.arange(SUBLANES)
    tabs = []
    for s in (1, 2, 4):
        p = jnp.exp(lam * dt * s).reshape(1, -1)
        tabs.append(jnp.where((rows >= s)[:, None], p, 0.0))
    tabs.append(jnp.exp((lam * dt).reshape(1, -1) * (rows + 1.0)[:, None]))
    tab = jnp.stack(tabs, axis=0)
    tab = jnp.stack([tab.real, tab.imag], axis=1).astype(F32)

    gh = SSM_GROUPS // 2
    eye = jnp.eye(gh, dtype=F32)

    def in_mat(m):
        return jnp.einsum('gnc,gh->gchn', m, eye).reshape(gh * SSM_GROUP, gh * SSM_STATE)

    def out_mat(m):
        return jnp.einsum('gcn,gh->gnhc', m, eye).reshape(gh * SSM_STATE, gh * SSM_GROUP)

    b_mats, c_mats = [], []
    for h in range(2):
        sl = slice(h * gh, (h + 1) * gh)
        b_mats.append(jnp.concatenate([in_mat(b_bar.real[sl]), in_mat(b_bar.imag[sl])], axis=1))
        c_mats.append(jnp.concatenate([out_mat(c_re.astype(F32)[sl]),
                                       -out_mat(c_im.astype(F32)[sl])], axis=0))
    return tab, jnp.stack(b_mats).astype(BF16), jnp.stack(c_mats).astype(BF16)


def _ssm_kernel(u_ref, tab_ref, bmat_ref, cmat_ref, d_ref, y_ref, xs_ref, carry_ref):
    tt = u_ref.shape[1]
    half_w = 2 * HALF_STATE

    @pl.when(pl.program_id(1) == 0)
    def _():
        carry_ref[...] = jnp.zeros_like(carry_ref)

    u = u_ref[0]
    ub = u.astype(BF16)
    hw = SSM_WIDTH // 2
    for h in range(2):
        xs_ref[:, h * half_w:(h + 1) * half_w] = jnp.dot(
            ub[:, h * hw:(h + 1) * hw], bmat_ref[h], preferred_element_type=F32)

    n_cols = 2 * HALF_STATE // LANES
    for c0 in range(0, n_cols, SCAN_COLS):
        cols = []
        for c in range(c0, c0 + SCAN_COLS):
            h, j = divmod(c, HALF_STATE // LANES)
            re_col = h * half_w + j * LANES
            cols.append((c * LANES, re_col, re_col + HALF_STATE))

        def body(rb, carry, cols=cols):
            r0 = pl.multiple_of(rb * SUBLANES, SUBLANES)
            new = []
            for (tc, rc, ic), (cr, ci) in zip(cols, carry):
                re = xs_ref[pl.ds(r0, SUBLANES), rc:rc + LANES]
                im = xs_ref[pl.ds(r0, SUBLANES), ic:ic + LANES]
                for lvl, s in enumerate((1, 2, 4)):
                    ar = tab_ref[lvl, 0, :, tc:tc + LANES]
                    ai = tab_ref[lvl, 1, :, tc:tc + LANES]
                    sr = pltpu.roll(re, s, 0)
                    si = pltpu.roll(im, s, 0)
                    re, im = re + ar * sr - ai * si, im + ar * si + ai * sr
                pr = tab_ref[3, 0, :, tc:tc + LANES]
                pi = tab_ref[3, 1, :, tc:tc + LANES]
                re, im = re + pr * cr - pi * ci, im + pr * ci + pi * cr
                xs_ref[pl.ds(r0, SUBLANES), rc:rc + LANES] = re
                xs_ref[pl.ds(r0, SUBLANES), ic:ic + LANES] = im
                new.append((jnp.broadcast_to(re[SUBLANES - 1:, :], (SUBLANES, LANES)),
                            jnp.broadcast_to(im[SUBLANES - 1:, :], (SUBLANES, LANES))))
            return tuple(new)

        init = tuple((carry_ref[:, rc:rc + LANES], carry_ref[:, ic:ic + LANES]) for _, rc, ic in cols)
        fin = lax.fori_loop(0, tt // SUBLANES, body, init)
        for (_, rc, ic), (cr, ci) in zip(cols, fin):
            carry_ref[:, rc:rc + LANES] = cr
            carry_ref[:, ic:ic + LANES] = ci

    ys = []
    for h in range(2):
        xh = xs_ref[:, h * half_w:(h + 1) * half_w].astype(BF16)
        ys.append(jnp.dot(xh, cmat_ref[h], preferred_element_type=F32))
    y_ref[0] = jnp.concatenate(ys, axis=-1) + d_ref[...] * u


def _ssm(rest, tab, bmat, cmat, d_skip):
    b, s, _ = rest.shape
    tt = min(SSM_TIME_TILE, s)
    n_state = 4 * HALF_STATE
    return pl.pallas_call(
        _ssm_kernel,
        grid=(b, s // tt),
        in_specs=[
            pl.BlockSpec((1, tt, SSM_WIDTH), lambda bi, j: (bi, j, U_BLOCK)),
            pl.BlockSpec(tab.shape, lambda bi, j: (0, 0, 0, 0)),
            pl.BlockSpec(bmat.shape, lambda bi, j: (0, 0, 0)),
            pl.BlockSpec(cmat.shape, lambda bi, j: (0, 0, 0)),
            pl.BlockSpec((1, SSM_WIDTH), lambda bi, j: (0, 0)),
        ],
        out_specs=pl.BlockSpec((1, tt, SSM_WIDTH), lambda bi, j: (bi, j, 0)),
        out_shape=jax.ShapeDtypeStruct((b, s, SSM_WIDTH), F32),
        scratch_shapes=[
            pltpu.VMEM((tt, n_state), F32),
            pltpu.VMEM((SUBLANES, n_state), F32),
        ],
        compiler_params=_params(2),
        name="ssm",
    )(rest, tab, bmat, cmat, d_skip.reshape(1, SSM_WIDTH)).reshape(b * s, SSM_WIDTH)


ROUTE_IDS, ROUTE_GATES, ROUTE_SLOTS = 0, TOP_K, 2 * TOP_K
RUN_START, RUN_COUNT, RUN_OFFSET = 0, 1, 2
SLOTS_PER_TILE = ROW_TILE * TOP_K


def _split_dot(x, w_bf16):
    hi = x.astype(BF16)
    lo = (x - hi.astype(F32)).astype(BF16)
    return (jnp.dot(hi, w_bf16, preferred_element_type=F32)
            + jnp.dot(lo, w_bf16, preferred_element_type=F32))


def _mix_kernel(o0_ref, o1_ref, o2_ref, l0_ref, l1_ref, l2_ref, ys_ref, ga_ref, gb_ref, x_ref,
                expand_ref, wa_ref, wglu_ref, bglu_ref, ws_ref, wo_ref, gffn_ref, wr_ref, br_ref,
                x1_ref, h2_ref, route_ref, counts_ref, runs_ref, run_ref, ostage_ref, lstage_ref):
    tm = x_ref.shape[0]

    @pl.when(jnp.logical_and(pl.program_id(0) == 0, pl.program_id(1) == 0))
    def _():
        run_ref[...] = jnp.zeros_like(run_ref)

    for slot, (o_ref, l_ref) in enumerate(((o1_ref, l1_ref), (o2_ref, l2_ref))):
        dil = ATTN_GROUPS[slot + 1][1]
        for r in range(dil):
            for c in range(GROUP_WIDTH // LANES):
                ostage_ref[slot, c, pl.ds(r, tm // dil, stride=dil), :] = o_ref[
                    0, r, :, c * LANES:(c + 1) * LANES]
            lstage_ref[slot, pl.ds(r, tm // dil, stride=dil), :] = l_ref[0, r]
    group_out = [o0_ref[0, 0]] + [
        jnp.concatenate([ostage_ref[slot, c] for c in range(GROUP_WIDTH // LANES)], axis=-1)
        for slot in range(2)]

    lses = [l0_ref[0, 0], lstage_ref[0], lstage_ref[1]]
    mx = jnp.maximum(jnp.maximum(lses[0], lses[1]), lses[2])
    es = [jnp.exp(l - mx) for l in lses]
    inv = 1.0 / (es[0] + es[1] + es[2])
    expand = expand_ref[...]
    attn = jnp.zeros((tm, GROUP_WIDTH), F32)
    for e, o_g in zip(es, group_out):
        attn = attn + _split_dot(e * inv, expand) * o_g
    y_a = jnp.dot(attn.astype(BF16), wa_ref[...], preferred_element_type=F32)

    ys = ys_ref[...]
    ys = 0.5 * ys * (1.0 + jnp.tanh(math.sqrt(2.0 / math.pi) * (ys + 0.044715 * (ys * ys * ys))))
    glu = jnp.dot(ys.astype(BF16), wglu_ref[...], preferred_element_type=F32) + bglu_ref[...]
    ys = ys * jax.nn.sigmoid(glu)
    y_b = jnp.dot(ys.astype(BF16), ws_ref[...], preferred_element_type=F32)

    mixed = jax.nn.sigmoid(ga_ref[...]) * y_a + jax.nn.sigmoid(gb_ref[...]) * y_b
    x1 = x_ref[...] + jnp.dot(mixed.astype(BF16), wo_ref[...], preferred_element_type=F32)
    x1_ref[...] = x1

    ms = jnp.mean(x1 * x1, axis=-1, keepdims=True)
    h2 = x1 * lax.rsqrt(ms + EPS) * gffn_ref[...]
    h2b = h2.astype(BF16)
    h2_ref[...] = h2b

    logits = jnp.dot(h2b, wr_ref[...], preferred_element_type=F32) + br_ref[...]
    lane = lax.broadcasted_iota(jnp.int32, (tm, LANES), 1)
    work = jnp.where(lane < N_EXPERTS, logits, -jnp.inf)
    sel_mask = jnp.zeros((tm, LANES), F32)
    route = jnp.zeros((tm, LANES), F32)
    vals, hots = [], []
    for k in range(TOP_K):
        v = jnp.max(work, axis=-1, keepdims=True)
        idx = jnp.min(jnp.where(work == v, lane, LANES), axis=-1, keepdims=True)
        hot = lane == idx
        work = jnp.where(hot, -jnp.inf, work)
        sel_mask = jnp.where(hot, 1.0, sel_mask)
        route = jnp.where(lane == ROUTE_IDS + k, idx.astype(F32), route)
        vals.append(v)
        hots.append(hot)
    exps = [jnp.exp(v - vals[0]) for v in vals]
    inv_den = 1.0 / (exps[0] + exps[1] + exps[2] + exps[3])

    r_i = lax.broadcasted_iota(jnp.int32, (tm, tm), 0)
    c_i = lax.broadcasted_iota(jnp.int32, (tm, tm), 1)
    tri = jnp.where(c_i < r_i, 1.0, 0.0).astype(BF16)
    local = jnp.dot(tri, sel_mask.astype(BF16), preferred_element_type=F32)
    count = jnp.sum(sel_mask, axis=0, keepdims=True)
    e_r = lax.broadcasted_iota(jnp.int32, (LANES, LANES), 0)
    e_c = lax.broadcasted_iota(jnp.int32, (LANES, LANES), 1)
    upper = jnp.where(e_r < e_c, 1.0, 0.0).astype(BF16)
    offset = jnp.dot(jnp.broadcast_to(count, (SUBLANES, LANES)).astype(BF16), upper,
                     preferred_element_type=F32)
    slot_of = local + offset[0:1, :]
    for k in range(TOP_K):
        slot = jnp.sum(jnp.where(hots[k], slot_of, 0.0), axis=-1, keepdims=True)
        route = jnp.where(lane == ROUTE_GATES + k, exps[k] * inv_den, route)
        route = jnp.where(lane == ROUTE_SLOTS + k, slot, route)
    route_ref[...] = route
    row8 = lax.broadcasted_iota(jnp.int32, (SUBLANES, LANES), 0)
    runs = jnp.where(row8 == RUN_START, run_ref[...],
                     jnp.where(row8 == RUN_COUNT, jnp.broadcast_to(count, (SUBLANES, LANES)),
                               jnp.where(row8 == RUN_OFFSET, offset, 0.0)))
    runs_ref[0] = runs
    total = run_ref[0:1, :] + count
    run_ref[...] = jnp.broadcast_to(total, run_ref.shape)
    counts_ref[...] = jnp.broadcast_to(total, counts_ref.shape)


def _mix(os_, lses, ys, rest, x2d, w):
    b, s, _ = rest.shape
    t = b * s
    tm = ROW_TILE
    n_i = s // tm
    rest2d = rest.reshape(t, REST_WIDTH)
    row = lambda width, blk=0: pl.BlockSpec((tm, width), lambda bi, i: (bi * n_i + i, blk))
    const = lambda a: pl.BlockSpec(a.shape, lambda bi, i: (0,) * a.ndim)
    grouped = lambda width: [pl.BlockSpec((1, d, tm // d, width), lambda bi, i: (bi, 0, i, 0))
                             for _, d in ATTN_GROUPS]
    consts = [w["expand"], w["wa"], w["wglu"], w["bglu"], w["ws"], w["wo"], w["gffn"], w["wr"], w["br"]]
    n_dilated = len(ATTN_GROUPS) - 1
    return pl.pallas_call(
        _mix_kernel,
        grid=(b, n_i),
        in_specs=grouped(GROUP_WIDTH) + grouped(LANES) + [
            row(SSM_WIDTH),
            row(D_MODEL, GATE_A_BLOCK),
            row(D_MODEL, GATE_B_BLOCK),
            row(D_MODEL),
        ] + [const(a) for a in consts],
        out_specs=[
            row(D_MODEL),
            row(D_MODEL),
            row(LANES),
            pl.BlockSpec((SUBLANES, LANES), lambda bi, i: (0, 0)),
            pl.BlockSpec((1, SUBLANES, LANES), lambda bi, i: (bi * n_i + i, 0, 0)),
        ],
        out_shape=[
            jax.ShapeDtypeStruct((t, D_MODEL), F32),
            jax.ShapeDtypeStruct((t, D_MODEL), BF16),
            jax.ShapeDtypeStruct((t, LANES), F32),
            jax.ShapeDtypeStruct((SUBLANES, LANES), F32),
            jax.ShapeDtypeStruct((t // tm, SUBLANES, LANES), F32),
        ],
        scratch_shapes=[
            pltpu.VMEM((SUBLANES, LANES), F32),
            pltpu.VMEM((n_dilated, GROUP_WIDTH // LANES, tm, LANES), F32),
            pltpu.VMEM((n_dilated, tm, LANES), F32),
        ],
        compiler_params=_params(2),
        name="mix_router",
    )(*os_, *lses, ys, rest2d, rest2d, x2d, *consts)


def _n_moe_tiles(t):
    return t * TOP_K // MOE_TILE + N_EXPERTS


def _copy_run(src_ref, dst_ref, src_row, dst_row, n_rows, sem):
    size = SLOTS_PER_TILE
    while size >= 1:
        @pl.when((n_rows & size) != 0)
        def _(size=size, src_row=src_row, dst_row=dst_row):
            pltpu.make_async_copy(
                src_ref.at[pl.ds(pl.multiple_of(src_row * SUBLANES, SUBLANES), size * SUBLANES), :],
                dst_ref.at[pl.ds(pl.multiple_of(dst_row * SUBLANES, SUBLANES), size * SUBLANES), :],
                sem).start()
        step = n_rows & size
        src_row = src_row + step
        dst_row = dst_row + step
        size //= 2


def _wait_tile(hbm_ref, vmem_ref, sem):
    pltpu.make_async_copy(hbm_ref.at[pl.ds(0, SLOTS_PER_TILE * SUBLANES), :], vmem_ref, sem).wait()


def _dispatch_kernel(counts_ref, start_ref, cnt_ref, toff_ref, h2_ref, route_ref,
                     xs_ref, texp_ref, off_ref, stage_ref, sems):
    i = pl.program_id(0)
    n_steps = pl.num_programs(0)
    n_tiles = texp_ref.shape[0] - 1

    @pl.when(i == 0)
    def _():
        def per_expert(e, start):
            off_ref[e] = start
            n_t = (counts_ref[e] + (MOE_TILE - 1)) // MOE_TILE

            def mark(ti, c):
                texp_ref[start // MOE_TILE + ti] = e
                return c

            lax.fori_loop(0, n_t, mark, 0)
            return start + n_t * MOE_TILE

        end = lax.fori_loop(0, N_EXPERTS, per_expert, 0)
        n_active = end // MOE_TILE
        texp_ref[n_tiles] = n_active

        def fill(ti, c):
            texp_ref[ti] = texp_ref[n_active - 1]
            return c

        lax.fori_loop(n_active, n_tiles, fill, 0)

    slots_t = route_ref[...].T
    slot_id = lax.broadcasted_iota(jnp.int32, (SLOTS_PER_TILE, ROW_TILE), 0).astype(F32)
    perm = jnp.zeros((SLOTS_PER_TILE, ROW_TILE), F32)
    for k in range(TOP_K):
        perm = jnp.where(slot_id == slots_t[ROUTE_SLOTS + k:ROUTE_SLOTS + k + 1, :], 1.0, perm)
    srt = jnp.dot(perm.astype(BF16), h2_ref[...], preferred_element_type=F32)
    buf = i % 2
    for s in range(D_MODEL // LANES):
        stage_ref[buf, pl.ds(s, SLOTS_PER_TILE, stride=SUBLANES), :] = srt[:, s * LANES:(s + 1) * LANES]

    @pl.when(i > 0)
    def _():
        _wait_tile(xs_ref, stage_ref.at[1 - buf], sems.at[1 - buf])

    def per_run(e, c):
        r = i * N_EXPERTS + e
        _copy_run(stage_ref.at[buf], xs_ref, toff_ref[r], off_ref[e] + start_ref[r], cnt_ref[r],
                  sems.at[buf])
        return c

    lax.fori_loop(0, N_EXPERTS, per_run, 0)

    @pl.when(i == n_steps - 1)
    def _():
        _wait_tile(xs_ref, stage_ref.at[buf], sems.at[buf])


def _dispatch(counts, runs, h2, route):
    t = h2.shape[0]
    tm = ROW_TILE
    n_tiles = _n_moe_tiles(t)
    grid_spec = pltpu.PrefetchScalarGridSpec(
        num_scalar_prefetch=4,
        grid=(t // tm,),
        in_specs=[
            pl.BlockSpec((tm, D_MODEL), lambda i, *_: (i, 0)),
            pl.BlockSpec((tm, LANES), lambda i, *_: (i, 0)),
        ],
        out_specs=[
            pl.BlockSpec(memory_space=pl.ANY),
            pl.BlockSpec((n_tiles + 1,), lambda i, *_: (0,), memory_space=pltpu.SMEM),
            pl.BlockSpec((N_EXPERTS,), lambda i, *_: (0,), memory_space=pltpu.SMEM),
        ],
        scratch_shapes=[
            pltpu.VMEM((2, SLOTS_PER_TILE * SUBLANES, LANES), F32),
            pltpu.SemaphoreType.DMA((2,)),
        ],
    )
    return pl.pallas_call(
        _dispatch_kernel,
        grid_spec=grid_spec,
        out_shape=[
            jax.ShapeDtypeStruct((n_tiles * MOE_TILE * SUBLANES, LANES), F32),
            jax.ShapeDtypeStruct((n_tiles + 1,), jnp.int32),
            jax.ShapeDtypeStruct((N_EXPERTS,), jnp.int32),
        ],
        compiler_params=_params(1),
        name="moe_dispatch",
    )(counts, *runs, h2, route)


def _ffn_kernel(texp_ref, xs_ref, wgu_ref, bgu_ref, wd_ref, bd_ref, out_ref, wgu_bf, wd_bf, last_ref):
    i = pl.program_id(0)
    tm = MOE_TILE
    n_active = texp_ref[pl.num_programs(0)]
    e = texp_ref[i]

    @pl.when(i == 0)
    def _():
        last_ref[0] = -1

    @pl.when(jnp.logical_and(i < n_active, e != last_ref[0]))
    def _():
        wgu_bf[...] = wgu_ref[0].astype(BF16)
        wd_bf[...] = wd_ref[0].astype(BF16)
        last_ref[0] = e

    @pl.when(i < n_active)
    def _():
        x = jnp.concatenate(
            [xs_ref[pl.ds(s, tm, stride=SUBLANES), :] for s in range(D_MODEL // LANES)], axis=-1)
        gu = jnp.dot(x.astype(BF16), wgu_bf[...], preferred_element_type=F32) + bgu_ref[0]
        x_glu = jnp.minimum(gu[:, :D_FF], SWIGLU_LIMIT)
        x_lin = jnp.clip(gu[:, D_FF:], -SWIGLU_LIMIT, SWIGLU_LIMIT)
        act = x_glu * jax.nn.sigmoid(SWIGLU_ALPHA * x_glu) * (x_lin + 1.0)
        out = jnp.dot(act.astype(BF16), wd_bf[...], preferred_element_type=F32) + bd_ref[0]
        for s in range(D_MODEL // LANES):
            out_ref[pl.ds(s, tm, stride=SUBLANES), :] = out[:, s * LANES:(s + 1) * LANES]


def _expert_ffn(texp, xs_tiles, w_gu, b_gu, w_down, b_down):
    n_tiles = texp.shape[0] - 1
    tm = MOE_TILE
    grid_spec = pltpu.PrefetchScalarGridSpec(
        num_scalar_prefetch=1,
        grid=(n_tiles,),
        in_specs=[
            pl.BlockSpec((tm * SUBLANES, LANES), lambda i, te: (jnp.minimum(i, te[n_tiles] - 1), 0)),
            pl.BlockSpec((1, D_MODEL, 2 * D_FF), lambda i, te: (te[i], 0, 0)),
            pl.BlockSpec((1, 1, 2 * D_FF), lambda i, te: (te[i], 0, 0)),
            pl.BlockSpec((1, D_FF, D_MODEL), lambda i, te: (te[i], 0, 0)),
            pl.BlockSpec((1, 1, D_MODEL), lambda i, te: (te[i], 0, 0)),
        ],
        out_specs=pl.BlockSpec((tm * SUBLANES, LANES), lambda i, te: (i, 0)),
        scratch_shapes=[
            pltpu.VMEM((D_MODEL, 2 * D_FF), BF16),
            pltpu.VMEM((D_FF, D_MODEL), BF16),
            pltpu.SMEM((1,), jnp.int32),
        ],
    )
    return pl.pallas_call(
        _ffn_kernel,
        grid_spec=grid_spec,
        out_shape=jax.ShapeDtypeStruct((n_tiles * tm * SUBLANES, LANES), F32),
        compiler_params=_params(1),
        name="moe_ffn",
    )(texp, xs_tiles, w_gu, b_gu.reshape(N_EXPERTS, 1, 2 * D_FF), w_down,
      b_down.reshape(N_EXPERTS, 1, D_MODEL))


def _combine_kernel(off_ref, start_ref, cnt_ref, toff_ref, route_ref, x1_ref, ys_ref, out_ref,
                    buf_ref, sems):
    i = pl.program_id(0)
    n_steps = pl.num_programs(0)
    tm = x1_ref.shape[0]

    def fetch(tile, buf):
        def per_run(e, c):
            r = tile * N_EXPERTS + e
            _copy_run(ys_ref, buf_ref.at[buf], off_ref[e] + start_ref[r], toff_ref[r], cnt_ref[r],
                      sems.at[buf])
            return c

        lax.fori_loop(0, N_EXPERTS, per_run, 0)

    @pl.when(i == 0)
    def _():
        fetch(0, 0)

    @pl.when(i + 1 < n_steps)
    def _():
        fetch(i + 1, (i + 1) % 2)

    buf = i % 2
    _wait_tile(ys_ref, buf_ref.at[buf], sems.at[buf])

    rows = jnp.concatenate(
        [buf_ref[buf, pl.ds(s, SLOTS_PER_TILE, stride=SUBLANES), :] for s in range(D_MODEL // LANES)],
        axis=-1).astype(BF16)
    route = route_ref[...]
    slot_id = lax.broadcasted_iota(jnp.int32, (tm, SLOTS_PER_TILE), 1).astype(F32)
    weights = jnp.zeros((tm, SLOTS_PER_TILE), F32)
    for k in range(TOP_K):
        slot = route[:, ROUTE_SLOTS + k:ROUTE_SLOTS + k + 1]
        gate = route[:, ROUTE_GATES + k:ROUTE_GATES + k + 1]
        weights = jnp.where(slot_id == slot, gate, weights)
    out_ref[...] = x1_ref[...] + jnp.dot(weights.astype(BF16), rows, preferred_element_type=F32)


def _combine(off, runs, route, x1, ys_tiles):
    t = x1.shape[0]
    tm = ROW_TILE
    grid_spec = pltpu.PrefetchScalarGridSpec(
        num_scalar_prefetch=4,
        grid=(t // tm,),
        in_specs=[
            pl.BlockSpec((tm, LANES), lambda i, *_: (i, 0)),
            pl.BlockSpec((tm, D_MODEL), lambda i, *_: (i, 0)),
            pl.BlockSpec(memory_space=pl.ANY),
        ],
        out_specs=pl.BlockSpec((tm, D_MODEL), lambda i, *_: (i, 0)),
        scratch_shapes=[
            pltpu.VMEM((2, SLOTS_PER_TILE * SUBLANES, LANES), F32),
            pltpu.SemaphoreType.DMA((2,)),
        ],
    )
    return pl.pallas_call(
        _combine_kernel,
        grid_spec=grid_spec,
        out_shape=jax.ShapeDtypeStruct((t, D_MODEL), F32),
        compiler_params=_params(1),
        name="moe_combine",
    )(off, *runs, route, x1, ys_tiles)


def _layer(x, g_mix, w_in, q_gain, k_gain, rel_bias, ssm_a_re, ssm_a_im, ssm_log_dt, ssm_b_re,
           ssm_b_im, ssm_c_re, ssm_c_im, ssm_d, w_glu, b_glu, w_attn_proj, w_ssm_proj, w_out,
           g_ffn, w_router, b_router, w_gate_up, b_gate_up, w_down, b_down):
    b, s, _ = x.shape
    x2d = x.reshape(b * s, D_MODEL)

    *qkvs, rest = _inproj(x, g_mix, w_in[:, _inproj_columns()].astype(BF16))

    head_of = np.arange(GROUP_WIDTH) // HEAD_DIM
    seg = jnp.asarray(head_of[:, None] == head_of[None, :], BF16)
    os_, lses = [], []
    for gi, (window, dil) in enumerate(ATTN_GROUPS):
        table = rel_bias[:, gi * HEADS_PER_GROUP:(gi + 1) * HEADS_PER_GROUP]
        o, lse = _attention_group(qkvs[gi], gi, _attn_bias(table, window, dil), q_gain[gi], k_gain[gi], seg)
        os_.append(o)
        lses.append(lse)

    tab, bmat, cmat = _ssm_tables(ssm_a_re, ssm_a_im, ssm_log_dt, ssm_b_re, ssm_b_im, ssm_c_re, ssm_c_im)
    ys = _ssm(rest, tab, bmat, cmat, ssm_d)

    expand = jnp.asarray(np.arange(LANES)[:, None] == head_of[None, :], BF16)
    pad_e = LANES - N_EXPERTS
    weights = dict(
        expand=expand, wa=w_attn_proj.astype(BF16), wglu=w_glu.astype(BF16),
        bglu=b_glu.reshape(1, SSM_WIDTH), ws=w_ssm_proj.astype(BF16), wo=w_out.astype(BF16),
        gffn=g_ffn.reshape(1, D_MODEL), wr=jnp.pad(w_router, ((0, 0), (0, pad_e))).astype(BF16),
        br=jnp.pad(b_router, (0, pad_e)).reshape(1, LANES))
    x1, h2, route, counts, run_tab = _mix(os_, lses, ys, rest, x2d, weights)

    counts_i = counts[0, :N_EXPERTS].astype(jnp.int32)
    run_table = run_tab[:, :, :N_EXPERTS].astype(jnp.int32)
    runs = [run_table[:, row].reshape(-1) for row in (RUN_START, RUN_COUNT, RUN_OFFSET)]
    xs_tiles, texp, off = _dispatch(counts_i, runs, h2, route)
    ys_tiles = _expert_ffn(texp, xs_tiles, w_gate_up, b_gate_up, w_down, b_down)
    out = _combine(off, runs, route, x1, ys_tiles)
    return out.reshape(b, s, D_MODEL)


_layer_jit = jax.jit(_layer)


def kernel(x, g_mix, w_in, q_gain, k_gain, rel_bias, ssm_a_re, ssm_a_im, ssm_log_dt, ssm_b_re, ssm_b_im, ssm_c_re, ssm_c_im, ssm_d, w_glu, b_glu, w_attn_proj, w_ssm_proj, w_out, g_ffn, w_router, b_router, w_gate_up, b_gate_up, w_down, b_down):
    return _layer_jit(x, g_mix[0], w_in[0], q_gain[0], k_gain[0], rel_bias, ssm_a_re[0], ssm_a_im[0],
                      ssm_log_dt[0], ssm_b_re[0], ssm_b_im[0], ssm_c_re[0], ssm_c_im[0], ssm_d[0],
                      w_glu[0], b_glu[0], w_attn_proj[0], w_ssm_proj[0], w_out[0], g_ffn[0],
                      w_router[0], b_router[0], w_gate_up[0], b_gate_up[0], w_down[0], b_down[0])
```

```python
import functools
import math

import jax
import jax.numpy as jnp
import numpy as np
from jax import lax
from jax.experimental import pallas as pl
from jax.experimental.pallas import tpu as pltpu

F32 = jnp.float32
BF16 = jnp.bfloat16

D_MODEL = 1024
HEAD_DIM = 64
ATTN_GROUPS = ((128, 1), (512, 4), (2048, 16))
HEADS_PER_GROUP = 8
GROUP_WIDTH = HEADS_PER_GROUP * HEAD_DIM
N_ATTN_HEADS = len(ATTN_GROUPS) * HEADS_PER_GROUP
QKV_WIDTH = 3 * N_ATTN_HEADS * HEAD_DIM
ATTN_BLOCK = 128
REL_BUCKETS = 32
REL_MAX_DIST = 2048
SSM_WIDTH = 512
SSM_GROUP = 16
SSM_GROUPS = 32
SSM_STATE = 64
IN_WIDTH = QKV_WIDTH + SSM_WIDTH + 2 * D_MODEL
N_EXPERTS = 32
TOP_K = 4
D_FF = 1024
SWIGLU_LIMIT = 7.0
SWIGLU_ALPHA = 1.702
EPS = 1e-6

SUBLANES = 8
LANES = 128
MASK_VALUE = -1e30
VMEM_LIMIT = 56 * 1024 * 1024

QKV_GROUP_WIDTH = 3 * GROUP_WIDTH
REST_WIDTH = 2 * D_MODEL + SSM_WIDTH
GATE_A_BLOCK = 0
GATE_B_BLOCK = 1
U_BLOCK = 2 * D_MODEL // SSM_WIDTH

ROW_TILE = 256
SSM_TIME_TILE = 512
MOE_TILE = 512


def _params(n_axes, vmem=VMEM_LIMIT):
    return pltpu.CompilerParams(dimension_semantics=("arbitrary",) * n_axes, vmem_limit_bytes=vmem)


def _inproj_columns():
    n_g = len(ATTN_GROUPS)
    cols = []
    for gi in range(n_g):
        for part in range(3):
            start = (part * n_g + gi) * GROUP_WIDTH
            cols.append(np.arange(start, start + GROUP_WIDTH))
    cols.append(np.arange(QKV_WIDTH + SSM_WIDTH, IN_WIDTH))
    cols.append(np.arange(QKV_WIDTH, QKV_WIDTH + SSM_WIDTH))
    return np.concatenate(cols)


def _inproj_kernel(x_ref, g_ref, w_ref, q0_ref, q1_ref, q2_ref, rest_ref, stage_ref):
    x = x_ref[0]
    tm = x.shape[0]
    ms = jnp.mean(x * x, axis=-1, keepdims=True)
    h = x * lax.rsqrt(ms + EPS) * g_ref[...]
    res = jnp.dot(h.astype(BF16), w_ref[...], preferred_element_type=F32)
    w = QKV_GROUP_WIDTH
    q0_ref[0, 0] = res[:, 0:w]
    rest_ref[0] = res[:, 3 * w:]
    n_blk = w // LANES
    for c in range(2 * n_blk):
        stage_ref[c] = res[:, w + c * LANES:w + (c + 1) * LANES]
    for out_ref, gi in ((q1_ref, 1), (q2_ref, 2)):
        dil = ATTN_GROUPS[gi][1]
        for r in range(dil):
            for c in range(n_blk):
                out_ref[0, r, :, c * LANES:(c + 1) * LANES] = stage_ref[
                    (gi - 1) * n_blk + c, pl.ds(r, tm // dil, stride=dil), :]


def _inproj(x, g, w_bf16):
    b, s, _ = x.shape
    tm = ROW_TILE
    dils = [d for _, d in ATTN_GROUPS]
    out_specs = [pl.BlockSpec((1, d, tm // d, QKV_GROUP_WIDTH), lambda bi, i: (bi, 0, i, 0)) for d in dils]
    out_shape = [jax.ShapeDtypeStruct((b, d, s // d, QKV_GROUP_WIDTH), F32) for d in dils]
    return pl.pallas_call(
        _inproj_kernel,
        grid=(b, s // tm),
        in_specs=[
            pl.BlockSpec((1, tm, D_MODEL), lambda bi, i: (bi, i, 0)),
            pl.BlockSpec((1, D_MODEL), lambda bi, i: (0, 0)),
            pl.BlockSpec((D_MODEL, IN_WIDTH), lambda bi, i: (0, 0), pipeline_mode=pl.Buffered(1)),
        ],
        out_specs=out_specs + [pl.BlockSpec((1, tm, REST_WIDTH), lambda bi, i: (bi, i, 0))],
        out_shape=out_shape + [jax.ShapeDtypeStruct((b, s, REST_WIDTH), F32)],
        scratch_shapes=[pltpu.VMEM((2 * QKV_GROUP_WIDTH // LANES, tm, LANES), F32)],
        compiler_params=_params(2),
        name="inproj",
    )(x, g.reshape(1, D_MODEL), w_bf16)


def _rel_bucket_np(dist):
    max_exact = REL_BUCKETS // 2
    d = np.maximum(dist, 1).astype(np.float32)
    large = max_exact + (np.log(d / max_exact) / math.log(REL_MAX_DIST / max_exact)
                         * (REL_BUCKETS - max_exact)).astype(np.int32)
    large = np.minimum(large, REL_BUCKETS - 1)
    return np.where(dist < max_exact, dist, large).astype(np.int32)


def _attn_bias(table, window, dilation):
    blk = ATTN_BLOCK
    span = window // dilation
    qi = np.arange(blk)[:, None]
    ki = np.arange(2 * blk)[None, :]
    steps = qi + blk - ki
    band = (steps >= 0) & (steps <= span)
    bucket = _rel_bucket_np(np.clip(steps, 0, None) * dilation)
    onehot = np.eye(REL_BUCKETS, dtype=np.float32)[bucket]
    bias = jnp.einsum('qkb,bh->hqk', onehot, table.astype(F32), precision=lax.Precision.HIGHEST)
    b0 = jnp.where(band[None], bias, MASK_VALUE)
    b1 = jnp.where((band & (ki >= blk))[None], bias, MASK_VALUE)
    return jnp.stack([b0, b1], axis=0)


def _attn_kernel(q_ref, k_ref, v_ref, bias_ref, qg_ref, kg_ref, seg_ref, o_ref, lse_ref,
                 kcat_ref, vcat_ref):
    blk = ATTN_BLOCK
    i = pl.program_id(2)
    seg = seg_ref[...]

    def head_norm(t, gain):
        ss = _split_dot(t * t, seg)
        return t * lax.rsqrt(ss * (1.0 / HEAD_DIM) + EPS) * gain

    qn = (head_norm(q_ref[0, 0], qg_ref[...]) * (HEAD_DIM ** -0.5)).astype(BF16)

    @pl.when(i == 0)
    def _():
        kcat_ref[0:blk, :] = jnp.zeros((blk, GROUP_WIDTH), BF16)
        vcat_ref[0:blk, :] = jnp.zeros((blk, GROUP_WIDTH), BF16)

    @pl.when(i > 0)
    def _():
        kcat_ref[0:blk, :] = kcat_ref[blk:2 * blk, :]
        vcat_ref[0:blk, :] = vcat_ref[blk:2 * blk, :]

    kcat_ref[blk:2 * blk, :] = head_norm(k_ref[0, 0], kg_ref[...]).astype(BF16)
    vcat_ref[blk:2 * blk, :] = v_ref[0, 0].astype(BF16)
    first = (i == 0).astype(jnp.int32)

    lane = lax.broadcasted_iota(jnp.int32, (blk, LANES), 1)
    low_half = lane < HEAD_DIM
    lse_tile = jnp.zeros((blk, LANES), F32)
    outs = []
    for p in range(GROUP_WIDTH // LANES):
        qp = qn[:, p * LANES:(p + 1) * LANES]
        kp = kcat_ref[:, p * LANES:(p + 1) * LANES]
        vp = vcat_ref[:, p * LANES:(p + 1) * LANES]
        pair = []
        for hh in range(2):
            head = 2 * p + hh
            sel = low_half if hh == 0 else jnp.logical_not(low_half)
            qm = jnp.where(sel, qp, jnp.zeros_like(qp))
            s = lax.dot_general(qm, kp, (((1,), (1,)), ((), ())), preferred_element_type=F32)
            s = s + bias_ref[first, head]
            mx = jnp.max(s, axis=-1, keepdims=True)
            pe = jnp.exp(s - mx)
            den = jnp.sum(pe, axis=-1, keepdims=True)
            o = jnp.dot(pe.astype(BF16), vp, preferred_element_type=F32)
            pair.append(o * (1.0 / den))
            lse_tile = jnp.where(lane == head, mx + jnp.log(den), lse_tile)
        outs.append(jnp.where(low_half, pair[0], pair[1]))
    o_ref[0, 0] = jnp.concatenate(outs, axis=-1)
    lse_ref[0, 0] = lse_tile


def _attention_group(qkv, gi, bias, q_gain, k_gain, seg):
    b, dil, length, _ = qkv.shape
    blk = ATTN_BLOCK
    nblk = length // blk

    def col(c):
        return pl.BlockSpec((1, 1, blk, GROUP_WIDTH), lambda bi, r, i: (bi, r, i, c))

    const = lambda shape: pl.BlockSpec(shape, lambda bi, r, i: (0,) * len(shape))
    return pl.pallas_call(
        _attn_kernel,
        grid=(b, dil, nblk),
        in_specs=[
            col(0), col(1), col(2),
            const((2, HEADS_PER_GROUP, blk, 2 * blk)),
            const((1, GROUP_WIDTH)),
            const((1, GROUP_WIDTH)),
            const((GROUP_WIDTH, GROUP_WIDTH)),
        ],
        out_specs=[
            pl.BlockSpec((1, 1, blk, GROUP_WIDTH), lambda bi, r, i: (bi, r, i, 0)),
            pl.BlockSpec((1, 1, blk, LANES), lambda bi, r, i: (bi, r, i, 0)),
        ],
        out_shape=[
            jax.ShapeDtypeStruct((b, dil, length, GROUP_WIDTH), F32),
            jax.ShapeDtypeStruct((b, dil, length, LANES), F32),
        ],
        scratch_shapes=[
            pltpu.VMEM((2 * blk, GROUP_WIDTH), BF16),
            pltpu.VMEM((2 * blk, GROUP_WIDTH), BF16),
        ],
        compiler_params=_params(3),
        name=f"attn_g{gi}",
    )(qkv, qkv, qkv, bias,
      jnp.tile(q_gain, HEADS_PER_GROUP).reshape(1, GROUP_WIDTH),
      jnp.tile(k_gain, HEADS_PER_GROUP).reshape(1, GROUP_WIDTH), seg)


HALF_STATE = SSM_GROUPS * SSM_STATE // 2
SCAN_COLS = 4


def _ssm_tables(a_re, a_im, log_dt, b_re, b_im, c_re, c_im):
    lam = lax.complex(a_re.astype(F32), a_im.astype(F32))
    dt = jnp.exp(log_dt.astype(F32))[:, None]
    a_bar = jnp.exp(lam * dt)
    b_bar = ((a_bar - 1.0) / lam)[..., None] * lax.complex(b_re.astype(F32), b_im.astype(F32))

    rows = jnp.arange(SUBLANES)
    tabs = []
    for s in (1, 2, 4):
        p = jnp.exp(lam * dt * s).reshape(1, -1)
        tabs.append(jnp.where((rows >= s)[:, None], p, 0.0))
    tabs.append(jnp.exp((lam * dt).reshape(1, -1) * (rows + 1.0)[:, None]))
    tab = jnp.stack(tabs, axis=0)
    tab = jnp.stack([tab.real, tab.imag], axis=1).astype(F32)

    gh = SSM_GROUPS // 2
    eye = jnp.eye(gh, dtype=F32)

    def in_mat(m):
        return jnp.einsum('gnc,gh->gchn', m, eye).reshape(gh * SSM_GROUP, gh * SSM_STATE)

    def out_mat(m):
        return jnp.einsum('gcn,gh->gnhc', m, eye).reshape(gh * SSM_STATE, gh * SSM_GROUP)

    b_mats, c_mats = [], []
    for h in range(2):
        sl = slice(h * gh, (h + 1) * gh)
        b_mats.append(jnp.concatenate([in_mat(b_bar.real[sl]), in_mat(b_bar.imag[sl])], axis=1))
        c_mats.append(jnp.concatenate([out_mat(c_re.astype(F32)[sl]),
                                       -out_mat(c_im.astype(F32)[sl])], axis=0))
    return tab, jnp.stack(b_mats).astype(BF16), jnp.stack(c_mats).astype(BF16)


def _ssm_kernel(u_ref, tab_ref, bmat_ref, cmat_ref, d_ref, y_ref, xs_ref, carry_ref):
    tt = u_ref.shape[1]
    half_w = 2 * HALF_STATE

    @pl.when(pl.program_id(1) == 0)
    def _():
        carry_ref[...] = jnp.zeros_like(carry_ref)

    u = u_ref[0]
    ub = u.astype(BF16)
    hw = SSM_WIDTH // 2
    for h in range(2):
        xs_ref[:, h * half_w:(h + 1) * half_w] = jnp.dot(
            ub[:, h * hw:(h + 1) * hw], bmat_ref[h], preferred_element_type=F32)

    n_cols = 2 * HALF_STATE // LANES
    for c0 in range(0, n_cols, SCAN_COLS):
        cols = []
        for c in range(c0, c0 + SCAN_COLS):
            h, j = divmod(c, HALF_STATE // LANES)
            re_col = h * half_w + j * LANES
            cols.append((c * LANES, re_col, re_col + HALF_STATE))

        def body(rb, carry, cols=cols):
            r0 = pl.multiple_of(rb * SUBLANES, SUBLANES)
            new = []
            for (tc, rc, ic), (cr, ci) in zip(cols, carry):
                re = xs_ref[pl.ds(r0, SUBLANES), rc:rc + LANES]
                im = xs_ref[pl.ds(r0, SUBLANES), ic:ic + LANES]
                for lvl, s in enumerate((1, 2, 4)):
                    ar = tab_ref[lvl, 0, :, tc:tc + LANES]
                    ai = tab_ref[lvl, 1, :, tc:tc + LANES]
                    sr = pltpu.roll(re, s, 0)
                    si = pltpu.roll(im, s, 0)
                    re, im = re + ar * sr - ai * si, im + ar * si + ai * sr
                pr = tab_ref[3, 0, :, tc:tc + LANES]
                pi = tab_ref[3, 1, :, tc:tc + LANES]
                re, im = re + pr * cr - pi * ci, im + pr * ci + pi * cr
                xs_ref[pl.ds(r0, SUBLANES), rc:rc + LANES] = re
                xs_ref[pl.ds(r0, SUBLANES), ic:ic + LANES] = im
                new.append((jnp.broadcast_to(re[SUBLANES - 1:, :], (SUBLANES, LANES)),
                            jnp.broadcast_to(im[SUBLANES - 1:, :], (SUBLANES, LANES))))
            return tuple(new)

        init = tuple((carry_ref[:, rc:rc + LANES], carry_ref[:, ic:ic + LANES]) for _, rc, ic in cols)
        fin = lax.fori_loop(0, tt // SUBLANES, body, init)
        for (_, rc, ic), (cr, ci) in zip(cols, fin):
            carry_ref[:, rc:rc + LANES] = cr
            carry_ref[:, ic:ic + LANES] = ci

    ys = []
    for h in range(2):
        xh = xs_ref[:, h * half_w:(h + 1) * half_w].astype(BF16)
        ys.append(jnp.dot(xh, cmat_ref[h], preferred_element_type=F32))
    y_ref[0] = jnp.concatenate(ys, axis=-1) + d_ref[...] * u


def _ssm(rest, tab, bmat, cmat, d_skip):
    b, s, _ = rest.shape
    tt = min(SSM_TIME_TILE, s)
    n_state = 4 * HALF_STATE
    return pl.pallas_call(
        _ssm_kernel,
        grid=(b, s // tt),
        in_specs=[
            pl.BlockSpec((1, tt, SSM_WIDTH), lambda bi, j: (bi, j, U_BLOCK)),
            pl.BlockSpec(tab.shape, lambda bi, j: (0, 0, 0, 0)),
            pl.BlockSpec(bmat.shape, lambda bi, j: (0, 0, 0)),
            pl.BlockSpec(cmat.shape, lambda bi, j: (0, 0, 0)),
            pl.BlockSpec((1, SSM_WIDTH), lambda bi, j: (0, 0)),
        ],
        out_specs=pl.BlockSpec((1, tt, SSM_WIDTH), lambda bi, j: (bi, j, 0)),
        out_shape=jax.ShapeDtypeStruct((b, s, SSM_WIDTH), F32),
        scratch_shapes=[
            pltpu.VMEM((tt, n_state), F32),
            pltpu.VMEM((SUBLANES, n_state), F32),
        ],
        compiler_params=_params(2),
        name="ssm",
    )(rest, tab, bmat, cmat, d_skip.reshape(1, SSM_WIDTH)).reshape(b * s, SSM_WIDTH)


ROUTE_IDS, ROUTE_GATES, ROUTE_SLOTS = 0, TOP_K, 2 * TOP_K
RUN_START, RUN_COUNT, RUN_OFFSET = 0, 1, 2
SLOTS_PER_TILE = ROW_TILE * TOP_K


def _split_dot(x, w_bf16):
    hi = x.astype(BF16)
    lo = (x - hi.astype(F32)).astype(BF16)
    return (jnp.dot(hi, w_bf16, preferred_element_type=F32)
            + jnp.dot(lo, w_bf16, preferred_element_type=F32))


def _mix_kernel(o0_ref, o1_ref, o2_ref, l0_ref, l1_ref, l2_ref, ys_ref, ga_ref, gb_ref, x_ref,
                expand_ref, wa_ref, wglu_ref, bglu_ref, ws_ref, wo_ref, gffn_ref, wr_ref, br_ref,
                x1_ref, h2_ref, route_ref, counts_ref, runs_ref, run_ref, ostage_ref, lstage_ref):
    tm = x_ref.shape[0]

    @pl.when(jnp.logical_and(pl.program_id(0) == 0, pl.program_id(1) == 0))
    def _():
        run_ref[...] = jnp.zeros_like(run_ref)

    for slot, (o_ref, l_ref) in enumerate(((o1_ref, l1_ref), (o2_ref, l2_ref))):
        dil = ATTN_GROUPS[slot + 1][1]
        for r in range(dil):
            for c in range(GROUP_WIDTH // LANES):
                ostage_ref[slot, c, pl.ds(r, tm // dil, stride=dil), :] = o_ref[
                    0, r, :, c * LANES:(c + 1) * LANES]
            lstage_ref[slot, pl.ds(r, tm // dil, stride=dil), :] = l_ref[0, r]
    group_out = [o0_ref[0, 0]] + [
        jnp.concatenate([ostage_ref[slot, c] for c in range(GROUP_WIDTH // LANES)], axis=-1)
        for slot in range(2)]

    lses = [l0_ref[0, 0], lstage_ref[0], lstage_ref[1]]
    mx = jnp.maximum(jnp.maximum(lses[0], lses[1]), lses[2])
    es = [jnp.exp(l - mx) for l in lses]
    inv = 1.0 / (es[0] + es[1] + es[2])
    expand = expand_ref[...]
    attn = jnp.zeros((tm, GROUP_WIDTH), F32)
    for e, o_g in zip(es, group_out):
        attn = attn + _split_dot(e * inv, expand) * o_g
    y_a = jnp.dot(attn.astype(BF16), wa_ref[...], preferred_element_type=F32)

    ys = ys_ref[...]
    ys = 0.5 * ys * (1.0 + jnp.tanh(math.sqrt(2.0 / math.pi) * (ys + 0.044715 * (ys * ys * ys))))
    glu = jnp.dot(ys.astype(BF16), wglu_ref[...], preferred_element_type=F32) + bglu_ref[...]
    ys = ys * jax.nn.sigmoid(glu)
    y_b = jnp.dot(ys.astype(BF16), ws_ref[...], preferred_element_type=F32)

    mixed = jax.nn.sigmoid(ga_ref[...]) * y_a + jax.nn.sigmoid(gb_ref[...]) * y_b
    x1 = x_ref[...] + jnp.dot(mixed.astype(BF16), wo_ref[...], preferred_element_type=F32)
    x1_ref[...] = x1

    ms = jnp.mean(x1 * x1, axis=-1, keepdims=True)
    h2 = x1 * lax.rsqrt(ms + EPS) * gffn_ref[...]
    h2b = h2.astype(BF16)
    h2_ref[...] = h2b

    logits = jnp.dot(h2b, wr_ref[...], preferred_element_type=F32) + br_ref[...]
    lane = lax.broadcasted_iota(jnp.int32, (tm, LANES), 1)
    work = jnp.where(lane < N_EXPERTS, logits, -jnp.inf)
    sel_mask = jnp.zeros((tm, LANES), F32)
    route = jnp.zeros((tm, LANES), F32)
    vals, hots = [], []
    for k in range(TOP_K):
        v = jnp.max(work, axis=-1, keepdims=True)
        idx = jnp.min(jnp.where(work == v, lane, LANES), axis=-1, keepdims=True)
        hot = lane == idx
        work = jnp.where(hot, -jnp.inf, work)
        sel_mask = jnp.where(hot, 1.0, sel_mask)
        route = jnp.where(lane == ROUTE_IDS + k, idx.astype(F32), route)
        vals.append(v)
        hots.append(hot)
    exps = [jnp.exp(v - vals[0]) for v in vals]
    inv_den = 1.0 / (exps[0] + exps[1] + exps[2] + exps[3])

    r_i = lax.broadcasted_iota(jnp.int32, (tm, tm), 0)
    c_i = lax.broadcasted_iota(jnp.int32, (tm, tm), 1)
    tri = jnp.where(c_i < r_i, 1.0, 0.0).astype(BF16)
    local = jnp.dot(tri, sel_mask.astype(BF16), preferred_element_type=F32)
    count = jnp.sum(sel_mask, axis=0, keepdims=True)
    e_r = lax.broadcasted_iota(jnp.int32, (LANES, LANES), 0)
    e_c = lax.broadcasted_iota(jnp.int32, (LANES, LANES), 1)
    upper = jnp.where(e_r < e_c, 1.0, 0.0).astype(BF16)
    offset = jnp.dot(jnp.broadcast_to(count, (SUBLANES, LANES)).astype(BF16), upper,
                     preferred_element_type=F32)
    slot_of = local + offset[0:1, :]
    for k in range(TOP_K):
        slot = jnp.sum(jnp.where(hots[k], slot_of, 0.0), axis=-1, keepdims=True)
        route = jnp.where(lane == ROUTE_GATES + k, exps[k] * inv_den, route)
        route = jnp.where(lane == ROUTE_SLOTS + k, slot, route)
    route_ref[...] = route
    row8 = lax.broadcasted_iota(jnp.int32, (SUBLANES, LANES), 0)
    runs = jnp.where(row8 == RUN_START, run_ref[...],
                     jnp.where(row8 == RUN_COUNT, jnp.broadcast_to(count, (SUBLANES, LANES)),
                               jnp.where(row8 == RUN_OFFSET, offset, 0.0)))
    runs_ref[0] = runs
    total = run_ref[0:1, :] + count
    run_ref[...] = jnp.broadcast_to(total, run_ref.shape)
    counts_ref[...] = jnp.broadcast_to(total, counts_ref.shape)


def _mix(os_, lses, ys, rest, x2d, w):
    b, s, _ = rest.shape
    t = b * s
    tm = ROW_TILE
    n_i = s // tm
    rest2d = rest.reshape(t, REST_WIDTH)
    row = lambda width, blk=0: pl.BlockSpec((tm, width), lambda bi, i: (bi * n_i + i, blk))
    const = lambda a: pl.BlockSpec(a.shape, lambda bi, i: (0,) * a.ndim)
    grouped = lambda width: [pl.BlockSpec((1, d, tm // d, width), lambda bi, i: (bi, 0, i, 0))
                             for _, d in ATTN_GROUPS]
    consts = [w["expand"], w["wa"], w["wglu"], w["bglu"], w["ws"], w["wo"], w["gffn"], w["wr"], w["br"]]
    n_dilated = len(ATTN_GROUPS) - 1
    return pl.pallas_call(
        _mix_kernel,
        grid=(b, n_i),
        in_specs=grouped(GROUP_WIDTH) + grouped(LANES) + [
            row(SSM_WIDTH),
            row(D_MODEL, GATE_A_BLOCK),
            row(D_MODEL, GATE_B_BLOCK),
            row(D_MODEL),
        ] + [const(a) for a in consts],
        out_specs=[
            row(D_MODEL),
            row(D_MODEL),
            row(LANES),
            pl.BlockSpec((SUBLANES, LANES), lambda bi, i: (0, 0)),
            pl.BlockSpec((1, SUBLANES, LANES), lambda bi, i: (bi * n_i + i, 0, 0)),
        ],
        out_shape=[
            jax.ShapeDtypeStruct((t, D_MODEL), F32),
            jax.ShapeDtypeStruct((t, D_MODEL), BF16),
            jax.ShapeDtypeStruct((t, LANES), F32),
            jax.ShapeDtypeStruct((SUBLANES, LANES), F32),
            jax.ShapeDtypeStruct((t // tm, SUBLANES, LANES), F32),
        ],
        scratch_shapes=[
            pltpu.VMEM((SUBLANES, LANES), F32),
            pltpu.VMEM((n_dilated, GROUP_WIDTH // LANES, tm, LANES), F32),
            pltpu.VMEM((n_dilated, tm, LANES), F32),
        ],
        compiler_params=_params(2),
        name="mix_router",
    )(*os_, *lses, ys, rest2d, rest2d, x2d, *consts)


def _n_moe_tiles(t):
    return t * TOP_K // MOE_TILE + N_EXPERTS


def _copy_run(src_ref, dst_ref, src_row, dst_row, n_rows, sem):
    size = SLOTS_PER_TILE
    while size >= 1:
        @pl.when((n_rows & size) != 0)
        def _(size=size, src_row=src_row, dst_row=dst_row):
            pltpu.make_async_copy(
                src_ref.at[pl.ds(pl.multiple_of(src_row * SUBLANES, SUBLANES), size * SUBLANES), :],
                dst_ref.at[pl.ds(pl.multiple_of(dst_row * SUBLANES, SUBLANES), size * SUBLANES), :],
                sem).start()
        step = n_rows & size
        src_row = src_row + step
        dst_row = dst_row + step
        size //= 2


def _wait_tile(hbm_ref, vmem_ref, sem):
    pltpu.make_async_copy(hbm_ref.at[pl.ds(0, SLOTS_PER_TILE * SUBLANES), :], vmem_ref, sem).wait()


def _dispatch_kernel(counts_ref, start_ref, cnt_ref, toff_ref, h2_ref, route_ref,
                     xs_ref, texp_ref, off_ref, stage_ref, sems):
    i = pl.program_id(0)
    n_steps = pl.num_programs(0)
    n_tiles = texp_ref.shape[0] - 1

    @pl.when(i == 0)
    def _():
        def per_expert(e, start):
            off_ref[e] = start
            n_t = (counts_ref[e] + (MOE_TILE - 1)) // MOE_TILE

            def mark(ti, c):
                texp_ref[start // MOE_TILE + ti] = e
                return c

            lax.fori_loop(0, n_t, mark, 0)
            return start + n_t * MOE_TILE

        end = lax.fori_loop(0, N_EXPERTS, per_expert, 0)
        n_active = end // MOE_TILE
        texp_ref[n_tiles] = n_active

        def fill(ti, c):
            texp_ref[ti] = texp_ref[n_active - 1]
            return c

        lax.fori_loop(n_active, n_tiles, fill, 0)

    slots_t = route_ref[...].T
    slot_id = lax.broadcasted_iota(jnp.int32, (SLOTS_PER_TILE, ROW_TILE), 0).astype(F32)
    perm = jnp.zeros((SLOTS_PER_TILE, ROW_TILE), F32)
    for k in range(TOP_K):
        perm = jnp.where(slot_id == slots_t[ROUTE_SLOTS + k:ROUTE_SLOTS + k + 1, :], 1.0, perm)
    srt = jnp.dot(perm.astype(BF16), h2_ref[...], preferred_element_type=F32)
    buf = i % 2
    for s in range(D_MODEL // LANES):
        stage_ref[buf, pl.ds(s, SLOTS_PER_TILE, stride=SUBLANES), :] = srt[:, s * LANES:(s + 1) * LANES]

    @pl.when(i > 0)
    def _():
        _wait_tile(xs_ref, stage_ref.at[1 - buf], sems.at[1 - buf])

    def per_run(e, c):
        r = i * N_EXPERTS + e
        _copy_run(stage_ref.at[buf], xs_ref, toff_ref[r], off_ref[e] + start_ref[r], cnt_ref[r],
                  sems.at[buf])
        return c

    lax.fori_loop(0, N_EXPERTS, per_run, 0)

    @pl.when(i == n_steps - 1)
    def _():
        _wait_tile(xs_ref, stage_ref.at[buf], sems.at[buf])


def _dispatch(counts, runs, h2, route):
    t = h2.shape[0]
    tm = ROW_TILE
    n_tiles = _n_moe_tiles(t)
    grid_spec = pltpu.PrefetchScalarGridSpec(
        num_scalar_prefetch=4,
        grid=(t // tm,),
        in_specs=[
            pl.BlockSpec((tm, D_MODEL), lambda i, *_: (i, 0)),
            pl.BlockSpec((tm, LANES), lambda i, *_: (i, 0)),
        ],
        out_specs=[
            pl.BlockSpec(memory_space=pl.ANY),
            pl.BlockSpec((n_tiles + 1,), lambda i, *_: (0,), memory_space=pltpu.SMEM),
            pl.BlockSpec((N_EXPERTS,), lambda i, *_: (0,), memory_space=pltpu.SMEM),
        ],
        scratch_shapes=[
            pltpu.VMEM((2, SLOTS_PER_TILE * SUBLANES, LANES), F32),
            pltpu.SemaphoreType.DMA((2,)),
        ],
    )
    return pl.pallas_call(
        _dispatch_kernel,
        grid_spec=grid_spec,
        out_shape=[
            jax.ShapeDtypeStruct((n_tiles * MOE_TILE * SUBLANES, LANES), F32),
            jax.ShapeDtypeStruct((n_tiles + 1,), jnp.int32),
            jax.ShapeDtypeStruct((N_EXPERTS,), jnp.int32),
        ],
        compiler_params=_params(1),
        name="moe_dispatch",
    )(counts, *runs, h2, route)


def _ffn_kernel(texp_ref, xs_ref, wgu_ref, bgu_ref, wd_ref, bd_ref, out_ref, wgu_bf, wd_bf, last_ref):
    i = pl.program_id(0)
    tm = MOE_TILE
    n_active = texp_ref[pl.num_programs(0)]
    e = texp_ref[i]

    @pl.when(i == 0)
    def _():
        last_ref[0] = -1

    @pl.when(jnp.logical_and(i < n_active, e != last_ref[0]))
    def _():
        wgu_bf[...] = wgu_ref[0].astype(BF16)
        wd_bf[...] = wd_ref[0].astype(BF16)
        last_ref[0] = e

    @pl.when(i < n_active)
    def _():
        x = jnp.concatenate(
            [xs_ref[pl.ds(s, tm, stride=SUBLANES), :] for s in range(D_MODEL // LANES)], axis=-1)
        gu = jnp.dot(x.astype(BF16), wgu_bf[...], preferred_element_type=F32) + bgu_ref[0]
        x_glu = jnp.minimum(gu[:, :D_FF], SWIGLU_LIMIT)
        x_lin = jnp.clip(gu[:, D_FF:], -SWIGLU_LIMIT, SWIGLU_LIMIT)
        act = x_glu * jax.nn.sigmoid(SWIGLU_ALPHA * x_glu) * (x_lin + 1.0)
        out = jnp.dot(act.astype(BF16), wd_bf[...], preferred_element_type=F32) + bd_ref[0]
        for s in range(D_MODEL // LANES):
            out_ref[pl.ds(s, tm, stride=SUBLANES), :] = out[:, s * LANES:(s + 1) * LANES]


def _expert_ffn(texp, xs_tiles, w_gu, b_gu, w_down, b_down):
    n_tiles = texp.shape[0] - 1
    tm = MOE_TILE
    grid_spec = pltpu.PrefetchScalarGridSpec(
        num_scalar_prefetch=1,
        grid=(n_tiles,),
        in_specs=[
            pl.BlockSpec((tm * SUBLANES, LANES), lambda i, te: (jnp.minimum(i, te[n_tiles] - 1), 0)),
            pl.BlockSpec((1, D_MODEL, 2 * D_FF), lambda i, te: (te[i], 0, 0)),
            pl.BlockSpec((1, 1, 2 * D_FF), lambda i, te: (te[i], 0, 0)),
            pl.BlockSpec((1, D_FF, D_MODEL), lambda i, te: (te[i], 0, 0)),
            pl.BlockSpec((1, 1, D_MODEL), lambda i, te: (te[i], 0, 0)),
        ],
        out_specs=pl.BlockSpec((tm * SUBLANES, LANES), lambda i, te: (i, 0)),
        scratch_shapes=[
            pltpu.VMEM((D_MODEL, 2 * D_FF), BF16),
            pltpu.VMEM((D_FF, D_MODEL), BF16),
            pltpu.SMEM((1,), jnp.int32),
        ],
    )
    return pl.pallas_call(
        _ffn_kernel,
        grid_spec=grid_spec,
        out_shape=jax.ShapeDtypeStruct((n_tiles * tm * SUBLANES, LANES), F32),
        compiler_params=_params(1),
        name="moe_ffn",
    )(texp, xs_tiles, w_gu, b_gu.reshape(N_EXPERTS, 1, 2 * D_FF), w_down,
      b_down.reshape(N_EXPERTS, 1, D_MODEL))


def _combine_kernel(off_ref, start_ref, cnt_ref, toff_ref, route_ref, x1_ref, ys_ref, out_ref,
                    buf_ref, sems):
    i = pl.program_id(0)
    n_steps = pl.num_programs(0)
    tm = x1_ref.shape[0]

    def fetch(tile, buf):
        def per_run(e, c):
            r = tile * N_EXPERTS + e
            _copy_run(ys_ref, buf_ref.at[buf], off_ref[e] + start_ref[r], toff_ref[r], cnt_ref[r],
                      sems.at[buf])
            return c

        lax.fori_loop(0, N_EXPERTS, per_run, 0)

    @pl.when(i == 0)
    def _():
        fetch(0, 0)

    @pl.when(i + 1 < n_steps)
    def _():
        fetch(i + 1, (i + 1) % 2)

    buf = i % 2
    _wait_tile(ys_ref, buf_ref.at[buf], sems.at[buf])

    rows = jnp.concatenate(
        [buf_ref[buf, pl.ds(s, SLOTS_PER_TILE, stride=SUBLANES), :] for s in range(D_MODEL // LANES)],
        axis=-1).astype(BF16)
    route = route_ref[...]
    slot_id = lax.broadcasted_iota(jnp.int32, (tm, SLOTS_PER_TILE), 1).astype(F32)
    weights = jnp.zeros((tm, SLOTS_PER_TILE), F32)
    for k in range(TOP_K):
        slot = route[:, ROUTE_SLOTS + k:ROUTE_SLOTS + k + 1]
        gate = route[:, ROUTE_GATES + k:ROUTE_GATES + k + 1]
        weights = jnp.where(slot_id == slot, gate, weights)
    out_ref[...] = x1_ref[...] + jnp.dot(weights.astype(BF16), rows, preferred_element_type=F32)


def _combine(off, runs, route, x1, ys_tiles):
    t = x1.shape[0]
    tm = ROW_TILE
    grid_spec = pltpu.PrefetchScalarGridSpec(
        num_scalar_prefetch=4,
        grid=(t // tm,),
        in_specs=[
            pl.BlockSpec((tm, LANES), lambda i, *_: (i, 0)),
            pl.BlockSpec((tm, D_MODEL), lambda i, *_: (i, 0)),
            pl.BlockSpec(memory_space=pl.ANY),
        ],
        out_specs=pl.BlockSpec((tm, D_MODEL), lambda i, *_: (i, 0)),
        scratch_shapes=[
            pltpu.VMEM((2, SLOTS_PER_TILE * SUBLANES, LANES), F32),
            pltpu.SemaphoreType.DMA((2,)),
        ],
    )
    return pl.pallas_call(
        _combine_kernel,
        grid_spec=grid_spec,
        out_shape=jax.ShapeDtypeStruct((t, D_MODEL), F32),
        compiler_params=_params(1),
        name="moe_combine",
    )(off, *runs, route, x1, ys_tiles)


def _layer(x, g_mix, w_in, q_gain, k_gain, rel_bias, ssm_a_re, ssm_a_im, ssm_log_dt, ssm_b_re,
           ssm_b_im, ssm_c_re, ssm_c_im, ssm_d, w_glu, b_glu, w_attn_proj, w_ssm_proj, w_out,
           g_ffn, w_router, b_router, w_gate_up, b_gate_up, w_down, b_down):
    b, s, _ = x.shape
    x2d = x.reshape(b * s, D_MODEL)

    *qkvs, rest = _inproj(x, g_mix, w_in[:, _inproj_columns()].astype(BF16))

    head_of = np.arange(GROUP_WIDTH) // HEAD_DIM
    seg = jnp.asarray(head_of[:, None] == head_of[None, :], BF16)
    os_, lses = [], []
    for gi, (window, dil) in enumerate(ATTN_GROUPS):
        table = rel_bias[:, gi * HEADS_PER_GROUP:(gi + 1) * HEADS_PER_GROUP]
        o, lse = _attention_group(qkvs[gi], gi, _attn_bias(table, window, dil), q_gain[gi], k_gain[gi], seg)
        os_.append(o)
        lses.append(lse)

    tab, bmat, cmat = _ssm_tables(ssm_a_re, ssm_a_im, ssm_log_dt, ssm_b_re, ssm_b_im, ssm_c_re, ssm_c_im)
    ys = _ssm(rest, tab, bmat, cmat, ssm_d)

    expand = jnp.asarray(np.arange(LANES)[:, None] == head_of[None, :], BF16)
    pad_e = LANES - N_EXPERTS
    weights = dict(
        expand=expand, wa=w_attn_proj.astype(BF16), wglu=w_glu.astype(BF16),
        bglu=b_glu.reshape(1, SSM_WIDTH), ws=w_ssm_proj.astype(BF16), wo=w_out.astype(BF16),
        gffn=g_ffn.reshape(1, D_MODEL), wr=jnp.pad(w_router, ((0, 0), (0, pad_e))).astype(BF16),
        br=jnp.pad(b_router, (0, pad_e)).reshape(1, LANES))
    x1, h2, route, counts, run_tab = _mix(os_, lses, ys, rest, x2d, weights)

    counts_i = counts[0, :N_EXPERTS].astype(jnp.int32)
    run_table = run_tab[:, :, :N_EXPERTS].astype(jnp.int32)
    runs = [run_table[:, row].reshape(-1) for row in (RUN_START, RUN_COUNT, RUN_OFFSET)]
    xs_tiles, texp, off = _dispatch(counts_i, runs, h2, route)
    ys_tiles = _expert_ffn(texp, xs_tiles, w_gate_up, b_gate_up, w_down, b_down)
    out = _combine(off, runs, route, x1, ys_tiles)
    return out.reshape(b, s, D_MODEL)


_layer_jit = jax.jit(_layer)


def kernel(x, g_mix, w_in, q_gain, k_gain, rel_bias, ssm_a_re, ssm_a_im, ssm_log_dt, ssm_b_re, ssm_b_im, ssm_c_re, ssm_c_im, ssm_d, w_glu, b_glu, w_attn_proj, w_ssm_proj, w_out, g_ffn, w_router, b_router, w_gate_up, b_gate_up, w_down, b_down):
    return _layer_jit(x, g_mix[0], w_in[0], q_gain[0], k_gain[0], rel_bias, ssm_a_re[0], ssm_a_im[0],
                      ssm_log_dt[0], ssm_b_re[0], ssm_b_im[0], ssm_c_re[0], ssm_c_im[0], ssm_d[0],
                      w_glu[0], b_glu[0], w_attn_proj[0], w_ssm_proj[0], w_out[0], g_ffn[0],
                      w_router[0], b_router[0], w_gate_up[0], b_gate_up[0], w_down[0], b_down[0])
```

```python
import functools
import math

import jax
import jax.numpy as jnp
import numpy as np
from jax import lax
from jax.experimental import pallas as pl
from jax.experimental.pallas import tpu as pltpu

F32 = jnp.float32
BF16 = jnp.bfloat16

D_MODEL = 1024
HEAD_DIM = 64
ATTN_GROUPS = ((128, 1), (512, 4), (2048, 16))
HEADS_PER_GROUP = 8
GROUP_WIDTH = HEADS_PER_GROUP * HEAD_DIM
N_ATTN_HEADS = len(ATTN_GROUPS) * HEADS_PER_GROUP
QKV_WIDTH = 3 * N_ATTN_HEADS * HEAD_DIM
ATTN_BLOCK = 128
ATTN_STEP_BLOCKS = 4
REL_BUCKETS = 32
REL_MAX_DIST = 2048
SSM_WIDTH = 512
SSM_GROUP = 16
SSM_GROUPS = 32
SSM_STATE = 64
IN_WIDTH = QKV_WIDTH + SSM_WIDTH + 2 * D_MODEL
N_EXPERTS = 32
TOP_K = 4
D_FF = 1024
SWIGLU_LIMIT = 7.0
SWIGLU_ALPHA = 1.702
EPS = 1e-6

SUBLANES = 8
LANES = 128
MASK_VALUE = -1e30
VMEM_LIMIT = 56 * 1024 * 1024

QKV_GROUP_WIDTH = 3 * GROUP_WIDTH
REST_WIDTH = 2 * D_MODEL + SSM_WIDTH
GATE_A_BLOCK = 0
GATE_B_BLOCK = 1
U_BLOCK = 2 * D_MODEL // SSM_WIDTH

ROW_TILE = 256
SSM_TIME_TILE = 512
MOE_TILE = 512


def _params(n_axes, vmem=VMEM_LIMIT):
    return pltpu.CompilerParams(dimension_semantics=("arbitrary",) * n_axes, vmem_limit_bytes=vmem)


def _inproj_columns():
    n_g = len(ATTN_GROUPS)
    spans = [((part * n_g + gi) * GROUP_WIDTH, (part * n_g + gi + 1) * GROUP_WIDTH)
             for gi in range(n_g) for part in range(3)]
    return spans + [(QKV_WIDTH + SSM_WIDTH, IN_WIDTH), (QKV_WIDTH, QKV_WIDTH + SSM_WIDTH)]


def _inproj_kernel(x_ref, g_ref, w_ref, q0_ref, q1_ref, q2_ref, rest_ref, stage_ref):
    x = x_ref[0]
    tm = x.shape[0]
    ms = jnp.mean(x * x, axis=-1, keepdims=True)
    h = x * lax.rsqrt(ms + EPS) * g_ref[...]
    res = jnp.dot(h.astype(BF16), w_ref[...], preferred_element_type=F32)
    w = QKV_GROUP_WIDTH
    q0_ref[0, 0] = res[:, 0:w]
    rest_ref[0] = res[:, 3 * w:]
    n_blk = w // LANES
    for c in range(2 * n_blk):
        stage_ref[c] = res[:, w + c * LANES:w + (c + 1) * LANES]
    for out_ref, gi in ((q1_ref, 1), (q2_ref, 2)):
        dil = ATTN_GROUPS[gi][1]
        for r in range(dil):
            for c in range(n_blk):
                out_ref[0, r, :, c * LANES:(c + 1) * LANES] = stage_ref[
                    (gi - 1) * n_blk + c, pl.ds(r, tm // dil, stride=dil), :]


def _inproj(x, g, w_bf16):
    b, s, _ = x.shape
    tm = ROW_TILE
    dils = [d for _, d in ATTN_GROUPS]
    out_specs = [pl.BlockSpec((1, d, tm // d, QKV_GROUP_WIDTH), lambda bi, i: (bi, 0, i, 0)) for d in dils]
    out_shape = [jax.ShapeDtypeStruct((b, d, s // d, QKV_GROUP_WIDTH), F32) for d in dils]
    return pl.pallas_call(
        _inproj_kernel,
        grid=(b, s // tm),
        in_specs=[
            pl.BlockSpec((1, tm, D_MODEL), lambda bi, i: (bi, i, 0)),
            pl.BlockSpec((1, D_MODEL), lambda bi, i: (0, 0)),
            pl.BlockSpec((D_MODEL, IN_WIDTH), lambda bi, i: (0, 0), pipeline_mode=pl.Buffered(1)),
        ],
        out_specs=out_specs + [pl.BlockSpec((1, tm, REST_WIDTH), lambda bi, i: (bi, i, 0))],
        out_shape=out_shape + [jax.ShapeDtypeStruct((b, s, REST_WIDTH), F32)],
        scratch_shapes=[pltpu.VMEM((2 * QKV_GROUP_WIDTH // LANES, tm, LANES), F32)],
        compiler_params=_params(2),
        name="inproj",
    )(x, g.reshape(1, D_MODEL), w_bf16)


def _rel_bucket_np(dist):
    max_exact = REL_BUCKETS // 2
    d = np.maximum(dist, 1).astype(np.float32)
    large = max_exact + (np.log(d / max_exact) / math.log(REL_MAX_DIST / max_exact)
                         * (REL_BUCKETS - max_exact)).astype(np.int32)
    large = np.minimum(large, REL_BUCKETS - 1)
    return np.where(dist < max_exact, dist, large).astype(np.int32)


def _attn_bias(table, window, dilation):
    blk = ATTN_BLOCK
    span = window // dilation
    qi = np.arange(blk)[:, None]
    ki = np.arange(2 * blk)[None, :]
    steps = qi + blk - ki
    band = (steps >= 0) & (steps <= span)
    bucket = _rel_bucket_np(np.clip(steps, 0, None) * dilation)
    onehot = np.eye(REL_BUCKETS, dtype=np.float32)[bucket]
    bias = jnp.einsum('qkb,bh->hqk', onehot, table.astype(F32), precision=lax.Precision.HIGHEST)
    b0 = jnp.where(band[None], bias, MASK_VALUE)
    b1 = jnp.where((band & (ki >= blk))[None], bias, MASK_VALUE)
    return jnp.stack([b0, b1], axis=0)


def _attn_kernel(q_ref, k_ref, v_ref, bias_ref, qg_ref, kg_ref, seg_ref, o_ref, lse_ref,
                 qbuf_ref, kbuf_ref, vbuf_ref):
    blk = ATTN_BLOCK
    rows = q_ref.shape[2]
    i = pl.program_id(2)
    seg = seg_ref[...]

    def head_norm(t, gain):
        ss = _split_dot(t * t, seg)
        return t * lax.rsqrt(ss * (1.0 / HEAD_DIM) + EPS) * gain

    qbuf_ref[...] = (head_norm(q_ref[0, 0], qg_ref[...]) * (HEAD_DIM ** -0.5)).astype(BF16)

    @pl.when(i == 0)
    def _():
        kbuf_ref[0:blk, :] = jnp.zeros((blk, GROUP_WIDTH), BF16)
        vbuf_ref[0:blk, :] = jnp.zeros((blk, GROUP_WIDTH), BF16)

    @pl.when(i > 0)
    def _():
        kbuf_ref[0:blk, :] = kbuf_ref[rows:rows + blk, :]
        vbuf_ref[0:blk, :] = vbuf_ref[rows:rows + blk, :]

    kbuf_ref[blk:blk + rows, :] = head_norm(k_ref[0, 0], kg_ref[...]).astype(BF16)
    vbuf_ref[blk:blk + rows, :] = v_ref[0, 0].astype(BF16)
    first = (i == 0).astype(jnp.int32)

    lane = lax.broadcasted_iota(jnp.int32, (blk, LANES), 1)
    low_half = lane < HEAD_DIM
    for j in range(rows // blk):
        variant = first if j == 0 else 0
        lse_tile = jnp.zeros((blk, LANES), F32)
        outs = []
        for p in range(GROUP_WIDTH // LANES):
            qp = qbuf_ref[j * blk:(j + 1) * blk, p * LANES:(p + 1) * LANES]
            kp = kbuf_ref[j * blk:(j + 2) * blk, p * LANES:(p + 1) * LANES]
            vp = vbuf_ref[j * blk:(j + 2) * blk, p * LANES:(p + 1) * LANES]
            pair = []
            for hh in range(2):
                head = 2 * p + hh
                sel = low_half if hh == 0 else jnp.logical_not(low_half)
                qm = jnp.where(sel, qp, jnp.zeros_like(qp))
                s = lax.dot_general(qm, kp, (((1,), (1,)), ((), ())), preferred_element_type=F32)
                s = s + bias_ref[variant, head]
                mx = jnp.max(s, axis=-1, keepdims=True)
                pe = jnp.exp(s - mx)
                den = jnp.sum(pe, axis=-1, keepdims=True)
                o = jnp.dot(pe.astype(BF16), vp, preferred_element_type=F32)
                pair.append(o * (1.0 / den))
                lse_tile = jnp.where(lane == head, mx + jnp.log(den), lse_tile)
            outs.append(jnp.where(low_half, pair[0], pair[1]))
        o_ref[0, 0, j * blk:(j + 1) * blk, :] = jnp.concatenate(outs, axis=-1)
        lse_ref[0, 0, j * blk:(j + 1) * blk, :] = lse_tile


def _attention_group(qkv, gi, bias, q_gain, k_gain, seg):
    b, dil, length, _ = qkv.shape
    blk = ATTN_BLOCK
    rows = ATTN_STEP_BLOCKS * blk

    def col(c):
        return pl.BlockSpec((1, 1, rows, GROUP_WIDTH), lambda bi, r, i: (bi, r, i, c))

    const = lambda shape: pl.BlockSpec(shape, lambda bi, r, i: (0,) * len(shape))
    return pl.pallas_call(
        _attn_kernel,
        grid=(b, dil, length // rows),
        in_specs=[
            col(0), col(1), col(2),
            const((2, HEADS_PER_GROUP, blk, 2 * blk)),
            const((1, GROUP_WIDTH)),
            const((1, GROUP_WIDTH)),
            const((GROUP_WIDTH, GROUP_WIDTH)),
        ],
        out_specs=[
            pl.BlockSpec((1, 1, rows, GROUP_WIDTH), lambda bi, r, i: (bi, r, i, 0)),
            pl.BlockSpec((1, 1, rows, LANES), lambda bi, r, i: (bi, r, i, 0)),
        ],
        out_shape=[
            jax.ShapeDtypeStruct((b, dil, length, GROUP_WIDTH), F32),
            jax.ShapeDtypeStruct((b, dil, length, LANES), F32),
        ],
        scratch_shapes=[
            pltpu.VMEM((rows, GROUP_WIDTH), BF16),
            pltpu.VMEM((blk + rows, GROUP_WIDTH), BF16),
            pltpu.VMEM((blk + rows, GROUP_WIDTH), BF16),
        ],
        compiler_params=_params(3),
        name=f"attn_g{gi}",
    )(qkv, qkv, qkv, bias,
      jnp.tile(q_gain, HEADS_PER_GROUP).reshape(1, GROUP_WIDTH),
      jnp.tile(k_gain, HEADS_PER_GROUP).reshape(1, GROUP_WIDTH), seg)


HALF_STATE = SSM_GROUPS * SSM_STATE // 2
SCAN_COLS = 4


def _ssm_tables(a_re, a_im, log_dt, b_re, b_im, c_re, c_im):
    lam = lax.complex(a_re.astype(F32), a_im.astype(F32))
    dt = jnp.exp(log_dt.astype(F32))[:, None]
    a_bar = jnp.exp(lam * dt)
    b_bar = ((a_bar - 1.0) / lam)[..., None] * lax.complex(b_re.astype(F32), b_im.astype(F32))

    rows = jnp.arange(SUBLANES)
    tabs = []
    for s in (1, 2, 4):
        p = jnp.exp(lam * dt * s).reshape(1, -1)
        tabs.append(jnp.where((rows >= s)[:, None], p, 0.0))
    tabs.append(jnp.exp((lam * dt).reshape(1, -1) * (rows + 1.0)[:, None]))
    tab = jnp.stack(tabs, axis=0)
    tab = jnp.stack([tab.real, tab.imag], axis=1).astype(F32)

    gh = SSM_GROUPS // 2
    eye = jnp.eye(gh, dtype=F32)

    def in_mat(m):
        return jnp.einsum('gnc,gh->gchn', m, eye).reshape(gh * SSM_GROUP, gh * SSM_STATE)

    def out_mat(m):
        return jnp.einsum('gcn,gh->gnhc', m, eye).reshape(gh * SSM_STATE, gh * SSM_GROUP)

    b_mats, c_mats = [], []
    for h in range(2):
        sl = slice(h * gh, (h + 1) * gh)
        b_mats.append(jnp.concatenate([in_mat(b_bar.real[sl]), in_mat(b_bar.imag[sl])], axis=1))
        c_mats.append(jnp.concatenate([out_mat(c_re.astype(F32)[sl]),
                                       -out_mat(c_im.astype(F32)[sl])], axis=0))
    return tab, jnp.stack(b_mats).astype(BF16), jnp.stack(c_mats).astype(BF16)


def _ssm_kernel(u_ref, tab_ref, bmat_ref, cmat_ref, d_ref, y_ref, xs_ref, carry_ref):
    tt = u_ref.shape[1]
    half_w = 2 * HALF_STATE

    @pl.when(pl.program_id(1) == 0)
    def _():
        carry_ref[...] = jnp.zeros_like(carry_ref)

    u = u_ref[0]
    ub = u.astype(BF16)
    hw = SSM_WIDTH // 2
    for h in range(2):
        xs_ref[:, h * half_w:(h + 1) * half_w] = jnp.dot(
            ub[:, h * hw:(h + 1) * hw], bmat_ref[h], preferred_element_type=F32)

    n_cols = 2 * HALF_STATE // LANES
    for c0 in range(0, n_cols, SCAN_COLS):
        cols = []
        for c in range(c0, c0 + SCAN_COLS):
            h, j = divmod(c, HALF_STATE // LANES)
            re_col = h * half_w + j * LANES
            cols.append((c * LANES, re_col, re_col + HALF_STATE))

        def body(rb, carry, cols=cols):
            r0 = pl.multiple_of(rb * SUBLANES, SUBLANES)
            new = []
            for (tc, rc, ic), (cr, ci) in zip(cols, carry):
                re = xs_ref[pl.ds(r0, SUBLANES), rc:rc + LANES]
                im = xs_ref[pl.ds(r0, SUBLANES), ic:ic + LANES]
                for lvl, s in enumerate((1, 2, 4)):
                    ar = tab_ref[lvl, 0, :, tc:tc + LANES]
                    ai = tab_ref[lvl, 1, :, tc:tc + LANES]
                    sr = pltpu.roll(re, s, 0)
                    si = pltpu.roll(im, s, 0)
                    re, im = re + ar * sr - ai * si, im + ar * si + ai * sr
                pr = tab_ref[3, 0, :, tc:tc + LANES]
                pi = tab_ref[3, 1, :, tc:tc + LANES]
                re, im = re + pr * cr - pi * ci, im + pr * ci + pi * cr
                xs_ref[pl.ds(r0, SUBLANES), rc:rc + LANES] = re
                xs_ref[pl.ds(r0, SUBLANES), ic:ic + LANES] = im
                new.append((jnp.broadcast_to(re[SUBLANES - 1:, :], (SUBLANES, LANES)),
                            jnp.broadcast_to(im[SUBLANES - 1:, :], (SUBLANES, LANES))))
            return tuple(new)

        init = tuple((carry_ref[:, rc:rc + LANES], carry_ref[:, ic:ic + LANES]) for _, rc, ic in cols)
        fin = lax.fori_loop(0, tt // SUBLANES, body, init)
        for (_, rc, ic), (cr, ci) in zip(cols, fin):
            carry_ref[:, rc:rc + LANES] = cr
            carry_ref[:, ic:ic + LANES] = ci

    ys = []
    for h in range(2):
        xh = xs_ref[:, h * half_w:(h + 1) * half_w].astype(BF16)
        ys.append(jnp.dot(xh, cmat_ref[h], preferred_element_type=F32))
    y_ref[0] = jnp.concatenate(ys, axis=-1) + d_ref[...] * u


def _ssm(rest, tab, bmat, cmat, d_skip):
    b, s, _ = rest.shape
    tt = min(SSM_TIME_TILE, s)
    n_state = 4 * HALF_STATE
    return pl.pallas_call(
        _ssm_kernel,
        grid=(b, s // tt),
        in_specs=[
            pl.BlockSpec((1, tt, SSM_WIDTH), lambda bi, j: (bi, j, U_BLOCK)),
            pl.BlockSpec(tab.shape, lambda bi, j: (0, 0, 0, 0)),
            pl.BlockSpec(bmat.shape, lambda bi, j: (0, 0, 0)),
            pl.BlockSpec(cmat.shape, lambda bi, j: (0, 0, 0)),
            pl.BlockSpec((1, SSM_WIDTH), lambda bi, j: (0, 0)),
        ],
        out_specs=pl.BlockSpec((1, tt, SSM_WIDTH), lambda bi, j: (bi, j, 0)),
        out_shape=jax.ShapeDtypeStruct((b, s, SSM_WIDTH), F32),
        scratch_shapes=[
            pltpu.VMEM((tt, n_state), F32),
            pltpu.VMEM((SUBLANES, n_state), F32),
        ],
        compiler_params=_params(2),
        name="ssm",
    )(rest, tab, bmat, cmat, d_skip.reshape(1, SSM_WIDTH)).reshape(b * s, SSM_WIDTH)


ROUTE_IDS, ROUTE_GATES, ROUTE_SLOTS = 0, TOP_K, 2 * TOP_K
RUN_START, RUN_COUNT, RUN_OFFSET = 0, 1, 2
SLOTS_PER_TILE = ROW_TILE * TOP_K


def _split_dot(x, w_bf16):
    hi = x.astype(BF16)
    lo = (x - hi.astype(F32)).astype(BF16)
    return (jnp.dot(hi, w_bf16, preferred_element_type=F32)
            + jnp.dot(lo, w_bf16, preferred_element_type=F32))


def _mix_kernel(o0_ref, o1_ref, o2_ref, l0_ref, l1_ref, l2_ref, ys_ref, ga_ref, gb_ref, x_ref,
                expand_ref, wa_ref, wglu_ref, bglu_ref, ws_ref, wo_ref, gffn_ref, wr_ref, br_ref,
                x1_ref, h2_ref, route_ref, counts_ref, runs_ref, run_ref, ostage_ref, lstage_ref):
    tm = x_ref.shape[0]

    @pl.when(jnp.logical_and(pl.program_id(0) == 0, pl.program_id(1) == 0))
    def _():
        run_ref[...] = jnp.zeros_like(run_ref)

    for slot, (o_ref, l_ref) in enumerate(((o1_ref, l1_ref), (o2_ref, l2_ref))):
        dil = ATTN_GROUPS[slot + 1][1]
        for r in range(dil):
            for c in range(GROUP_WIDTH // LANES):
                ostage_ref[slot, c, pl.ds(r, tm // dil, stride=dil), :] = o_ref[
                    0, r, :, c * LANES:(c + 1) * LANES]
            lstage_ref[slot, pl.ds(r, tm // dil, stride=dil), :] = l_ref[0, r]
    group_out = [o0_ref[0, 0]] + [
        jnp.concatenate([ostage_ref[slot, c] for c in range(GROUP_WIDTH // LANES)], axis=-1)
        for slot in range(2)]

    lses = [l0_ref[0, 0], lstage_ref[0], lstage_ref[1]]
    mx = jnp.maximum(jnp.maximum(lses[0], lses[1]), lses[2])
    es = [jnp.exp(l - mx) for l in lses]
    inv = 1.0 / (es[0] + es[1] + es[2])
    expand = expand_ref[...]
    attn = jnp.zeros((tm, GROUP_WIDTH), F32)
    for e, o_g in zip(es, group_out):
        attn = attn + _split_dot(e * inv, expand) * o_g
    y_a = jnp.dot(attn.astype(BF16), wa_ref[...], preferred_element_type=F32)

    ys = ys_ref[...]
    ys = 0.5 * ys * (1.0 + jnp.tanh(math.sqrt(2.0 / math.pi) * (ys + 0.044715 * (ys * ys * ys))))
    glu = jnp.dot(ys.astype(BF16), wglu_ref[...], preferred_element_type=F32) + bglu_ref[...]
    ys = ys * jax.nn.sigmoid(glu)
    y_b = jnp.dot(ys.astype(BF16), ws_ref[...], preferred_element_type=F32)

    mixed = jax.nn.sigmoid(ga_ref[...]) * y_a + jax.nn.sigmoid(gb_ref[...]) * y_b
    x1 = x_ref[...] + jnp.dot(mixed.astype(BF16), wo_ref[...], preferred_element_type=F32)
    x1_ref[...] = x1

    ms = jnp.mean(x1 * x1, axis=-1, keepdims=True)
    h2 = x1 * lax.rsqrt(ms + EPS) * gffn_ref[...]
    h2b = h2.astype(BF16)
    h2_ref[...] = h2b

    logits = jnp.dot(h2b, wr_ref[...], preferred_element_type=F32) + br_ref[...]
    lane = lax.broadcasted_iota(jnp.int32, (tm, LANES), 1)
    work = jnp.where(lane < N_EXPERTS, logits, -jnp.inf)
    sel_mask = jnp.zeros((tm, LANES), F32)
    route = jnp.zeros((tm, LANES), F32)
    vals, hots = [], []
    for k in range(TOP_K):
        v = jnp.max(work, axis=-1, keepdims=True)
        idx = jnp.min(jnp.where(work == v, lane, LANES), axis=-1, keepdims=True)
        hot = lane == idx
        work = jnp.where(hot, -jnp.inf, work)
        sel_mask = jnp.where(hot, 1.0, sel_mask)
        route = jnp.where(lane == ROUTE_IDS + k, idx.astype(F32), route)
        vals.append(v)
        hots.append(hot)
    exps = [jnp.exp(v - vals[0]) for v in vals]
    inv_den = 1.0 / (exps[0] + exps[1] + exps[2] + exps[3])

    r_i = lax.broadcasted_iota(jnp.int32, (tm, tm), 0)
    c_i = lax.broadcasted_iota(jnp.int32, (tm, tm), 1)
    tri = jnp.where(c_i < r_i, 1.0, 0.0).astype(BF16)
    local = jnp.dot(tri, sel_mask.astype(BF16), preferred_element_type=F32)
    count = jnp.sum(sel_mask, axis=0, keepdims=True)
    e_r = lax.broadcasted_iota(jnp.int32, (LANES, LANES), 0)
    e_c = lax.broadcasted_iota(jnp.int32, (LANES, LANES), 1)
    upper = jnp.where(e_r < e_c, 1.0, 0.0).astype(BF16)
    offset = jnp.dot(jnp.broadcast_to(count, (SUBLANES, LANES)).astype(BF16), upper,
                     preferred_element_type=F32)
    slot_of = local + offset[0:1, :]
    for k in range(TOP_K):
        slot = jnp.sum(jnp.where(hots[k], slot_of, 0.0), axis=-1, keepdims=True)
        route = jnp.where(lane == ROUTE_GATES + k, exps[k] * inv_den, route)
        route = jnp.where(lane == ROUTE_SLOTS + k, slot, route)
    route_ref[...] = route
    row8 = lax.broadcasted_iota(jnp.int32, (SUBLANES, LANES), 0)
    runs = jnp.where(row8 == RUN_START, run_ref[...],
                     jnp.where(row8 == RUN_COUNT, jnp.broadcast_to(count, (SUBLANES, LANES)),
                               jnp.where(row8 == RUN_OFFSET, offset, 0.0)))
    runs_ref[0] = runs
    total = run_ref[0:1, :] + count
    run_ref[...] = jnp.broadcast_to(total, run_ref.shape)
    counts_ref[...] = jnp.broadcast_to(total, counts_ref.shape)


def _mix(os_, lses, ys, rest, x2d, w):
    b, s, _ = rest.shape
    t = b * s
    tm = ROW_TILE
    n_i = s // tm
    rest2d = rest.reshape(t, REST_WIDTH)
    row = lambda width, blk=0: pl.BlockSpec((tm, width), lambda bi, i: (bi * n_i + i, blk))
    const = lambda a: pl.BlockSpec(a.shape, lambda bi, i: (0,) * a.ndim)
    grouped = lambda width: [pl.BlockSpec((1, d, tm // d, width), lambda bi, i: (bi, 0, i, 0))
                             for _, d in ATTN_GROUPS]
    consts = [w["expand"], w["wa"], w["wglu"], w["bglu"], w["ws"], w["wo"], w["gffn"], w["wr"], w["br"]]
    n_dilated = len(ATTN_GROUPS) - 1
    return pl.pallas_call(
        _mix_kernel,
        grid=(b, n_i),
        in_specs=grouped(GROUP_WIDTH) + grouped(LANES) + [
            row(SSM_WIDTH),
            row(D_MODEL, GATE_A_BLOCK),
            row(D_MODEL, GATE_B_BLOCK),
            row(D_MODEL),
        ] + [const(a) for a in consts],
        out_specs=[
            row(D_MODEL),
            row(D_MODEL),
            row(LANES),
            pl.BlockSpec((SUBLANES, LANES), lambda bi, i: (0, 0)),
            pl.BlockSpec((1, SUBLANES, LANES), lambda bi, i: (bi * n_i + i, 0, 0)),
        ],
        out_shape=[
            jax.ShapeDtypeStruct((t, D_MODEL), F32),
            jax.ShapeDtypeStruct((t, D_MODEL), BF16),
            jax.ShapeDtypeStruct((t, LANES), F32),
            jax.ShapeDtypeStruct((SUBLANES, LANES), F32),
            jax.ShapeDtypeStruct((t // tm, SUBLANES, LANES), F32),
        ],
        scratch_shapes=[
            pltpu.VMEM((SUBLANES, LANES), F32),
            pltpu.VMEM((n_dilated, GROUP_WIDTH // LANES, tm, LANES), F32),
            pltpu.VMEM((n_dilated, tm, LANES), F32),
        ],
        compiler_params=_params(2),
        name="mix_router",
    )(*os_, *lses, ys, rest2d, rest2d, x2d, *consts)


def _n_moe_tiles(t):
    return t * TOP_K // MOE_TILE + N_EXPERTS


def _copy_run(src_ref, dst_ref, src_row, dst_row, n_rows, sem):
    size = SLOTS_PER_TILE
    while size >= 1:
        @pl.when((n_rows & size) != 0)
        def _(size=size, src_row=src_row, dst_row=dst_row):
            pltpu.make_async_copy(
                src_ref.at[pl.ds(pl.multiple_of(src_row * SUBLANES, SUBLANES), size * SUBLANES), :],
                dst_ref.at[pl.ds(pl.multiple_of(dst_row * SUBLANES, SUBLANES), size * SUBLANES), :],
                sem).start()
        step = n_rows & size
        src_row = src_row + step
        dst_row = dst_row + step
        size //= 2


def _wait_tile(hbm_ref, vmem_ref, sem):
    pltpu.make_async_copy(hbm_ref.at[pl.ds(0, SLOTS_PER_TILE * SUBLANES), :], vmem_ref, sem).wait()


def _dispatch_kernel(counts_ref, start_ref, cnt_ref, toff_ref, h2_ref, route_ref,
                     xs_ref, texp_ref, off_ref, stage_ref, sems):
    i = pl.program_id(0)
    n_steps = pl.num_programs(0)
    n_tiles = texp_ref.shape[0] - 1

    @pl.when(i == 0)
    def _():
        def per_expert(e, start):
            off_ref[e] = start
            n_t = (counts_ref[e] + (MOE_TILE - 1)) // MOE_TILE

            def mark(ti, c):
                texp_ref[start // MOE_TILE + ti] = e
                return c

            lax.fori_loop(0, n_t, mark, 0)
            return start + n_t * MOE_TILE

        end = lax.fori_loop(0, N_EXPERTS, per_expert, 0)
        n_active = end // MOE_TILE
        texp_ref[n_tiles] = n_active

        def fill(ti, c):
            texp_ref[ti] = texp_ref[n_active - 1]
            return c

        lax.fori_loop(n_active, n_tiles, fill, 0)

    slots_t = route_ref[...].T
    slot_id = lax.broadcasted_iota(jnp.int32, (SLOTS_PER_TILE, ROW_TILE), 0).astype(F32)
    perm = jnp.zeros((SLOTS_PER_TILE, ROW_TILE), F32)
    for k in range(TOP_K):
        perm = jnp.where(slot_id == slots_t[ROUTE_SLOTS + k:ROUTE_SLOTS + k + 1, :], 1.0, perm)
    srt = jnp.dot(perm.astype(BF16), h2_ref[...], preferred_element_type=F32)
    buf = i % 2
    for s in range(D_MODEL // LANES):
        stage_ref[buf, pl.ds(s, SLOTS_PER_TILE, stride=SUBLANES), :] = srt[:, s * LANES:(s + 1) * LANES]

    @pl.when(i > 0)
    def _():
        _wait_tile(xs_ref, stage_ref.at[1 - buf], sems.at[1 - buf])

    def per_run(e, c):
        r = i * N_EXPERTS + e
        _copy_run(stage_ref.at[buf], xs_ref, toff_ref[r], off_ref[e] + start_ref[r], cnt_ref[r],
                  sems.at[buf])
        return c

    lax.fori_loop(0, N_EXPERTS, per_run, 0)

    @pl.when(i == n_steps - 1)
    def _():
        _wait_tile(xs_ref, stage_ref.at[buf], sems.at[buf])


def _dispatch(counts, runs, h2, route):
    t = h2.shape[0]
    tm = ROW_TILE
    n_tiles = _n_moe_tiles(t)
    grid_spec = pltpu.PrefetchScalarGridSpec(
        num_scalar_prefetch=4,
        grid=(t // tm,),
        in_specs=[
            pl.BlockSpec((tm, D_MODEL), lambda i, *_: (i, 0)),
            pl.BlockSpec((tm, LANES), lambda i, *_: (i, 0)),
        ],
        out_specs=[
            pl.BlockSpec(memory_space=pl.ANY),
            pl.BlockSpec((n_tiles + 1,), lambda i, *_: (0,), memory_space=pltpu.SMEM),
            pl.BlockSpec((N_EXPERTS,), lambda i, *_: (0,), memory_space=pltpu.SMEM),
        ],
        scratch_shapes=[
            pltpu.VMEM((2, SLOTS_PER_TILE * SUBLANES, LANES), F32),
            pltpu.SemaphoreType.DMA((2,)),
        ],
    )
    return pl.pallas_call(
        _dispatch_kernel,
        grid_spec=grid_spec,
        out_shape=[
            jax.ShapeDtypeStruct((n_tiles * MOE_TILE * SUBLANES, LANES), F32),
            jax.ShapeDtypeStruct((n_tiles + 1,), jnp.int32),
            jax.ShapeDtypeStruct((N_EXPERTS,), jnp.int32),
        ],
        compiler_params=_params(1),
        name="moe_dispatch",
    )(counts, *runs, h2, route)


def _ffn_kernel(texp_ref, xs_ref, wgu_ref, bgu_ref, wd_ref, bd_ref, out_ref, wgu_bf, wd_bf, last_ref):
    i = pl.program_id(0)
    tm = MOE_TILE
    n_active = texp_ref[pl.num_programs(0)]
    e = texp_ref[i]

    @pl.when(i == 0)
    def _():
        last_ref[0] = -1

    @pl.when(jnp.logical_and(i < n_active, e != last_ref[0]))
    def _():
        wgu_bf[...] = wgu_ref[0].astype(BF16)
        wd_bf[...] = wd_ref[0].astype(BF16)
        last_ref[0] = e

    @pl.when(i < n_active)
    def _():
        x = jnp.concatenate(
            [xs_ref[pl.ds(s, tm, stride=SUBLANES), :] for s in range(D_MODEL // LANES)], axis=-1)
        gu = jnp.dot(x.astype(BF16), wgu_bf[...], preferred_element_type=F32) + bgu_ref[0]
        x_glu = jnp.minimum(gu[:, :D_FF], SWIGLU_LIMIT)
        x_lin = jnp.clip(gu[:, D_FF:], -SWIGLU_LIMIT, SWIGLU_LIMIT)
        act = x_glu * jax.nn.sigmoid(SWIGLU_ALPHA * x_glu) * (x_lin + 1.0)
        out = jnp.dot(act.astype(BF16), wd_bf[...], preferred_element_type=F32) + bd_ref[0]
        for s in range(D_MODEL // LANES):
            out_ref[pl.ds(s, tm, stride=SUBLANES), :] = out[:, s * LANES:(s + 1) * LANES]


def _expert_ffn(texp, xs_tiles, w_gu, b_gu, w_down, b_down):
    n_tiles = texp.shape[0] - 1
    tm = MOE_TILE
    grid_spec = pltpu.PrefetchScalarGridSpec(
        num_scalar_prefetch=1,
        grid=(n_tiles,),
        in_specs=[
            pl.BlockSpec((tm * SUBLANES, LANES), lambda i, te: (jnp.minimum(i, te[n_tiles] - 1), 0)),
            pl.BlockSpec((1, D_MODEL, 2 * D_FF), lambda i, te: (te[i], 0, 0)),
            pl.BlockSpec((1, 1, 2 * D_FF), lambda i, te: (te[i], 0, 0)),
            pl.BlockSpec((1, D_FF, D_MODEL), lambda i, te: (te[i], 0, 0)),
            pl.BlockSpec((1, 1, D_MODEL), lambda i, te: (te[i], 0, 0)),
        ],
        out_specs=pl.BlockSpec((tm * SUBLANES, LANES), lambda i, te: (i, 0)),
        scratch_shapes=[
            pltpu.VMEM((D_MODEL, 2 * D_FF), BF16),
            pltpu.VMEM((D_FF, D_MODEL), BF16),
            pltpu.SMEM((1,), jnp.int32),
        ],
    )
    return pl.pallas_call(
        _ffn_kernel,
        grid_spec=grid_spec,
        out_shape=jax.ShapeDtypeStruct((n_tiles * tm * SUBLANES, LANES), F32),
        compiler_params=_params(1),
        name="moe_ffn",
    )(texp, xs_tiles, w_gu, b_gu.reshape(N_EXPERTS, 1, 2 * D_FF), w_down,
      b_down.reshape(N_EXPERTS, 1, D_MODEL))


def _combine_kernel(off_ref, start_ref, cnt_ref, toff_ref, route_ref, x1_ref, ys_ref, out_ref,
                    buf_ref, sems):
    i = pl.program_id(0)
    n_steps = pl.num_programs(0)
    tm = x1_ref.shape[0]

    def fetch(tile, buf):
        def per_run(e, c):
            r = tile * N_EXPERTS + e
            _copy_run(ys_ref, buf_ref.at[buf], off_ref[e] + start_ref[r], toff_ref[r], cnt_ref[r],
                      sems.at[buf])
            return c

        lax.fori_loop(0, N_EXPERTS, per_run, 0)

    @pl.when(i == 0)
    def _():
        fetch(0, 0)

    @pl.when(i + 1 < n_steps)
    def _():
        fetch(i + 1, (i + 1) % 2)

    buf = i % 2
    _wait_tile(ys_ref, buf_ref.at[buf], sems.at[buf])

    rows = jnp.concatenate(
        [buf_ref[buf, pl.ds(s, SLOTS_PER_TILE, stride=SUBLANES), :] for s in range(D_MODEL // LANES)],
        axis=-1).astype(BF16)
    route = route_ref[...]
    slot_id = lax.broadcasted_iota(jnp.int32, (tm, SLOTS_PER_TILE), 1).astype(F32)
    weights = jnp.zeros((tm, SLOTS_PER_TILE), F32)
    for k in range(TOP_K):
        slot = route[:, ROUTE_SLOTS + k:ROUTE_SLOTS + k + 1]
        gate = route[:, ROUTE_GATES + k:ROUTE_GATES + k + 1]
        weights = jnp.where(slot_id == slot, gate, weights)
    out_ref[...] = x1_ref[...] + jnp.dot(weights.astype(BF16), rows, preferred_element_type=F32)


def _combine(off, runs, route, x1, ys_tiles):
    t = x1.shape[0]
    tm = ROW_TILE
    grid_spec = pltpu.PrefetchScalarGridSpec(
        num_scalar_prefetch=4,
        grid=(t // tm,),
        in_specs=[
            pl.BlockSpec((tm, LANES), lambda i, *_: (i, 0)),
            pl.BlockSpec((tm, D_MODEL), lambda i, *_: (i, 0)),
            pl.BlockSpec(memory_space=pl.ANY),
        ],
        out_specs=pl.BlockSpec((tm, D_MODEL), lambda i, *_: (i, 0)),
        scratch_shapes=[
            pltpu.VMEM((2, SLOTS_PER_TILE * SUBLANES, LANES), F32),
            pltpu.SemaphoreType.DMA((2,)),
        ],
    )
    return pl.pallas_call(
        _combine_kernel,
        grid_spec=grid_spec,
        out_shape=jax.ShapeDtypeStruct((t, D_MODEL), F32),
        compiler_params=_params(1),
        name="moe_combine",
    )(off, *runs, route, x1, ys_tiles)


def _layer(x, g_mix, w_in, q_gain, k_gain, rel_bias, ssm_a_re, ssm_a_im, ssm_log_dt, ssm_b_re,
           ssm_b_im, ssm_c_re, ssm_c_im, ssm_d, w_glu, b_glu, w_attn_proj, w_ssm_proj, w_out,
           g_ffn, w_router, b_router, w_gate_up, b_gate_up, w_down, b_down):
    b, s, _ = x.shape
    x2d = x.reshape(b * s, D_MODEL)

    w_in_b = w_in.astype(BF16)
    *qkvs, rest = _inproj(x, g_mix, jnp.concatenate([w_in_b[:, a:b] for a, b in _inproj_columns()], axis=1))

    head_of = np.arange(GROUP_WIDTH) // HEAD_DIM
    seg = jnp.asarray(head_of[:, None] == head_of[None, :], BF16)
    os_, lses = [], []
    for gi, (window, dil) in enumerate(ATTN_GROUPS):
        table = rel_bias[:, gi * HEADS_PER_GROUP:(gi + 1) * HEADS_PER_GROUP]
        o, lse = _attention_group(qkvs[gi], gi, _attn_bias(table, window, dil), q_gain[gi], k_gain[gi], seg)
        os_.append(o)
        lses.append(lse)

    tab, bmat, cmat = _ssm_tables(ssm_a_re, ssm_a_im, ssm_log_dt, ssm_b_re, ssm_b_im, ssm_c_re, ssm_c_im)
    ys = _ssm(rest, tab, bmat, cmat, ssm_d)

    expand = jnp.asarray(np.arange(LANES)[:, None] == head_of[None, :], BF16)
    pad_e = LANES - N_EXPERTS
    weights = dict(
        expand=expand, wa=w_attn_proj.astype(BF16), wglu=w_glu.astype(BF16),
        bglu=b_glu.reshape(1, SSM_WIDTH), ws=w_ssm_proj.astype(BF16), wo=w_out.astype(BF16),
        gffn=g_ffn.reshape(1, D_MODEL), wr=jnp.pad(w_router, ((0, 0), (0, pad_e))).astype(BF16),
        br=jnp.pad(b_router, (0, pad_e)).reshape(1, LANES))
    x1, h2, route, counts, run_tab = _mix(os_, lses, ys, rest, x2d, weights)

    counts_i = counts[0, :N_EXPERTS].astype(jnp.int32)
    run_table = run_tab[:, :, :N_EXPERTS].astype(jnp.int32)
    runs = [run_table[:, row].reshape(-1) for row in (RUN_START, RUN_COUNT, RUN_OFFSET)]
    xs_tiles, texp, off = _dispatch(counts_i, runs, h2, route)
    ys_tiles = _expert_ffn(texp, xs_tiles, w_gate_up, b_gate_up, w_down, b_down)
    out = _combine(off, runs, route, x1, ys_tiles)
    return out.reshape(b, s, D_MODEL)


_layer_jit = jax.jit(_layer)


def kernel(x, g_mix, w_in, q_gain, k_gain, rel_bias, ssm_a_re, ssm_a_im, ssm_log_dt, ssm_b_re, ssm_b_im, ssm_c_re, ssm_c_im, ssm_d, w_glu, b_glu, w_attn_proj, w_ssm_proj, w_out, g_ffn, w_router, b_router, w_gate_up, b_gate_up, w_down, b_down):
    return _layer_jit(x, g_mix[0], w_in[0], q_gain[0], k_gain[0], rel_bias, ssm_a_re[0], ssm_a_im[0],
                      ssm_log_dt[0], ssm_b_re[0], ssm_b_im[0], ssm_c_re[0], ssm_c_im[0], ssm_d[0],
                      w_glu[0], b_glu[0], w_attn_proj[0], w_ssm_proj[0], w_out[0], g_ffn[0],
                      w_router[0], b_router[0], w_gate_up[0], b_gate_up[0], w_down[0], b_down[0])
```

```python
import functools
import math

import jax
import jax.numpy as jnp
import numpy as np
from jax import lax
from jax.experimental import pallas as pl
from jax.experimental.pallas import tpu as pltpu

F32 = jnp.float32
BF16 = jnp.bfloat16

D_MODEL = 1024
HEAD_DIM = 64
ATTN_GROUPS = ((128, 1), (512, 4), (2048, 16))
HEADS_PER_GROUP = 8
GROUP_WIDTH = HEADS_PER_GROUP * HEAD_DIM
N_ATTN_HEADS = len(ATTN_GROUPS) * HEADS_PER_GROUP
QKV_WIDTH = 3 * N_ATTN_HEADS * HEAD_DIM
ATTN_BLOCK = 128
ATTN_STEP_BLOCKS = 4
REL_BUCKETS = 32
REL_MAX_DIST = 2048
SSM_WIDTH = 512
SSM_GROUP = 16
SSM_GROUPS = 32
SSM_STATE = 64
IN_WIDTH = QKV_WIDTH + SSM_WIDTH + 2 * D_MODEL
N_EXPERTS = 32
TOP_K = 4
D_FF = 1024
SWIGLU_LIMIT = 7.0
SWIGLU_ALPHA = 1.702
EPS = 1e-6

SUBLANES = 8
LANES = 128
MASK_VALUE = -1e30
VMEM_LIMIT = 56 * 1024 * 1024

QKV_GROUP_WIDTH = 3 * GROUP_WIDTH
REST_WIDTH = 2 * D_MODEL + SSM_WIDTH
GATE_A_BLOCK = 0
GATE_B_BLOCK = 1
U_BLOCK = 2 * D_MODEL // SSM_WIDTH

ROW_TILE = 256
SSM_TIME_TILE = 512
MOE_TILE = 512


def _params(n_axes, vmem=VMEM_LIMIT):
    return pltpu.CompilerParams(dimension_semantics=("arbitrary",) * n_axes, vmem_limit_bytes=vmem)


def _inproj_columns():
    n_g = len(ATTN_GROUPS)
    spans = [((part * n_g + gi) * GROUP_WIDTH, (part * n_g + gi + 1) * GROUP_WIDTH)
             for gi in range(n_g) for part in range(3)]
    return spans + [(QKV_WIDTH + SSM_WIDTH, IN_WIDTH), (QKV_WIDTH, QKV_WIDTH + SSM_WIDTH)]


def _inproj_kernel(x_ref, g_ref, w_ref, q0_ref, q1_ref, q2_ref, rest_ref, stage_ref):
    x = x_ref[0]
    tm = x.shape[0]
    ms = jnp.mean(x * x, axis=-1, keepdims=True)
    h = x * lax.rsqrt(ms + EPS) * g_ref[...]
    res = jnp.dot(h.astype(BF16), w_ref[...], preferred_element_type=F32)
    w = QKV_GROUP_WIDTH
    q0_ref[0, 0] = res[:, 0:w]
    rest_ref[0] = res[:, 3 * w:]
    n_blk = w // LANES
    for c in range(2 * n_blk):
        stage_ref[c] = res[:, w + c * LANES:w + (c + 1) * LANES]
    for out_ref, gi in ((q1_ref, 1), (q2_ref, 2)):
        dil = ATTN_GROUPS[gi][1]
        for r in range(dil):
            for c in range(n_blk):
                out_ref[0, r, :, c * LANES:(c + 1) * LANES] = stage_ref[
                    (gi - 1) * n_blk + c, pl.ds(r, tm // dil, stride=dil), :]


def _inproj(x, g, w_bf16):
    b, s, _ = x.shape
    tm = ROW_TILE
    dils = [d for _, d in ATTN_GROUPS]
    out_specs = [pl.BlockSpec((1, d, tm // d, QKV_GROUP_WIDTH), lambda bi, i: (bi, 0, i, 0)) for d in dils]
    out_shape = [jax.ShapeDtypeStruct((b, d, s // d, QKV_GROUP_WIDTH), F32) for d in dils]
    return pl.pallas_call(
        _inproj_kernel,
        grid=(b, s // tm),
        in_specs=[
            pl.BlockSpec((1, tm, D_MODEL), lambda bi, i: (bi, i, 0)),
            pl.BlockSpec((1, D_MODEL), lambda bi, i: (0, 0)),
            pl.BlockSpec((D_MODEL, IN_WIDTH), lambda bi, i: (0, 0), pipeline_mode=pl.Buffered(1)),
        ],
        out_specs=out_specs + [pl.BlockSpec((1, tm, REST_WIDTH), lambda bi, i: (bi, i, 0))],
        out_shape=out_shape + [jax.ShapeDtypeStruct((b, s, REST_WIDTH), F32)],
        scratch_shapes=[pltpu.VMEM((2 * QKV_GROUP_WIDTH // LANES, tm, LANES), F32)],
        compiler_params=_params(2),
        name="inproj",
    )(x, g.reshape(1, D_MODEL), w_bf16)


def _rel_bucket_np(dist):
    max_exact = REL_BUCKETS // 2
    d = np.maximum(dist, 1).astype(np.float32)
    large = max_exact + (np.log(d / max_exact) / math.log(REL_MAX_DIST / max_exact)
                         * (REL_BUCKETS - max_exact)).astype(np.int32)
    large = np.minimum(large, REL_BUCKETS - 1)
    return np.where(dist < max_exact, dist, large).astype(np.int32)


def _attn_bias(table, window, dilation):
    blk = ATTN_BLOCK
    span = window // dilation
    qi = np.arange(blk)[:, None]
    ki = np.arange(2 * blk)[None, :]
    steps = qi + blk - ki
    band = (steps >= 0) & (steps <= span)
    bucket = _rel_bucket_np(np.clip(steps, 0, None) * dilation)
    onehot = np.eye(REL_BUCKETS, dtype=np.float32)[bucket]
    bias = jnp.einsum('qkb,bh->hqk', onehot, table.astype(F32), precision=lax.Precision.HIGHEST)
    b0 = jnp.where(band[None], bias, MASK_VALUE)
    b1 = jnp.where((band & (ki >= blk))[None], bias, MASK_VALUE)
    return jnp.stack([b0, b1], axis=0)


def _attn_kernel(q_ref, k_ref, v_ref, bias_ref, qg_ref, kg_ref, seg_ref, o_ref, lse_ref,
                 qbuf_ref, kbuf_ref, vbuf_ref):
    blk = ATTN_BLOCK
    rows = q_ref.shape[2]
    i = pl.program_id(2)
    seg = seg_ref[...]

    def head_norm(t, gain):
        ss = jnp.dot((t * t).astype(BF16), seg, preferred_element_type=F32)
        return t * lax.rsqrt(ss * (1.0 / HEAD_DIM) + EPS) * gain

    qbuf_ref[...] = (head_norm(q_ref[0, 0], qg_ref[...]) * (HEAD_DIM ** -0.5)).astype(BF16)

    @pl.when(i == 0)
    def _():
        kbuf_ref[0:blk, :] = jnp.zeros((blk, GROUP_WIDTH), BF16)
        vbuf_ref[0:blk, :] = jnp.zeros((blk, GROUP_WIDTH), BF16)

    @pl.when(i > 0)
    def _():
        kbuf_ref[0:blk, :] = kbuf_ref[rows:rows + blk, :]
        vbuf_ref[0:blk, :] = vbuf_ref[rows:rows + blk, :]

    kbuf_ref[blk:blk + rows, :] = head_norm(k_ref[0, 0], kg_ref[...]).astype(BF16)
    vbuf_ref[blk:blk + rows, :] = v_ref[0, 0].astype(BF16)
    first = (i == 0).astype(jnp.int32)

    lane = lax.broadcasted_iota(jnp.int32, (blk, LANES), 1)
    low_half = lane < HEAD_DIM
    for j in range(rows // blk):
        variant = first if j == 0 else 0
        lse_tile = jnp.zeros((blk, LANES), F32)
        outs = []
        for p in range(GROUP_WIDTH // LANES):
            qp = qbuf_ref[j * blk:(j + 1) * blk, p * LANES:(p + 1) * LANES]
            kp = kbuf_ref[j * blk:(j + 2) * blk, p * LANES:(p + 1) * LANES]
            vp = vbuf_ref[j * blk:(j + 2) * blk, p * LANES:(p + 1) * LANES]
            pair = []
            for hh in range(2):
                head = 2 * p + hh
                sel = low_half if hh == 0 else jnp.logical_not(low_half)
                qm = jnp.where(sel, qp, jnp.zeros_like(qp))
                s = lax.dot_general(qm, kp, (((1,), (1,)), ((), ())), preferred_element_type=F32)
                s = s + bias_ref[variant, head]
                mx = jnp.max(s, axis=-1, keepdims=True)
                pe = jnp.exp(s - mx)
                den = jnp.sum(pe, axis=-1, keepdims=True)
                o = jnp.dot(pe.astype(BF16), vp, preferred_element_type=F32)
                pair.append(o * (1.0 / den))
                lse_tile = jnp.where(lane == head, mx + jnp.log(den), lse_tile)
            outs.append(jnp.where(low_half, pair[0], pair[1]))
        o_ref[0, 0, j * blk:(j + 1) * blk, :] = jnp.concatenate(outs, axis=-1)
        lse_ref[0, 0, j * blk:(j + 1) * blk, :] = lse_tile


def _attention_group(qkv, gi, bias, q_gain, k_gain, seg):
    b, dil, length, _ = qkv.shape
    blk = ATTN_BLOCK
    rows = ATTN_STEP_BLOCKS * blk

    def col(c):
        return pl.BlockSpec((1, 1, rows, GROUP_WIDTH), lambda bi, r, i: (bi, r, i, c))

    const = lambda shape: pl.BlockSpec(shape, lambda bi, r, i: (0,) * len(shape))
    return pl.pallas_call(
        _attn_kernel,
        grid=(b, dil, length // rows),
        in_specs=[
            col(0), col(1), col(2),
            const((2, HEADS_PER_GROUP, blk, 2 * blk)),
            const((1, GROUP_WIDTH)),
            const((1, GROUP_WIDTH)),
            const((GROUP_WIDTH, GROUP_WIDTH)),
        ],
        out_specs=[
            pl.BlockSpec((1, 1, rows, GROUP_WIDTH), lambda bi, r, i: (bi, r, i, 0)),
            pl.BlockSpec((1, 1, rows, LANES), lambda bi, r, i: (bi, r, i, 0)),
        ],
        out_shape=[
            jax.ShapeDtypeStruct((b, dil, length, GROUP_WIDTH), F32),
            jax.ShapeDtypeStruct((b, dil, length, LANES), F32),
        ],
        scratch_shapes=[
            pltpu.VMEM((rows, GROUP_WIDTH), BF16),
            pltpu.VMEM((blk + rows, GROUP_WIDTH), BF16),
            pltpu.VMEM((blk + rows, GROUP_WIDTH), BF16),
        ],
        compiler_params=_params(3),
        name=f"attn_g{gi}",
    )(qkv, qkv, qkv, bias,
      jnp.tile(q_gain, HEADS_PER_GROUP).reshape(1, GROUP_WIDTH),
      jnp.tile(k_gain, HEADS_PER_GROUP).reshape(1, GROUP_WIDTH), seg)


HALF_STATE = SSM_GROUPS * SSM_STATE // 2
SCAN_COLS = 4


def _ssm_tables(a_re, a_im, log_dt, b_re, b_im, c_re, c_im):
    lam = lax.complex(a_re.astype(F32), a_im.astype(F32))
    dt = jnp.exp(log_dt.astype(F32))[:, None]
    a_bar = jnp.exp(lam * dt)
    b_bar = ((a_bar - 1.0) / lam)[..., None] * lax.complex(b_re.astype(F32), b_im.astype(F32))

    rows = jnp.arange(SUBLANES)
    tabs = []
    for s in (1, 2, 4):
        p = jnp.exp(lam * dt * s).reshape(1, -1)
        tabs.append(jnp.where((rows >= s)[:, None], p, 0.0))
    tabs.append(jnp.exp((lam * dt).reshape(1, -1) * (rows + 1.0)[:, None]))
    tab = jnp.stack(tabs, axis=0)
    tab = jnp.stack([tab.real, tab.imag], axis=1).astype(F32)

    gh = SSM_GROUPS // 2
    eye = jnp.eye(gh, dtype=F32)

    def in_mat(m):
        return jnp.einsum('gnc,gh->gchn', m, eye).reshape(gh * SSM_GROUP, gh * SSM_STATE)

    def out_mat(m):
        return jnp.einsum('gcn,gh->gnhc', m, eye).reshape(gh * SSM_STATE, gh * SSM_GROUP)

    b_mats, c_mats = [], []
    for h in range(2):
        sl = slice(h * gh, (h + 1) * gh)
        b_mats.append(jnp.concatenate([in_mat(b_bar.real[sl]), in_mat(b_bar.imag[sl])], axis=1))
        c_mats.append(jnp.concatenate([out_mat(c_re.astype(F32)[sl]),
                                       -out_mat(c_im.astype(F32)[sl])], axis=0))
    return tab, jnp.stack(b_mats).astype(BF16), jnp.stack(c_mats).astype(BF16)


def _ssm_kernel(u_ref, tab_ref, bmat_ref, cmat_ref, d_ref, y_ref, xs_ref, carry_ref):
    tt = u_ref.shape[1]
    half_w = 2 * HALF_STATE

    @pl.when(pl.program_id(1) == 0)
    def _():
        carry_ref[...] = jnp.zeros_like(carry_ref)

    u = u_ref[0]
    ub = u.astype(BF16)
    hw = SSM_WIDTH // 2
    for h in range(2):
        xs_ref[:, h * half_w:(h + 1) * half_w] = jnp.dot(
            ub[:, h * hw:(h + 1) * hw], bmat_ref[h], preferred_element_type=F32)

    n_cols = 2 * HALF_STATE // LANES
    for c0 in range(0, n_cols, SCAN_COLS):
        cols = []
        for c in range(c0, c0 + SCAN_COLS):
            h, j = divmod(c, HALF_STATE // LANES)
            re_col = h * half_w + j * LANES
            cols.append((c * LANES, re_col, re_col + HALF_STATE))

        def body(rb, carry, cols=cols):
            r0 = pl.multiple_of(rb * SUBLANES, SUBLANES)
            new = []
            for (tc, rc, ic), (cr, ci) in zip(cols, carry):
                re = xs_ref[pl.ds(r0, SUBLANES), rc:rc + LANES]
                im = xs_ref[pl.ds(r0, SUBLANES), ic:ic + LANES]
                for lvl, s in enumerate((1, 2, 4)):
                    ar = tab_ref[lvl, 0, :, tc:tc + LANES]
                    ai = tab_ref[lvl, 1, :, tc:tc + LANES]
                    sr = pltpu.roll(re, s, 0)
                    si = pltpu.roll(im, s, 0)
                    re, im = re + ar * sr - ai * si, im + ar * si + ai * sr
                pr = tab_ref[3, 0, :, tc:tc + LANES]
                pi = tab_ref[3, 1, :, tc:tc + LANES]
                re, im = re + pr * cr - pi * ci, im + pr * ci + pi * cr
                xs_ref[pl.ds(r0, SUBLANES), rc:rc + LANES] = re
                xs_ref[pl.ds(r0, SUBLANES), ic:ic + LANES] = im
                new.append((jnp.broadcast_to(re[SUBLANES - 1:, :], (SUBLANES, LANES)),
                            jnp.broadcast_to(im[SUBLANES - 1:, :], (SUBLANES, LANES))))
            return tuple(new)

        init = tuple((carry_ref[:, rc:rc + LANES], carry_ref[:, ic:ic + LANES]) for _, rc, ic in cols)
        fin = lax.fori_loop(0, tt // SUBLANES, body, init)
        for (_, rc, ic), (cr, ci) in zip(cols, fin):
            carry_ref[:, rc:rc + LANES] = cr
            carry_ref[:, ic:ic + LANES] = ci

    ys = []
    for h in range(2):
        xh = xs_ref[:, h * half_w:(h + 1) * half_w].astype(BF16)
        ys.append(jnp.dot(xh, cmat_ref[h], preferred_element_type=F32))
    y_ref[0] = jnp.concatenate(ys, axis=-1) + d_ref[...] * u


def _ssm(rest, tab, bmat, cmat, d_skip):
    b, s, _ = rest.shape
    tt = min(SSM_TIME_TILE, s)
    n_state = 4 * HALF_STATE
    return pl.pallas_call(
        _ssm_kernel,
        grid=(b, s // tt),
        in_specs=[
            pl.BlockSpec((1, tt, SSM_WIDTH), lambda bi, j: (bi, j, U_BLOCK)),
            pl.BlockSpec(tab.shape, lambda bi, j: (0, 0, 0, 0)),
            pl.BlockSpec(bmat.shape, lambda bi, j: (0, 0, 0)),
            pl.BlockSpec(cmat.shape, lambda bi, j: (0, 0, 0)),
            pl.BlockSpec((1, SSM_WIDTH), lambda bi, j: (0, 0)),
        ],
        out_specs=pl.BlockSpec((1, tt, SSM_WIDTH), lambda bi, j: (bi, j, 0)),
        out_shape=jax.ShapeDtypeStruct((b, s, SSM_WIDTH), F32),
        scratch_shapes=[
            pltpu.VMEM((tt, n_state), F32),
            pltpu.VMEM((SUBLANES, n_state), F32),
        ],
        compiler_params=_params(2),
        name="ssm",
    )(rest, tab, bmat, cmat, d_skip.reshape(1, SSM_WIDTH)).reshape(b * s, SSM_WIDTH)


ROUTE_IDS, ROUTE_GATES, ROUTE_SLOTS = 0, TOP_K, 2 * TOP_K
RUN_START, RUN_COUNT, RUN_OFFSET = 0, 1, 2
SLOTS_PER_TILE = ROW_TILE * TOP_K


def _mix_kernel(o0_ref, o1_ref, o2_ref, l0_ref, l1_ref, l2_ref, ys_ref, ga_ref, gb_ref, x_ref,
                expand_ref, wa_ref, wglu_ref, bglu_ref, ws_ref, wo_ref, gffn_ref, wr_ref, br_ref,
                x1_ref, h2_ref, route_ref, counts_ref, runs_ref, run_ref, ostage_ref, lstage_ref):
    tm = x_ref.shape[0]

    @pl.when(jnp.logical_and(pl.program_id(0) == 0, pl.program_id(1) == 0))
    def _():
        run_ref[...] = jnp.zeros_like(run_ref)

    for slot, (o_ref, l_ref) in enumerate(((o1_ref, l1_ref), (o2_ref, l2_ref))):
        dil = ATTN_GROUPS[slot + 1][1]
        for r in range(dil):
            for c in range(GROUP_WIDTH // LANES):
                ostage_ref[slot, c, pl.ds(r, tm // dil, stride=dil), :] = o_ref[
                    0, r, :, c * LANES:(c + 1) * LANES]
            lstage_ref[slot, pl.ds(r, tm // dil, stride=dil), :] = l_ref[0, r]

    def sigmoid(z):
        return 0.5 * jnp.tanh(0.5 * z) + 0.5

    expand = expand_ref[...]
    n_lane_blk = GROUP_WIDTH // LANES

    def mix_rows(lo, n):
        rows = slice(lo, lo + n)
        group_out = [o0_ref[0, 0, rows, :]] + [
            jnp.concatenate([ostage_ref[slot, c, rows, :] for c in range(n_lane_blk)], axis=-1)
            for slot in range(2)]
        lses = [l0_ref[0, 0, rows, :], lstage_ref[0, rows, :], lstage_ref[1, rows, :]]
        mx = jnp.maximum(jnp.maximum(lses[0], lses[1]), lses[2])
        es = [jnp.exp(l - mx) for l in lses]
        inv = 1.0 / (es[0] + es[1] + es[2])
        attn = jnp.zeros((n, GROUP_WIDTH), F32)
        for e, o_g in zip(es, group_out):
            attn = attn + jnp.dot((e * inv).astype(BF16), expand, preferred_element_type=F32) * o_g
        y_a = jnp.dot(attn.astype(BF16), wa_ref[...], preferred_element_type=F32)

        ys = ys_ref[rows, :]
        ys = 0.5 * ys * (1.0 + jnp.tanh(math.sqrt(2.0 / math.pi) * (ys + 0.044715 * (ys * ys * ys))))
        glu = jnp.dot(ys.astype(BF16), wglu_ref[...], preferred_element_type=F32) + bglu_ref[...]
        ys = ys * sigmoid(glu)
        y_b = jnp.dot(ys.astype(BF16), ws_ref[...], preferred_element_type=F32)

        mixed = sigmoid(ga_ref[rows, :]) * y_a + sigmoid(gb_ref[rows, :]) * y_b
        x1 = x_ref[rows, :] + jnp.dot(mixed.astype(BF16), wo_ref[...], preferred_element_type=F32)
        x1_ref[rows, :] = x1

        ms = jnp.mean(x1 * x1, axis=-1, keepdims=True)
        h2b = (x1 * lax.rsqrt(ms + EPS) * gffn_ref[...]).astype(BF16)
        h2_ref[rows, :] = h2b
        return jnp.dot(h2b, wr_ref[...], preferred_element_type=F32) + br_ref[...]

    half = tm // 2
    logits = jnp.concatenate([mix_rows(0, half), mix_rows(half, half)], axis=0)

    lane = lax.broadcasted_iota(jnp.int32, (tm, LANES), 1)
    lane_f = lane.astype(F32)
    work = jnp.where(lane < N_EXPERTS, logits, -jnp.inf)
    sel_mask = jnp.zeros((tm, LANES), F32)
    route = jnp.zeros((tm, LANES), F32)
    vals, hots = [], []
    for k in range(TOP_K):
        v = jnp.max(work, axis=-1, keepdims=True)
        idx = jnp.min(jnp.where(work == v, lane_f, float(LANES)), axis=-1, keepdims=True)
        hot = lane_f == idx
        work = jnp.where(hot, -jnp.inf, work)
        sel_mask = jnp.where(hot, 1.0, sel_mask)
        route = jnp.where(lane == ROUTE_IDS + k, idx, route)
        vals.append(v)
        hots.append(hot)
    exps = [jnp.exp(v - vals[0]) for v in vals]
    inv_den = 1.0 / (exps[0] + exps[1] + exps[2] + exps[3])

    r_i = lax.broadcasted_iota(jnp.int32, (tm, tm), 0)
    c_i = lax.broadcasted_iota(jnp.int32, (tm, tm), 1)
    tri = jnp.where(c_i < r_i, 1.0, 0.0).astype(BF16)
    local = jnp.dot(tri, sel_mask.astype(BF16), preferred_element_type=F32)
    count = jnp.sum(sel_mask, axis=0, keepdims=True)
    e_r = lax.broadcasted_iota(jnp.int32, (LANES, LANES), 0)
    e_c = lax.broadcasted_iota(jnp.int32, (LANES, LANES), 1)
    upper = jnp.where(e_r < e_c, 1.0, 0.0).astype(BF16)
    offset = jnp.dot(jnp.broadcast_to(count, (SUBLANES, LANES)).astype(BF16), upper,
                     preferred_element_type=F32)
    slot_of = local + offset[0:1, :]
    for k in range(TOP_K):
        slot = jnp.sum(jnp.where(hots[k], slot_of, 0.0), axis=-1, keepdims=True)
        route = jnp.where(lane == ROUTE_GATES + k, exps[k] * inv_den, route)
        route = jnp.where(lane == ROUTE_SLOTS + k, slot, route)
    route_ref[...] = route
    row8 = lax.broadcasted_iota(jnp.int32, (SUBLANES, LANES), 0)
    runs = jnp.where(row8 == RUN_START, run_ref[...],
                     jnp.where(row8 == RUN_COUNT, jnp.broadcast_to(count, (SUBLANES, LANES)),
                               jnp.where(row8 == RUN_OFFSET, offset, 0.0)))
    runs_ref[0] = runs
    total = run_ref[0:1, :] + count
    run_ref[...] = jnp.broadcast_to(total, run_ref.shape)
    counts_ref[...] = jnp.broadcast_to(total, counts_ref.shape)


def _mix(os_, lses, ys, rest, x2d, w):
    b, s, _ = rest.shape
    t = b * s
    tm = ROW_TILE
    n_i = s // tm
    rest2d = rest.reshape(t, REST_WIDTH)
    row = lambda width, blk=0: pl.BlockSpec((tm, width), lambda bi, i: (bi * n_i + i, blk))
    const = lambda a: pl.BlockSpec(a.shape, lambda bi, i: (0,) * a.ndim)
    grouped = lambda width: [pl.BlockSpec((1, d, tm // d, width), lambda bi, i: (bi, 0, i, 0))
                             for _, d in ATTN_GROUPS]
    consts = [w["expand"], w["wa"], w["wglu"], w["bglu"], w["ws"], w["wo"], w["gffn"], w["wr"], w["br"]]
    n_dilated = len(ATTN_GROUPS) - 1
    return pl.pallas_call(
        _mix_kernel,
        grid=(b, n_i),
        in_specs=grouped(GROUP_WIDTH) + grouped(LANES) + [
            row(SSM_WIDTH),
            row(D_MODEL, GATE_A_BLOCK),
            row(D_MODEL, GATE_B_BLOCK),
            row(D_MODEL),
        ] + [const(a) for a in consts],
        out_specs=[
            row(D_MODEL),
            row(D_MODEL),
            row(LANES),
            pl.BlockSpec((SUBLANES, LANES), lambda bi, i: (0, 0)),
            pl.BlockSpec((1, SUBLANES, LANES), lambda bi, i: (bi * n_i + i, 0, 0)),
        ],
        out_shape=[
            jax.ShapeDtypeStruct((t, D_MODEL), F32),
            jax.ShapeDtypeStruct((t, D_MODEL), BF16),
            jax.ShapeDtypeStruct((t, LANES), F32),
            jax.ShapeDtypeStruct((SUBLANES, LANES), F32),
            jax.ShapeDtypeStruct((t // tm, SUBLANES, LANES), F32),
        ],
        scratch_shapes=[
            pltpu.VMEM((SUBLANES, LANES), F32),
            pltpu.VMEM((n_dilated, GROUP_WIDTH // LANES, tm, LANES), F32),
            pltpu.VMEM((n_dilated, tm, LANES), F32),
        ],
        compiler_params=_params(2),
        name="mix_router",
    )(*os_, *lses, ys, rest2d, rest2d, x2d, *consts)


def _n_moe_tiles(t):
    return t * TOP_K // MOE_TILE + N_EXPERTS


def _copy_run(src_ref, dst_ref, src_row, dst_row, n_rows, sem):
    size = ROW_TILE
    while size >= 1:
        @pl.when((n_rows & size) != 0)
        def _(size=size, src_row=src_row, dst_row=dst_row):
            pltpu.make_async_copy(
                src_ref.at[pl.ds(pl.multiple_of(src_row * SUBLANES, SUBLANES), size * SUBLANES), :],
                dst_ref.at[pl.ds(pl.multiple_of(dst_row * SUBLANES, SUBLANES), size * SUBLANES), :],
                sem).start()
        step = n_rows & size
        src_row = src_row + step
        dst_row = dst_row + step
        size //= 2


def _wait_tile(hbm_ref, vmem_ref, sem):
    pltpu.make_async_copy(hbm_ref.at[pl.ds(0, SLOTS_PER_TILE * SUBLANES), :], vmem_ref, sem).wait()


def _dispatch_kernel(counts_ref, start_ref, cnt_ref, toff_ref, h2_ref, route_ref,
                     xs_ref, texp_ref, off_ref, stage_ref, sems):
    i = pl.program_id(0)
    n_steps = pl.num_programs(0)
    n_tiles = texp_ref.shape[0] - 1

    @pl.when(i == 0)
    def _():
        def per_expert(e, start):
            off_ref[e] = start
            n_t = (counts_ref[e] + (MOE_TILE - 1)) // MOE_TILE

            def mark(ti, c):
                texp_ref[start // MOE_TILE + ti] = e
                return c

            lax.fori_loop(0, n_t, mark, 0)
            return start + n_t * MOE_TILE

        end = lax.fori_loop(0, N_EXPERTS, per_expert, 0)
        n_active = end // MOE_TILE
        texp_ref[n_tiles] = n_active

        def fill(ti, c):
            texp_ref[ti] = texp_ref[n_active - 1]
            return c

        lax.fori_loop(n_active, n_tiles, fill, 0)

    slots_t = route_ref[...].T
    slot_id = lax.broadcasted_iota(jnp.int32, (SLOTS_PER_TILE, ROW_TILE), 0).astype(F32)
    perm = jnp.zeros((SLOTS_PER_TILE, ROW_TILE), F32)
    for k in range(TOP_K):
        perm = jnp.where(slot_id == slots_t[ROUTE_SLOTS + k:ROUTE_SLOTS + k + 1, :], 1.0, perm)
    srt = jnp.dot(perm.astype(BF16), h2_ref[...], preferred_element_type=F32)
    buf = i % 2
    for s in range(D_MODEL // LANES):
        stage_ref[buf, pl.ds(s, SLOTS_PER_TILE, stride=SUBLANES), :] = srt[:, s * LANES:(s + 1) * LANES]

    @pl.when(i > 0)
    def _():
        _wait_tile(xs_ref, stage_ref.at[1 - buf], sems.at[1 - buf])

    def per_run(e, c):
        r = i * N_EXPERTS + e
        _copy_run(stage_ref.at[buf], xs_ref, toff_ref[r], off_ref[e] + start_ref[r], cnt_ref[r],
                  sems.at[buf])
        return c

    lax.fori_loop(0, N_EXPERTS, per_run, 0)

    @pl.when(i == n_steps - 1)
    def _():
        _wait_tile(xs_ref, stage_ref.at[buf], sems.at[buf])


def _dispatch(counts, runs, h2, route):
    t = h2.shape[0]
    tm = ROW_TILE
    n_tiles = _n_moe_tiles(t)
    grid_spec = pltpu.PrefetchScalarGridSpec(
        num_scalar_prefetch=4,
        grid=(t // tm,),
        in_specs=[
            pl.BlockSpec((tm, D_MODEL), lambda i, *_: (i, 0)),
            pl.BlockSpec((tm, LANES), lambda i, *_: (i, 0)),
        ],
        out_specs=[
            pl.BlockSpec(memory_space=pl.ANY),
            pl.BlockSpec((n_tiles + 1,), lambda i, *_: (0,), memory_space=pltpu.SMEM),
            pl.BlockSpec((N_EXPERTS,), lambda i, *_: (0,), memory_space=pltpu.SMEM),
        ],
        scratch_shapes=[
            pltpu.VMEM((2, SLOTS_PER_TILE * SUBLANES, LANES), F32),
            pltpu.SemaphoreType.DMA((2,)),
        ],
    )
    return pl.pallas_call(
        _dispatch_kernel,
        grid_spec=grid_spec,
        out_shape=[
            jax.ShapeDtypeStruct((n_tiles * MOE_TILE * SUBLANES, LANES), F32),
            jax.ShapeDtypeStruct((n_tiles + 1,), jnp.int32),
            jax.ShapeDtypeStruct((N_EXPERTS,), jnp.int32),
        ],
        compiler_params=_params(1),
        name="moe_dispatch",
    )(counts, *runs, h2, route)


def _ffn_kernel(texp_ref, xs_ref, wgu_ref, bgu_ref, wd_ref, bd_ref, out_ref, wgu_bf, wd_bf, last_ref):
    i = pl.program_id(0)
    tm = MOE_TILE
    n_active = texp_ref[pl.num_programs(0)]
    e = texp_ref[i]

    @pl.when(i == 0)
    def _():
        last_ref[0] = -1

    @pl.when(jnp.logical_and(i < n_active, e != last_ref[0]))
    def _():
        wgu_bf[...] = wgu_ref[0].astype(BF16)
        wd_bf[...] = wd_ref[0].astype(BF16)
        last_ref[0] = e

    @pl.when(i < n_active)
    def _():
        x = jnp.concatenate(
            [xs_ref[pl.ds(s, tm, stride=SUBLANES), :] for s in range(D_MODEL // LANES)], axis=-1)
        gu = jnp.dot(x.astype(BF16), wgu_bf[...], preferred_element_type=F32) + bgu_ref[0]
        x_glu = jnp.minimum(gu[:, :D_FF], SWIGLU_LIMIT)
        x_lin = jnp.clip(gu[:, D_FF:], -SWIGLU_LIMIT, SWIGLU_LIMIT)
        act = x_glu * jax.nn.sigmoid(SWIGLU_ALPHA * x_glu) * (x_lin + 1.0)
        out = jnp.dot(act.astype(BF16), wd_bf[...], preferred_element_type=F32) + bd_ref[0]
        for s in range(D_MODEL // LANES):
            out_ref[pl.ds(s, tm, stride=SUBLANES), :] = out[:, s * LANES:(s + 1) * LANES]


def _expert_ffn(texp, xs_tiles, w_gu, b_gu, w_down, b_down):
    n_tiles = texp.shape[0] - 1
    tm = MOE_TILE
    grid_spec = pltpu.PrefetchScalarGridSpec(
        num_scalar_prefetch=1,
        grid=(n_tiles,),
        in_specs=[
            pl.BlockSpec((tm * SUBLANES, LANES), lambda i, te: (jnp.minimum(i, te[n_tiles] - 1), 0)),
            pl.BlockSpec((1, D_MODEL, 2 * D_FF), lambda i, te: (te[i], 0, 0)),
            pl.BlockSpec((1, 1, 2 * D_FF), lambda i, te: (te[i], 0, 0)),
            pl.BlockSpec((1, D_FF, D_MODEL), lambda i, te: (te[i], 0, 0)),
            pl.BlockSpec((1, 1, D_MODEL), lambda i, te: (te[i], 0, 0)),
        ],
        out_specs=pl.BlockSpec((tm * SUBLANES, LANES), lambda i, te: (i, 0)),
        scratch_shapes=[
            pltpu.VMEM((D_MODEL, 2 * D_FF), BF16),
            pltpu.VMEM((D_FF, D_MODEL), BF16),
            pltpu.SMEM((1,), jnp.int32),
        ],
    )
    return pl.pallas_call(
        _ffn_kernel,
        grid_spec=grid_spec,
        out_shape=jax.ShapeDtypeStruct((n_tiles * tm * SUBLANES, LANES), F32),
        compiler_params=_params(1),
        name="moe_ffn",
    )(texp, xs_tiles, w_gu, b_gu.reshape(N_EXPERTS, 1, 2 * D_FF), w_down,
      b_down.reshape(N_EXPERTS, 1, D_MODEL))


def _combine_kernel(off_ref, start_ref, cnt_ref, toff_ref, route_ref, x1_ref, ys_ref, out_ref,
                    buf_ref, sems):
    i = pl.program_id(0)
    n_steps = pl.num_programs(0)
    tm = x1_ref.shape[0]

    def fetch(tile, buf):
        def per_run(e, c):
            r = tile * N_EXPERTS + e
            _copy_run(ys_ref, buf_ref.at[buf], off_ref[e] + start_ref[r], toff_ref[r], cnt_ref[r],
                      sems.at[buf])
            return c

        lax.fori_loop(0, N_EXPERTS, per_run, 0)

    @pl.when(i == 0)
    def _():
        fetch(0, 0)

    @pl.when(i + 1 < n_steps)
    def _():
        fetch(i + 1, (i + 1) % 2)

    buf = i % 2
    _wait_tile(ys_ref, buf_ref.at[buf], sems.at[buf])

    rows = jnp.concatenate(
        [buf_ref[buf, pl.ds(s, SLOTS_PER_TILE, stride=SUBLANES), :] for s in range(D_MODEL // LANES)],
        axis=-1).astype(BF16)
    route = route_ref[...]
    slot_id = lax.broadcasted_iota(jnp.int32, (tm, SLOTS_PER_TILE), 1).astype(F32)
    weights = jnp.zeros((tm, SLOTS_PER_TILE), F32)
    for k in range(TOP_K):
        slot = route[:, ROUTE_SLOTS + k:ROUTE_SLOTS + k + 1]
        gate = route[:, ROUTE_GATES + k:ROUTE_GATES + k + 1]
        weights = jnp.where(slot_id == slot, gate, weights)
    out_ref[...] = x1_ref[...] + jnp.dot(weights.astype(BF16), rows, preferred_element_type=F32)


def _combine(off, runs, route, x1, ys_tiles):
    t = x1.shape[0]
    tm = ROW_TILE
    grid_spec = pltpu.PrefetchScalarGridSpec(
        num_scalar_prefetch=4,
        grid=(t // tm,),
        in_specs=[
            pl.BlockSpec((tm, LANES), lambda i, *_: (i, 0)),
            pl.BlockSpec((tm, D_MODEL), lambda i, *_: (i, 0)),
            pl.BlockSpec(memory_space=pl.ANY),
        ],
        out_specs=pl.BlockSpec((tm, D_MODEL), lambda i, *_: (i, 0)),
        scratch_shapes=[
            pltpu.VMEM((2, SLOTS_PER_TILE * SUBLANES, LANES), F32),
            pltpu.SemaphoreType.DMA((2,)),
        ],
    )
    return pl.pallas_call(
        _combine_kernel,
        grid_spec=grid_spec,
        out_shape=jax.ShapeDtypeStruct((t, D_MODEL), F32),
        compiler_params=_params(1),
        name="moe_combine",
    )(off, *runs, route, x1, ys_tiles)


def _layer(x, g_mix, w_in, q_gain, k_gain, rel_bias, ssm_a_re, ssm_a_im, ssm_log_dt, ssm_b_re,
           ssm_b_im, ssm_c_re, ssm_c_im, ssm_d, w_glu, b_glu, w_attn_proj, w_ssm_proj, w_out,
           g_ffn, w_router, b_router, w_gate_up, b_gate_up, w_down, b_down):
    b, s, _ = x.shape
    x2d = x.reshape(b * s, D_MODEL)

    w_in_b = w_in.astype(BF16)
    *qkvs, rest = _inproj(x, g_mix, jnp.concatenate([w_in_b[:, a:b] for a, b in _inproj_columns()], axis=1))

    head_of = np.arange(GROUP_WIDTH) // HEAD_DIM
    seg = jnp.asarray(head_of[:, None] == head_of[None, :], BF16)
    os_, lses = [], []
    for gi, (window, dil) in enumerate(ATTN_GROUPS):
        table = rel_bias[:, gi * HEADS_PER_GROUP:(gi + 1) * HEADS_PER_GROUP]
        o, lse = _attention_group(qkvs[gi], gi, _attn_bias(table, window, dil), q_gain[gi], k_gain[gi], seg)
        os_.append(o)
        lses.append(lse)

    tab, bmat, cmat = _ssm_tables(ssm_a_re, ssm_a_im, ssm_log_dt, ssm_b_re, ssm_b_im, ssm_c_re, ssm_c_im)
    ys = _ssm(rest, tab, bmat, cmat, ssm_d)

    expand = jnp.asarray(np.arange(LANES)[:, None] == head_of[None, :], BF16)
    pad_e = LANES - N_EXPERTS
    weights = dict(
        expand=expand, wa=w_attn_proj.astype(BF16), wglu=w_glu.astype(BF16),
        bglu=b_glu.reshape(1, SSM_WIDTH), ws=w_ssm_proj.astype(BF16), wo=w_out.astype(BF16),
        gffn=g_ffn.reshape(1, D_MODEL), wr=jnp.pad(w_router, ((0, 0), (0, pad_e))).astype(BF16),
        br=jnp.pad(b_router, (0, pad_e)).reshape(1, LANES))
    x1, h2, route, counts, run_tab = _mix(os_, lses, ys, rest, x2d, weights)

    counts_i = counts[0, :N_EXPERTS].astype(jnp.int32)
    run_table = run_tab[:, :, :N_EXPERTS].astype(jnp.int32)
    runs = [run_table[:, row].reshape(-1) for row in (RUN_START, RUN_COUNT, RUN_OFFSET)]
    xs_tiles, texp, off = _dispatch(counts_i, runs, h2, route)
    ys_tiles = _expert_ffn(texp, xs_tiles, w_gate_up, b_gate_up, w_down, b_down)
    out = _combine(off, runs, route, x1, ys_tiles)
    return out.reshape(b, s, D_MODEL)


_layer_jit = jax.jit(_layer)


def kernel(x, g_mix, w_in, q_gain, k_gain, rel_bias, ssm_a_re, ssm_a_im, ssm_log_dt, ssm_b_re, ssm_b_im, ssm_c_re, ssm_c_im, ssm_d, w_glu, b_glu, w_attn_proj, w_ssm_proj, w_out, g_ffn, w_router, b_router, w_gate_up, b_gate_up, w_down, b_down):
    return _layer_jit(x, g_mix[0], w_in[0], q_gain[0], k_gain[0], rel_bias, ssm_a_re[0], ssm_a_im[0],
                      ssm_log_dt[0], ssm_b_re[0], ssm_b_im[0], ssm_c_re[0], ssm_c_im[0], ssm_d[0],
                      w_glu[0], b_glu[0], w_attn_proj[0], w_ssm_proj[0], w_out[0], g_ffn[0],
                      w_router[0], b_router[0], w_gate_up[0], b_gate_up[0], w_down[0], b_down[0])
```

```python
import functools
import math

import jax
import jax.numpy as jnp
import numpy as np
from jax import lax
from jax.experimental import pallas as pl
from jax.experimental.pallas import tpu as pltpu

F32 = jnp.float32
BF16 = jnp.bfloat16

D_MODEL = 1024
HEAD_DIM = 64
ATTN_GROUPS = ((128, 1), (512, 4), (2048, 16))
HEADS_PER_GROUP = 8
GROUP_WIDTH = HEADS_PER_GROUP * HEAD_DIM
N_ATTN_HEADS = len(ATTN_GROUPS) * HEADS_PER_GROUP
QKV_WIDTH = 3 * N_ATTN_HEADS * HEAD_DIM
ATTN_BLOCK = 128
ATTN_STEP_BLOCKS = 4
REL_BUCKETS = 32
REL_MAX_DIST = 2048
SSM_WIDTH = 512
SSM_GROUP = 16
SSM_GROUPS = 32
SSM_STATE = 64
IN_WIDTH = QKV_WIDTH + SSM_WIDTH + 2 * D_MODEL
N_EXPERTS = 32
TOP_K = 4
D_FF = 1024
SWIGLU_LIMIT = 7.0
SWIGLU_ALPHA = 1.702
EPS = 1e-6

SUBLANES = 8
LANES = 128
MASK_VALUE = -1e30
VMEM_LIMIT = 56 * 1024 * 1024

QKV_GROUP_WIDTH = 3 * GROUP_WIDTH
REST_WIDTH = 2 * D_MODEL + SSM_WIDTH
GATE_A_BLOCK = 0
GATE_B_BLOCK = 1
U_BLOCK = 2 * D_MODEL // SSM_WIDTH

ROW_TILE = 256
MOE_TILE = 512


def _params(n_axes, vmem=VMEM_LIMIT):
    return pltpu.CompilerParams(dimension_semantics=("arbitrary",) * n_axes, vmem_limit_bytes=vmem)


def _inproj_columns():
    n_g = len(ATTN_GROUPS)
    spans = [((part * n_g + gi) * GROUP_WIDTH, (part * n_g + gi + 1) * GROUP_WIDTH)
             for gi in range(n_g) for part in range(3)]
    return spans + [(QKV_WIDTH + SSM_WIDTH, IN_WIDTH), (QKV_WIDTH, QKV_WIDTH + SSM_WIDTH)]


def _inproj_kernel(x_ref, g_ref, w_ref, q0_ref, q1_ref, q2_ref, rest_ref, stage_ref):
    x = x_ref[0]
    tm = x.shape[0]
    ms = jnp.mean(x * x, axis=-1, keepdims=True)
    h = x * lax.rsqrt(ms + EPS) * g_ref[...]
    res = jnp.dot(h.astype(BF16), w_ref[...], preferred_element_type=F32)
    w = QKV_GROUP_WIDTH
    q0_ref[0, 0] = res[:, 0:w]
    rest_ref[0] = res[:, 3 * w:]
    n_blk = w // LANES
    for c in range(2 * n_blk):
        stage_ref[c] = res[:, w + c * LANES:w + (c + 1) * LANES]
    for out_ref, gi in ((q1_ref, 1), (q2_ref, 2)):
        dil = ATTN_GROUPS[gi][1]
        for r in range(dil):
            for c in range(n_blk):
                out_ref[0, r, :, c * LANES:(c + 1) * LANES] = stage_ref[
                    (gi - 1) * n_blk + c, pl.ds(r, tm // dil, stride=dil), :]


def _inproj(x, g, w_bf16):
    b, s, _ = x.shape
    tm = ROW_TILE
    dils = [d for _, d in ATTN_GROUPS]
    out_specs = [pl.BlockSpec((1, d, tm // d, QKV_GROUP_WIDTH), lambda bi, i: (bi, 0, i, 0)) for d in dils]
    out_shape = [jax.ShapeDtypeStruct((b, d, s // d, QKV_GROUP_WIDTH), F32) for d in dils]
    return pl.pallas_call(
        _inproj_kernel,
        grid=(b, s // tm),
        in_specs=[
            pl.BlockSpec((1, tm, D_MODEL), lambda bi, i: (bi, i, 0)),
            pl.BlockSpec((1, D_MODEL), lambda bi, i: (0, 0)),
            pl.BlockSpec((D_MODEL, IN_WIDTH), lambda bi, i: (0, 0), pipeline_mode=pl.Buffered(1)),
        ],
        out_specs=out_specs + [pl.BlockSpec((1, tm, REST_WIDTH), lambda bi, i: (bi, i, 0))],
        out_shape=out_shape + [jax.ShapeDtypeStruct((b, s, REST_WIDTH), F32)],
        scratch_shapes=[pltpu.VMEM((2 * QKV_GROUP_WIDTH // LANES, tm, LANES), F32)],
        compiler_params=_params(2),
        name="inproj",
    )(x, g.reshape(1, D_MODEL), w_bf16)


def _rel_bucket_np(dist):
    max_exact = REL_BUCKETS // 2
    d = np.maximum(dist, 1).astype(np.float32)
    large = max_exact + (np.log(d / max_exact) / math.log(REL_MAX_DIST / max_exact)
                         * (REL_BUCKETS - max_exact)).astype(np.int32)
    large = np.minimum(large, REL_BUCKETS - 1)
    return np.where(dist < max_exact, dist, large).astype(np.int32)


def _attn_bias(table, window, dilation):
    blk = ATTN_BLOCK
    span = window // dilation
    qi = np.arange(blk)[:, None]
    ki = np.arange(2 * blk)[None, :]
    steps = qi + blk - ki
    band = (steps >= 0) & (steps <= span)
    bucket = _rel_bucket_np(np.clip(steps, 0, None) * dilation)
    onehot = np.eye(REL_BUCKETS, dtype=np.float32)[bucket]
    bias = jnp.einsum('qkb,bh->hqk', onehot, table.astype(F32), precision=lax.Precision.HIGHEST)
    b0 = jnp.where(band[None], bias, MASK_VALUE)
    b1 = jnp.where((band & (ki >= blk))[None], bias, MASK_VALUE)
    return jnp.stack([b0, b1], axis=0)


def _attn_kernel(q_ref, k_ref, v_ref, bias_ref, qg_ref, kg_ref, seg_ref, o_ref, lse_ref,
                 qbuf_ref, kbuf_ref, vbuf_ref):
    blk = ATTN_BLOCK
    rows = q_ref.shape[2]
    i = pl.program_id(2)
    seg = seg_ref[...]

    def head_norm(t, gain):
        ss = jnp.dot((t * t).astype(BF16), seg, preferred_element_type=F32)
        return t * lax.rsqrt(ss * (1.0 / HEAD_DIM) + EPS) * gain

    qbuf_ref[...] = (head_norm(q_ref[0, 0], qg_ref[...]) * (HEAD_DIM ** -0.5)).astype(BF16)

    @pl.when(i == 0)
    def _():
        kbuf_ref[0:blk, :] = jnp.zeros((blk, GROUP_WIDTH), BF16)
        vbuf_ref[0:blk, :] = jnp.zeros((blk, GROUP_WIDTH), BF16)

    @pl.when(i > 0)
    def _():
        kbuf_ref[0:blk, :] = kbuf_ref[rows:rows + blk, :]
        vbuf_ref[0:blk, :] = vbuf_ref[rows:rows + blk, :]

    kbuf_ref[blk:blk + rows, :] = head_norm(k_ref[0, 0], kg_ref[...]).astype(BF16)
    vbuf_ref[blk:blk + rows, :] = v_ref[0, 0].astype(BF16)
    first = (i == 0).astype(jnp.int32)

    lane = lax.broadcasted_iota(jnp.int32, (blk, LANES), 1)
    low_half = lane < HEAD_DIM
    for j in range(rows // blk):
        variant = first if j == 0 else 0
        lse_tile = jnp.zeros((blk, LANES), F32)
        outs = []
        for p in range(GROUP_WIDTH // LANES):
            qp = qbuf_ref[j * blk:(j + 1) * blk, p * LANES:(p + 1) * LANES]
            kp = kbuf_ref[j * blk:(j + 2) * blk, p * LANES:(p + 1) * LANES]
            vp = vbuf_ref[j * blk:(j + 2) * blk, p * LANES:(p + 1) * LANES]
            pair = []
            for hh in range(2):
                head = 2 * p + hh
                sel = low_half if hh == 0 else jnp.logical_not(low_half)
                qm = jnp.where(sel, qp, jnp.zeros_like(qp))
                s = lax.dot_general(qm, kp, (((1,), (1,)), ((), ())), preferred_element_type=F32)
                s = s + bias_ref[variant, head]
                mx = jnp.max(s, axis=-1, keepdims=True)
                pe = jnp.exp(s - mx)
                den = jnp.sum(pe, axis=-1, keepdims=True)
                o = jnp.dot(pe.astype(BF16), vp, preferred_element_type=F32)
                pair.append(o * (1.0 / den))
                lse_tile = jnp.where(lane == head, mx + jnp.log(den), lse_tile)
            outs.append(jnp.where(low_half, pair[0], pair[1]))
        o_ref[0, 0, j * blk:(j + 1) * blk, :] = jnp.concatenate(outs, axis=-1)
        lse_ref[0, 0, j * blk:(j + 1) * blk, :] = lse_tile


def _attention_group(qkv, gi, bias, q_gain, k_gain, seg):
    b, dil, length, _ = qkv.shape
    blk = ATTN_BLOCK
    rows = ATTN_STEP_BLOCKS * blk

    def col(c):
        return pl.BlockSpec((1, 1, rows, GROUP_WIDTH), lambda bi, r, i: (bi, r, i, c))

    const = lambda shape: pl.BlockSpec(shape, lambda bi, r, i: (0,) * len(shape))
    return pl.pallas_call(
        _attn_kernel,
        grid=(b, dil, length // rows),
        in_specs=[
            col(0), col(1), col(2),
            const((2, HEADS_PER_GROUP, blk, 2 * blk)),
            const((1, GROUP_WIDTH)),
            const((1, GROUP_WIDTH)),
            const((GROUP_WIDTH, GROUP_WIDTH)),
        ],
        out_specs=[
            pl.BlockSpec((1, 1, rows, GROUP_WIDTH), lambda bi, r, i: (bi, r, i, 0)),
            pl.BlockSpec((1, 1, rows, LANES), lambda bi, r, i: (bi, r, i, 0)),
        ],
        out_shape=[
            jax.ShapeDtypeStruct((b, dil, length, GROUP_WIDTH), F32),
            jax.ShapeDtypeStruct((b, dil, length, LANES), F32),
        ],
        scratch_shapes=[
            pltpu.VMEM((rows, GROUP_WIDTH), BF16),
            pltpu.VMEM((blk + rows, GROUP_WIDTH), BF16),
            pltpu.VMEM((blk + rows, GROUP_WIDTH), BF16),
        ],
        compiler_params=_params(3),
        name=f"attn_g{gi}",
    )(qkv, qkv, qkv, bias,
      jnp.tile(q_gain, HEADS_PER_GROUP).reshape(1, GROUP_WIDTH),
      jnp.tile(k_gain, HEADS_PER_GROUP).reshape(1, GROUP_WIDTH), seg)


HALF_STATE = SSM_GROUPS * SSM_STATE // 2
SCAN_COLS = 4
SSM_STEP = 64
TAB_STEP, TAB_CHUNK = 0, 1


def _ssm_tables(a_re, a_im, log_dt, b_re, b_im, c_re, c_im, chunk_len):
    lam = lax.complex(a_re.astype(F32), a_im.astype(F32))
    dt = jnp.exp(log_dt.astype(F32))[:, None]
    a_bar = jnp.exp(lam * dt)
    b_bar = ((a_bar - 1.0) / lam)[..., None] * lax.complex(b_re.astype(F32), b_im.astype(F32))

    tab = jnp.stack([a_bar.reshape(-1), jnp.exp(lam * dt * chunk_len).reshape(-1)])
    tab = jnp.stack([tab.real, tab.imag], axis=1).astype(F32)
    tab = jnp.broadcast_to(tab[:, :, None, :], (2, 2, SUBLANES, tab.shape[-1]))

    gh = SSM_GROUPS // 2
    eye = jnp.eye(gh, dtype=F32)

    def in_mat(m):
        return jnp.einsum('gnc,gh->gchn', m, eye).reshape(gh * SSM_GROUP, gh * SSM_STATE)

    def out_mat(m):
        return jnp.einsum('gcn,gh->gnhc', m, eye).reshape(gh * SSM_STATE, gh * SSM_GROUP)

    b_mats, c_mats = [], []
    for h in range(2):
        sl = slice(h * gh, (h + 1) * gh)
        b_mats.append(jnp.concatenate([in_mat(b_bar.real[sl]), in_mat(b_bar.imag[sl])], axis=1))
        c_mats.append(jnp.concatenate([out_mat(c_re.astype(F32)[sl]),
                                       -out_mat(c_im.astype(F32)[sl])], axis=0))
    return tab, jnp.stack(b_mats).astype(BF16), jnp.stack(c_mats).astype(BF16)


def _ssm_kernel(u_ref, tab_ref, bmat_ref, cmat_ref, d_ref, y_ref, xs_ref, state_ref, stage_ref,
                *, chunks):
    sweep = pl.program_id(0)
    i = pl.program_id(1)
    n_rows, tj = u_ref.shape[0], u_ref.shape[1]
    half_w = 2 * HALF_STATE
    n_lane_blk = SSM_WIDTH // LANES
    cols = []
    for c in range(2 * HALF_STATE // LANES):
        h, k = divmod(c, HALF_STATE // LANES)
        cols.append((c * LANES, h * half_w + k * LANES, h * half_w + HALF_STATE + k * LANES))

    def table(which, part, tc):
        return tab_ref[which, part, :, tc:tc + LANES]

    @pl.when(jnp.logical_and(sweep == 0, i == 0))
    def _():
        state_ref[...] = jnp.zeros_like(state_ref)

    @pl.when(jnp.logical_and(sweep == 1, i == 0))
    def _():
        row = lax.broadcasted_iota(jnp.int32, (SUBLANES, LANES), 0)
        seq_start = (row % chunks) == 0
        for tc, rc, ic in cols:
            er, ei = state_ref[:, rc:rc + LANES], state_ref[:, ic:ic + LANES]
            pr, pi = table(TAB_CHUNK, 0, tc), table(TAB_CHUNK, 1, tc)
            xr = jnp.zeros_like(er)
            xi = jnp.zeros_like(ei)
            for _ in range(chunks - 1):
                nr = pr * xr - pi * xi + er
                ni = pr * xi + pi * xr + ei
                xr = jnp.where(seq_start, 0.0, pltpu.roll(nr, 1, 0))
                xi = jnp.where(seq_start, 0.0, pltpu.roll(ni, 1, 0))
            state_ref[:, rc:rc + LANES] = xr
            state_ref[:, ic:ic + LANES] = xi

    for r in range(n_rows):
        for c in range(n_lane_blk):
            stage_ref[c, pl.ds(r, tj, stride=SUBLANES), :] = u_ref[r, :, c * LANES:(c + 1) * LANES]
    u = jnp.concatenate([stage_ref[c] for c in range(n_lane_blk)], axis=-1)
    ub = u.astype(BF16)
    hw = SSM_WIDTH // 2
    for h in range(2):
        xs_ref[:, h * half_w:(h + 1) * half_w] = jnp.dot(
            ub[:, h * hw:(h + 1) * hw], bmat_ref[h], preferred_element_type=F32)

    for c0 in range(0, len(cols), SCAN_COLS):
        grp = cols[c0:c0 + SCAN_COLS]
        consts = [(table(TAB_STEP, 0, tc), table(TAB_STEP, 1, tc)) for tc, _, _ in grp]

        def body(j, st, grp=grp, consts=consts):
            at_j = pl.ds(pl.multiple_of(j * SUBLANES, SUBLANES), SUBLANES)
            new = []
            for (_, rc, ic), (ar, ai), (xr, xi) in zip(grp, consts, st):
                nr = ar * xr - ai * xi + xs_ref[at_j, rc:rc + LANES]
                ni = ar * xi + ai * xr + xs_ref[at_j, ic:ic + LANES]
                xs_ref[at_j, rc:rc + LANES] = nr
                xs_ref[at_j, ic:ic + LANES] = ni
                new.append((nr, ni))
            return tuple(new)

        init = tuple((state_ref[:, rc:rc + LANES], state_ref[:, ic:ic + LANES]) for _, rc, ic in grp)
        fin = lax.fori_loop(0, tj, body, init)
        for (_, rc, ic), (xr, xi) in zip(grp, fin):
            state_ref[:, rc:rc + LANES] = xr
            state_ref[:, ic:ic + LANES] = xi

    @pl.when(sweep == 1)
    def _():
        ys = []
        for h in range(2):
            xh = xs_ref[:, h * half_w:(h + 1) * half_w].astype(BF16)
            ys.append(jnp.dot(xh, cmat_ref[h], preferred_element_type=F32))
        y = jnp.concatenate(ys, axis=-1) + d_ref[...] * u
        for c in range(n_lane_blk):
            stage_ref[c] = y[:, c * LANES:(c + 1) * LANES]
        for r in range(n_rows):
            y_ref[r] = jnp.concatenate(
                [stage_ref[c, pl.ds(r, tj, stride=SUBLANES), :] for c in range(n_lane_blk)], axis=-1)


def _ssm(rest, a_re, a_im, log_dt, b_re, b_im, c_re, c_im, d_skip):
    b, s, _ = rest.shape
    chunks = SUBLANES // b
    chunk_len = s // chunks
    tj = min(SSM_STEP, chunk_len)
    tab, bmat, cmat = _ssm_tables(a_re, a_im, log_dt, b_re, b_im, c_re, c_im, chunk_len)
    return pl.pallas_call(
        functools.partial(_ssm_kernel, chunks=chunks),
        grid=(2, chunk_len // tj),
        in_specs=[
            pl.BlockSpec((SUBLANES, tj, SSM_WIDTH), lambda sw, j: (0, j, U_BLOCK)),
            pl.BlockSpec(tab.shape, lambda sw, j: (0, 0, 0, 0)),
            pl.BlockSpec(bmat.shape, lambda sw, j: (0, 0, 0)),
            pl.BlockSpec(cmat.shape, lambda sw, j: (0, 0, 0)),
            pl.BlockSpec((1, SSM_WIDTH), lambda sw, j: (0, 0)),
        ],
        out_specs=pl.BlockSpec((SUBLANES, tj, SSM_WIDTH), lambda sw, j: (0, j * sw, 0)),
        out_shape=jax.ShapeDtypeStruct((SUBLANES, chunk_len, SSM_WIDTH), F32),
        scratch_shapes=[
            pltpu.VMEM((SUBLANES * tj, 4 * HALF_STATE), F32),
            pltpu.VMEM((SUBLANES, 4 * HALF_STATE), F32),
            pltpu.VMEM((SSM_WIDTH // LANES, SUBLANES * tj, LANES), F32),
        ],
        compiler_params=_params(2),
        name="ssm",
    )(rest.reshape(SUBLANES, chunk_len, REST_WIDTH), tab, bmat, cmat,
      d_skip.reshape(1, SSM_WIDTH)).reshape(b * s, SSM_WIDTH)


ROUTE_IDS, ROUTE_GATES, ROUTE_SLOTS = 0, TOP_K, 2 * TOP_K
RUN_START, RUN_COUNT, RUN_OFFSET = 0, 1, 2
SLOTS_PER_TILE = ROW_TILE * TOP_K


def _mix_kernel(o0_ref, o1_ref, o2_ref, l0_ref, l1_ref, l2_ref, ys_ref, ga_ref, gb_ref, x_ref,
                expand_ref, wa_ref, wglu_ref, bglu_ref, ws_ref, wo_ref, gffn_ref, wr_ref, br_ref,
                x1_ref, h2_ref, route_ref, counts_ref, runs_ref, run_ref, ostage_ref, lstage_ref):
    tm = x_ref.shape[0]

    @pl.when(jnp.logical_and(pl.program_id(0) == 0, pl.program_id(1) == 0))
    def _():
        run_ref[...] = jnp.zeros_like(run_ref)

    for slot, (o_ref, l_ref) in enumerate(((o1_ref, l1_ref), (o2_ref, l2_ref))):
        dil = ATTN_GROUPS[slot + 1][1]
        for r in range(dil):
            for c in range(GROUP_WIDTH // LANES):
                ostage_ref[slot, c, pl.ds(r, tm // dil, stride=dil), :] = o_ref[
                    0, r, :, c * LANES:(c + 1) * LANES]
            lstage_ref[slot, pl.ds(r, tm // dil, stride=dil), :] = l_ref[0, r]

    def sigmoid(z):
        return 0.5 * jnp.tanh(0.5 * z) + 0.5

    expand = expand_ref[...]
    n_lane_blk = GROUP_WIDTH // LANES

    def mix_rows(lo, n):
        rows = slice(lo, lo + n)
        group_out = [o0_ref[0, 0, rows, :]] + [
            jnp.concatenate([ostage_ref[slot, c, rows, :] for c in range(n_lane_blk)], axis=-1)
            for slot in range(2)]
        lses = [l0_ref[0, 0, rows, :], lstage_ref[0, rows, :], lstage_ref[1, rows, :]]
        mx = jnp.maximum(jnp.maximum(lses[0], lses[1]), lses[2])
        es = [jnp.exp(l - mx) for l in lses]
        inv = 1.0 / (es[0] + es[1] + es[2])
        attn = jnp.zeros((n, GROUP_WIDTH), F32)
        for e, o_g in zip(es, group_out):
            attn = attn + jnp.dot((e * inv).astype(BF16), expand, preferred_element_type=F32) * o_g
        y_a = jnp.dot(attn.astype(BF16), wa_ref[...], preferred_element_type=F32)

        ys = ys_ref[rows, :]
        ys = 0.5 * ys * (1.0 + jnp.tanh(math.sqrt(2.0 / math.pi) * (ys + 0.044715 * (ys * ys * ys))))
        glu = jnp.dot(ys.astype(BF16), wglu_ref[...], preferred_element_type=F32) + bglu_ref[...]
        ys = ys * sigmoid(glu)
        y_b = jnp.dot(ys.astype(BF16), ws_ref[...], preferred_element_type=F32)

        mixed = sigmoid(ga_ref[rows, :]) * y_a + sigmoid(gb_ref[rows, :]) * y_b
        x1 = x_ref[rows, :] + jnp.dot(mixed.astype(BF16), wo_ref[...], preferred_element_type=F32)
        x1_ref[rows, :] = x1

        ms = jnp.mean(x1 * x1, axis=-1, keepdims=True)
        h2b = (x1 * lax.rsqrt(ms + EPS) * gffn_ref[...]).astype(BF16)
        h2_ref[rows, :] = h2b
        return jnp.dot(h2b, wr_ref[...], preferred_element_type=F32) + br_ref[...]

    logits = mix_rows(0, tm)

    lane = lax.broadcasted_iota(jnp.int32, (tm, LANES), 1)
    lane_f = lane.astype(F32)
    work = jnp.where(lane < N_EXPERTS, logits, -jnp.inf)
    sel_mask = jnp.zeros((tm, LANES), F32)
    route = jnp.zeros((tm, LANES), F32)
    vals, hots = [], []
    for k in range(TOP_K):
        v = jnp.max(work, axis=-1, keepdims=True)
        idx = jnp.min(jnp.where(work == v, lane_f, float(LANES)), axis=-1, keepdims=True)
        hot = lane_f == idx
        work = jnp.where(hot, -jnp.inf, work)
        sel_mask = jnp.where(hot, 1.0, sel_mask)
        route = jnp.where(lane == ROUTE_IDS + k, idx, route)
        vals.append(v)
        hots.append(hot)
    exps = [jnp.exp(v - vals[0]) for v in vals]
    inv_den = 1.0 / (exps[0] + exps[1] + exps[2] + exps[3])

    r_i = lax.broadcasted_iota(jnp.int32, (tm, tm), 0)
    c_i = lax.broadcasted_iota(jnp.int32, (tm, tm), 1)
    tri = jnp.where(c_i < r_i, 1.0, 0.0).astype(BF16)
    local = jnp.dot(tri, sel_mask.astype(BF16), preferred_element_type=F32)
    count = jnp.sum(sel_mask, axis=0, keepdims=True)
    e_r = lax.broadcasted_iota(jnp.int32, (LANES, LANES), 0)
    e_c = lax.broadcasted_iota(jnp.int32, (LANES, LANES), 1)
    upper = jnp.where(e_r < e_c, 1.0, 0.0).astype(BF16)
    offset = jnp.dot(jnp.broadcast_to(count, (SUBLANES, LANES)).astype(BF16), upper,
                     preferred_element_type=F32)
    slot_of = local + offset[0:1, :]
    for k in range(TOP_K):
        slot = jnp.sum(jnp.where(hots[k], slot_of, 0.0), axis=-1, keepdims=True)
        route = jnp.where(lane == ROUTE_GATES + k, exps[k] * inv_den, route)
        route = jnp.where(lane == ROUTE_SLOTS + k, slot, route)
    route_ref[...] = route
    row8 = lax.broadcasted_iota(jnp.int32, (SUBLANES, LANES), 0)
    runs = jnp.where(row8 == RUN_START, run_ref[...],
                     jnp.where(row8 == RUN_COUNT, jnp.broadcast_to(count, (SUBLANES, LANES)),
                               jnp.where(row8 == RUN_OFFSET, offset, 0.0)))
    runs_ref[0] = runs
    total = run_ref[0:1, :] + count
    run_ref[...] = jnp.broadcast_to(total, run_ref.shape)
    counts_ref[...] = jnp.broadcast_to(total, counts_ref.shape)


def _mix(os_, lses, ys, rest, x2d, w):
    b, s, _ = rest.shape
    t = b * s
    tm = ROW_TILE
    n_i = s // tm
    rest2d = rest.reshape(t, REST_WIDTH)
    row = lambda width, blk=0: pl.BlockSpec((tm, width), lambda bi, i: (bi * n_i + i, blk))
    const = lambda a: pl.BlockSpec(a.shape, lambda bi, i: (0,) * a.ndim)
    grouped = lambda width: [pl.BlockSpec((1, d, tm // d, width), lambda bi, i: (bi, 0, i, 0))
                             for _, d in ATTN_GROUPS]
    consts = [w["expand"], w["wa"], w["wglu"], w["bglu"], w["ws"], w["wo"], w["gffn"], w["wr"], w["br"]]
    n_dilated = len(ATTN_GROUPS) - 1
    return pl.pallas_call(
        _mix_kernel,
        grid=(b, n_i),
        in_specs=grouped(GROUP_WIDTH) + grouped(LANES) + [
            row(SSM_WIDTH),
            row(D_MODEL, GATE_A_BLOCK),
            row(D_MODEL, GATE_B_BLOCK),
            row(D_MODEL),
        ] + [const(a) for a in consts],
        out_specs=[
            row(D_MODEL),
            row(D_MODEL),
            row(LANES),
            pl.BlockSpec((SUBLANES, LANES), lambda bi, i: (0, 0)),
            pl.BlockSpec((1, SUBLANES, LANES), lambda bi, i: (bi * n_i + i, 0, 0)),
        ],
        out_shape=[
            jax.ShapeDtypeStruct((t, D_MODEL), F32),
            jax.ShapeDtypeStruct((t, D_MODEL), BF16),
            jax.ShapeDtypeStruct((t, LANES), F32),
            jax.ShapeDtypeStruct((SUBLANES, LANES), F32),
            jax.ShapeDtypeStruct((t // tm, SUBLANES, LANES), F32),
        ],
        scratch_shapes=[
            pltpu.VMEM((SUBLANES, LANES), F32),
            pltpu.VMEM((n_dilated, GROUP_WIDTH // LANES, tm, LANES), F32),
            pltpu.VMEM((n_dilated, tm, LANES), F32),
        ],
        compiler_params=_params(2),
        name="mix_router",
    )(*os_, *lses, ys, rest2d, rest2d, x2d, *consts)


def _n_moe_tiles(t):
    return t * TOP_K // MOE_TILE + N_EXPERTS


def _copy_run(src_ref, dst_ref, src_row, dst_row, n_rows, sem):
    size = ROW_TILE
    while size >= 1:
        @pl.when((n_rows & size) != 0)
        def _(size=size, src_row=src_row, dst_row=dst_row):
            pltpu.make_async_copy(
                src_ref.at[pl.ds(pl.multiple_of(src_row * SUBLANES, SUBLANES), size * SUBLANES), :],
                dst_ref.at[pl.ds(pl.multiple_of(dst_row * SUBLANES, SUBLANES), size * SUBLANES), :],
                sem).start()
        step = n_rows & size
        src_row = src_row + step
        dst_row = dst_row + step
        size //= 2


def _wait_tile(hbm_ref, vmem_ref, sem):
    pltpu.make_async_copy(hbm_ref.at[pl.ds(0, SLOTS_PER_TILE * SUBLANES), :], vmem_ref, sem).wait()


def _dispatch_kernel(counts_ref, start_ref, cnt_ref, toff_ref, h2_ref, route_ref,
                     xs_ref, texp_ref, off_ref, stage_ref, sems):
    i = pl.program_id(0)
    n_steps = pl.num_programs(0)
    n_tiles = texp_ref.shape[0] - 1

    @pl.when(i == 0)
    def _():
        def per_expert(e, start):
            off_ref[e] = start
            n_t = (counts_ref[e] + (MOE_TILE - 1)) // MOE_TILE

            def mark(ti, c):
                texp_ref[start // MOE_TILE + ti] = e
                return c

            lax.fori_loop(0, n_t, mark, 0)
            return start + n_t * MOE_TILE

        end = lax.fori_loop(0, N_EXPERTS, per_expert, 0)
        n_active = end // MOE_TILE
        texp_ref[n_tiles] = n_active

        def fill(ti, c):
            texp_ref[ti] = texp_ref[n_active - 1]
            return c

        lax.fori_loop(n_active, n_tiles, fill, 0)

    slots_t = route_ref[...].T
    slot_id = lax.broadcasted_iota(jnp.int32, (SLOTS_PER_TILE, ROW_TILE), 0).astype(F32)
    perm = jnp.zeros((SLOTS_PER_TILE, ROW_TILE), F32)
    for k in range(TOP_K):
        perm = jnp.where(slot_id == slots_t[ROUTE_SLOTS + k:ROUTE_SLOTS + k + 1, :], 1.0, perm)
    srt = jnp.dot(perm.astype(BF16), h2_ref[...], preferred_element_type=F32)
    buf = i % 2
    for s in range(D_MODEL // LANES):
        stage_ref[buf, pl.ds(s, SLOTS_PER_TILE, stride=SUBLANES), :] = srt[:, s * LANES:(s + 1) * LANES]

    @pl.when(i > 0)
    def _():
        _wait_tile(xs_ref, stage_ref.at[1 - buf], sems.at[1 - buf])

    def per_run(e, c):
        r = i * N_EXPERTS + e
        _copy_run(stage_ref.at[buf], xs_ref, toff_ref[r], off_ref[e] + start_ref[r], cnt_ref[r],
                  sems.at[buf])
        return c

    lax.fori_loop(0, N_EXPERTS, per_run, 0)

    @pl.when(i == n_steps - 1)
    def _():
        _wait_tile(xs_ref, stage_ref.at[buf], sems.at[buf])


def _dispatch(counts, runs, h2, route):
    t = h2.shape[0]
    tm = ROW_TILE
    n_tiles = _n_moe_tiles(t)
    grid_spec = pltpu.PrefetchScalarGridSpec(
        num_scalar_prefetch=4,
        grid=(t // tm,),
        in_specs=[
            pl.BlockSpec((tm, D_MODEL), lambda i, *_: (i, 0)),
            pl.BlockSpec((tm, LANES), lambda i, *_: (i, 0)),
        ],
        out_specs=[
            pl.BlockSpec(memory_space=pl.ANY),
            pl.BlockSpec((n_tiles + 1,), lambda i, *_: (0,), memory_space=pltpu.SMEM),
            pl.BlockSpec((N_EXPERTS,), lambda i, *_: (0,), memory_space=pltpu.SMEM),
        ],
        scratch_shapes=[
            pltpu.VMEM((2, SLOTS_PER_TILE * SUBLANES, LANES), F32),
            pltpu.SemaphoreType.DMA((2,)),
        ],
    )
    return pl.pallas_call(
        _dispatch_kernel,
        grid_spec=grid_spec,
        out_shape=[
            jax.ShapeDtypeStruct((n_tiles * MOE_TILE * SUBLANES, LANES), F32),
            jax.ShapeDtypeStruct((n_tiles + 1,), jnp.int32),
            jax.ShapeDtypeStruct((N_EXPERTS,), jnp.int32),
        ],
        compiler_params=_params(1),
        name="moe_dispatch",
    )(counts, *runs, h2, route)


def _ffn_kernel(texp_ref, xs_ref, wgu_ref, bgu_ref, wd_ref, bd_ref, out_ref, wgu_bf, wd_bf, last_ref):
    i = pl.program_id(0)
    tm = MOE_TILE
    n_active = texp_ref[pl.num_programs(0)]
    e = texp_ref[i]

    @pl.when(i == 0)
    def _():
        last_ref[0] = -1

    @pl.when(jnp.logical_and(i < n_active, e != last_ref[0]))
    def _():
        wgu_bf[...] = wgu_ref[0].astype(BF16)
        wd_bf[...] = wd_ref[0].astype(BF16)
        last_ref[0] = e

    @pl.when(i < n_active)
    def _():
        x = jnp.concatenate(
            [xs_ref[pl.ds(s, tm, stride=SUBLANES), :] for s in range(D_MODEL // LANES)], axis=-1)
        gu = jnp.dot(x.astype(BF16), wgu_bf[...], preferred_element_type=F32) + bgu_ref[0]
        x_glu = jnp.minimum(gu[:, :D_FF], SWIGLU_LIMIT)
        x_lin = jnp.clip(gu[:, D_FF:], -SWIGLU_LIMIT, SWIGLU_LIMIT)
        act = x_glu * jax.nn.sigmoid(SWIGLU_ALPHA * x_glu) * (x_lin + 1.0)
        out = jnp.dot(act.astype(BF16), wd_bf[...], preferred_element_type=F32) + bd_ref[0]
        for s in range(D_MODEL // LANES):
            out_ref[pl.ds(s, tm, stride=SUBLANES), :] = out[:, s * LANES:(s + 1) * LANES]


def _expert_ffn(texp, xs_tiles, w_gu, b_gu, w_down, b_down):
    n_tiles = texp.shape[0] - 1
    tm = MOE_TILE
    grid_spec = pltpu.PrefetchScalarGridSpec(
        num_scalar_prefetch=1,
        grid=(n_tiles,),
        in_specs=[
            pl.BlockSpec((tm * SUBLANES, LANES), lambda i, te: (jnp.minimum(i, te[n_tiles] - 1), 0)),
            pl.BlockSpec((1, D_MODEL, 2 * D_FF), lambda i, te: (te[i], 0, 0)),
            pl.BlockSpec((1, 1, 2 * D_FF), lambda i, te: (te[i], 0, 0)),
            pl.BlockSpec((1, D_FF, D_MODEL), lambda i, te: (te[i], 0, 0)),
            pl.BlockSpec((1, 1, D_MODEL), lambda i, te: (te[i], 0, 0)),
        ],
        out_specs=pl.BlockSpec((tm * SUBLANES, LANES), lambda i, te: (i, 0)),
        scratch_shapes=[
            pltpu.VMEM((D_MODEL, 2 * D_FF), BF16),
            pltpu.VMEM((D_FF, D_MODEL), BF16),
            pltpu.SMEM((1,), jnp.int32),
        ],
    )
    return pl.pallas_call(
        _ffn_kernel,
        grid_spec=grid_spec,
        out_shape=jax.ShapeDtypeStruct((n_tiles * tm * SUBLANES, LANES), F32),
        compiler_params=_params(1),
        name="moe_ffn",
    )(texp, xs_tiles, w_gu, b_gu.reshape(N_EXPERTS, 1, 2 * D_FF), w_down,
      b_down.reshape(N_EXPERTS, 1, D_MODEL))


def _combine_kernel(off_ref, start_ref, cnt_ref, toff_ref, route_ref, x1_ref, ys_ref, out_ref,
                    buf_ref, sems):
    i = pl.program_id(0)
    n_steps = pl.num_programs(0)
    tm = x1_ref.shape[0]

    def fetch(tile, buf):
        def per_run(e, c):
            r = tile * N_EXPERTS + e
            _copy_run(ys_ref, buf_ref.at[buf], off_ref[e] + start_ref[r], toff_ref[r], cnt_ref[r],
                      sems.at[buf])
            return c

        lax.fori_loop(0, N_EXPERTS, per_run, 0)

    @pl.when(i == 0)
    def _():
        fetch(0, 0)

    @pl.when(i + 1 < n_steps)
    def _():
        fetch(i + 1, (i + 1) % 2)

    buf = i % 2
    _wait_tile(ys_ref, buf_ref.at[buf], sems.at[buf])

    rows = jnp.concatenate(
        [buf_ref[buf, pl.ds(s, SLOTS_PER_TILE, stride=SUBLANES), :] for s in range(D_MODEL // LANES)],
        axis=-1).astype(BF16)
    route = route_ref[...]
    slot_id = lax.broadcasted_iota(jnp.int32, (tm, SLOTS_PER_TILE), 1).astype(F32)
    weights = jnp.zeros((tm, SLOTS_PER_TILE), F32)
    for k in range(TOP_K):
        slot = route[:, ROUTE_SLOTS + k:ROUTE_SLOTS + k + 1]
        gate = route[:, ROUTE_GATES + k:ROUTE_GATES + k + 1]
        weights = jnp.where(slot_id == slot, gate, weights)
    out_ref[...] = x1_ref[...] + jnp.dot(weights.astype(BF16), rows, preferred_element_type=F32)


def _combine(off, runs, route, x1, ys_tiles):
    t = x1.shape[0]
    tm = ROW_TILE
    grid_spec = pltpu.PrefetchScalarGridSpec(
        num_scalar_prefetch=4,
        grid=(t // tm,),
        in_specs=[
            pl.BlockSpec((tm, LANES), lambda i, *_: (i, 0)),
            pl.BlockSpec((tm, D_MODEL), lambda i, *_: (i, 0)),
            pl.BlockSpec(memory_space=pl.ANY),
        ],
        out_specs=pl.BlockSpec((tm, D_MODEL), lambda i, *_: (i, 0)),
        scratch_shapes=[
            pltpu.VMEM((2, SLOTS_PER_TILE * SUBLANES, LANES), F32),
            pltpu.SemaphoreType.DMA((2,)),
        ],
    )
    return pl.pallas_call(
        _combine_kernel,
        grid_spec=grid_spec,
        out_shape=jax.ShapeDtypeStruct((t, D_MODEL), F32),
        compiler_params=_params(1),
        name="moe_combine",
    )(off, *runs, route, x1, ys_tiles)


def _layer(x, g_mix, w_in, q_gain, k_gain, rel_bias, ssm_a_re, ssm_a_im, ssm_log_dt, ssm_b_re,
           ssm_b_im, ssm_c_re, ssm_c_im, ssm_d, w_glu, b_glu, w_attn_proj, w_ssm_proj, w_out,
           g_ffn, w_router, b_router, w_gate_up, b_gate_up, w_down, b_down):
    b, s, _ = x.shape
    x2d = x.reshape(b * s, D_MODEL)

    w_in_b = w_in.astype(BF16)
    *qkvs, rest = _inproj(x, g_mix, jnp.concatenate([w_in_b[:, a:b] for a, b in _inproj_columns()], axis=1))

    head_of = np.arange(GROUP_WIDTH) // HEAD_DIM
    seg = jnp.asarray(head_of[:, None] == head_of[None, :], BF16)
    os_, lses = [], []
    for gi, (window, dil) in enumerate(ATTN_GROUPS):
        table = rel_bias[:, gi * HEADS_PER_GROUP:(gi + 1) * HEADS_PER_GROUP]
        o, lse = _attention_group(qkvs[gi], gi, _attn_bias(table, window, dil), q_gain[gi], k_gain[gi], seg)
        os_.append(o)
        lses.append(lse)

    ys = _ssm(rest, ssm_a_re, ssm_a_im, ssm_log_dt, ssm_b_re, ssm_b_im, ssm_c_re, ssm_c_im, ssm_d)

    expand = jnp.asarray(np.arange(LANES)[:, None] == head_of[None, :], BF16)
    pad_e = LANES - N_EXPERTS
    weights = dict(
        expand=expand, wa=w_attn_proj.astype(BF16), wglu=w_glu.astype(BF16),
        bglu=b_glu.reshape(1, SSM_WIDTH), ws=w_ssm_proj.astype(BF16), wo=w_out.astype(BF16),
        gffn=g_ffn.reshape(1, D_MODEL), wr=jnp.pad(w_router, ((0, 0), (0, pad_e))).astype(BF16),
        br=jnp.pad(b_router, (0, pad_e)).reshape(1, LANES))
    x1, h2, route, counts, run_tab = _mix(os_, lses, ys, rest, x2d, weights)

    counts_i = counts[0, :N_EXPERTS].astype(jnp.int32)
    run_table = run_tab[:, :, :N_EXPERTS].astype(jnp.int32)
    runs = [run_table[:, row].reshape(-1) for row in (RUN_START, RUN_COUNT, RUN_OFFSET)]
    xs_tiles, texp, off = _dispatch(counts_i, runs, h2, route)
    ys_tiles = _expert_ffn(texp, xs_tiles, w_gate_up, b_gate_up, w_down, b_down)
    out = _combine(off, runs, route, x1, ys_tiles)
    return out.reshape(b, s, D_MODEL)


_layer_jit = jax.jit(_layer)


def kernel(x, g_mix, w_in, q_gain, k_gain, rel_bias, ssm_a_re, ssm_a_im, ssm_log_dt, ssm_b_re, ssm_b_im, ssm_c_re, ssm_c_im, ssm_d, w_glu, b_glu, w_attn_proj, w_ssm_proj, w_out, g_ffn, w_router, b_router, w_gate_up, b_gate_up, w_down, b_down):
    return _layer_jit(x, g_mix[0], w_in[0], q_gain[0], k_gain[0], rel_bias, ssm_a_re[0], ssm_a_im[0],
                      ssm_log_dt[0], ssm_b_re[0], ssm_b_im[0], ssm_c_re[0], ssm_c_im[0], ssm_d[0],
                      w_glu[0], b_glu[0], w_attn_proj[0], w_ssm_proj[0], w_out[0], g_ffn[0],
                      w_router[0], b_router[0], w_gate_up[0], b_gate_up[0], w_down[0], b_down[0])
```

```python
import functools
import math

import jax
import jax.numpy as jnp
import numpy as np
from jax import lax
from jax.experimental import pallas as pl
from jax.experimental.pallas import tpu as pltpu

F32 = jnp.float32
BF16 = jnp.bfloat16

D_MODEL = 1024
HEAD_DIM = 64
ATTN_GROUPS = ((128, 1), (512, 4), (2048, 16))
HEADS_PER_GROUP = 8
GROUP_WIDTH = HEADS_PER_GROUP * HEAD_DIM
N_ATTN_HEADS = len(ATTN_GROUPS) * HEADS_PER_GROUP
QKV_WIDTH = 3 * N_ATTN_HEADS * HEAD_DIM
ATTN_BLOCK = 128
ATTN_STEP_BLOCKS = 8
REL_BUCKETS = 32
REL_MAX_DIST = 2048
SSM_WIDTH = 512
SSM_GROUP = 16
SSM_GROUPS = 32
SSM_STATE = 64
IN_WIDTH = QKV_WIDTH + SSM_WIDTH + 2 * D_MODEL
N_EXPERTS = 32
TOP_K = 4
D_FF = 1024
SWIGLU_LIMIT = 7.0
SWIGLU_ALPHA = 1.702
EPS = 1e-6

SUBLANES = 8
LANES = 128
MASK_VALUE = -1e30
VMEM_LIMIT = 56 * 1024 * 1024

QKV_GROUP_WIDTH = 3 * GROUP_WIDTH
REST_WIDTH = 2 * D_MODEL + SSM_WIDTH
GATE_A_BLOCK = 0
GATE_B_BLOCK = 1
U_BLOCK = 2 * D_MODEL // SSM_WIDTH

ROW_TILE = 256
MOE_TILE = 512


def _params(n_axes, vmem=VMEM_LIMIT):
    return pltpu.CompilerParams(dimension_semantics=("arbitrary",) * n_axes, vmem_limit_bytes=vmem)


def _inproj_columns():
    n_g = len(ATTN_GROUPS)
    spans = [((part * n_g + gi) * GROUP_WIDTH, (part * n_g + gi + 1) * GROUP_WIDTH)
             for gi in range(n_g) for part in range(3)]
    return spans + [(QKV_WIDTH + SSM_WIDTH, IN_WIDTH), (QKV_WIDTH, QKV_WIDTH + SSM_WIDTH)]


def _inproj_kernel(x_ref, g_ref, w_ref, q0_ref, q1_ref, q2_ref, rest_ref, stage_ref):
    x = x_ref[0]
    tm = x.shape[0]
    ms = jnp.mean(x * x, axis=-1, keepdims=True)
    h = x * lax.rsqrt(ms + EPS) * g_ref[...]
    res = jnp.dot(h.astype(BF16), w_ref[...], preferred_element_type=F32)
    w = QKV_GROUP_WIDTH
    q0_ref[0, 0] = res[:, 0:w]
    rest_ref[0] = res[:, 3 * w:]
    n_blk = w // LANES
    for c in range(2 * n_blk):
        stage_ref[c] = res[:, w + c * LANES:w + (c + 1) * LANES]
    for out_ref, gi in ((q1_ref, 1), (q2_ref, 2)):
        dil = ATTN_GROUPS[gi][1]
        for r in range(dil):
            for c in range(n_blk):
                out_ref[0, r, :, c * LANES:(c + 1) * LANES] = stage_ref[
                    (gi - 1) * n_blk + c, pl.ds(r, tm // dil, stride=dil), :]


def _inproj(x, g, w_bf16):
    b, s, _ = x.shape
    tm = ROW_TILE
    dils = [d for _, d in ATTN_GROUPS]
    out_specs = [pl.BlockSpec((1, d, tm // d, QKV_GROUP_WIDTH), lambda bi, i: (bi, 0, i, 0)) for d in dils]
    out_shape = [jax.ShapeDtypeStruct((b, d, s // d, QKV_GROUP_WIDTH), F32) for d in dils]
    return pl.pallas_call(
        _inproj_kernel,
        grid=(b, s // tm),
        in_specs=[
            pl.BlockSpec((1, tm, D_MODEL), lambda bi, i: (bi, i, 0)),
            pl.BlockSpec((1, D_MODEL), lambda bi, i: (0, 0)),
            pl.BlockSpec((D_MODEL, IN_WIDTH), lambda bi, i: (0, 0), pipeline_mode=pl.Buffered(1)),
        ],
        out_specs=out_specs + [pl.BlockSpec((1, tm, REST_WIDTH), lambda bi, i: (bi, i, 0))],
        out_shape=out_shape + [jax.ShapeDtypeStruct((b, s, REST_WIDTH), F32)],
        scratch_shapes=[pltpu.VMEM((2 * QKV_GROUP_WIDTH // LANES, tm, LANES), F32)],
        compiler_params=_params(2),
        name="inproj",
    )(x, g.reshape(1, D_MODEL), w_bf16)


def _rel_bucket_np(dist):
    max_exact = REL_BUCKETS // 2
    d = np.maximum(dist, 1).astype(np.float32)
    large = max_exact + (np.log(d / max_exact) / math.log(REL_MAX_DIST / max_exact)
                         * (REL_BUCKETS - max_exact)).astype(np.int32)
    large = np.minimum(large, REL_BUCKETS - 1)
    return np.where(dist < max_exact, dist, large).astype(np.int32)


def _attn_bias(table, window, dilation):
    blk = ATTN_BLOCK
    span = window // dilation
    qi = np.arange(blk)[:, None]
    ki = np.arange(2 * blk)[None, :]
    steps = qi + blk - ki
    band = (steps >= 0) & (steps <= span)
    bucket = _rel_bucket_np(np.clip(steps, 0, None) * dilation)
    onehot = np.eye(REL_BUCKETS, dtype=np.float32)[bucket]
    bias = jnp.einsum('qkb,bh->hqk', onehot, table.astype(F32), precision=lax.Precision.HIGHEST)
    b0 = jnp.where(band[None], bias, MASK_VALUE)
    b1 = jnp.where((band & (ki >= blk))[None], bias, MASK_VALUE)
    return jnp.stack([b0, b1], axis=0)


def _attn_kernel(q_ref, k_ref, v_ref, bias_ref, qg_ref, kg_ref, seg_ref, o_ref, lse_ref,
                 qbuf_ref, kbuf_ref, vbuf_ref):
    blk = ATTN_BLOCK
    rows = q_ref.shape[2]
    i = pl.program_id(2)
    seg = seg_ref[...]

    def head_norm(t, gain):
        ss = jnp.dot((t * t).astype(BF16), seg, preferred_element_type=F32)
        return t * lax.rsqrt(ss * (1.0 / HEAD_DIM) + EPS) * gain

    qbuf_ref[...] = (head_norm(q_ref[0, 0], qg_ref[...]) * (HEAD_DIM ** -0.5)).astype(BF16)

    @pl.when(i == 0)
    def _():
        kbuf_ref[0:blk, :] = jnp.zeros((blk, GROUP_WIDTH), BF16)
        vbuf_ref[0:blk, :] = jnp.zeros((blk, GROUP_WIDTH), BF16)

    @pl.when(i > 0)
    def _():
        kbuf_ref[0:blk, :] = kbuf_ref[rows:rows + blk, :]
        vbuf_ref[0:blk, :] = vbuf_ref[rows:rows + blk, :]

    kbuf_ref[blk:blk + rows, :] = head_norm(k_ref[0, 0], kg_ref[...]).astype(BF16)
    vbuf_ref[blk:blk + rows, :] = v_ref[0, 0].astype(BF16)
    first = (i == 0).astype(jnp.int32)

    lane = lax.broadcasted_iota(jnp.int32, (blk, LANES), 1)
    low_half = lane < HEAD_DIM
    for j in range(rows // blk):
        variant = first if j == 0 else 0
        lse_tile = jnp.zeros((blk, LANES), F32)
        outs = []
        for p in range(GROUP_WIDTH // LANES):
            qp = qbuf_ref[j * blk:(j + 1) * blk, p * LANES:(p + 1) * LANES]
            kp = kbuf_ref[j * blk:(j + 2) * blk, p * LANES:(p + 1) * LANES]
            vp = vbuf_ref[j * blk:(j + 2) * blk, p * LANES:(p + 1) * LANES]
            pair = []
            for hh in range(2):
                head = 2 * p + hh
                sel = low_half if hh == 0 else jnp.logical_not(low_half)
                qm = jnp.where(sel, qp, jnp.zeros_like(qp))
                s = lax.dot_general(qm, kp, (((1,), (1,)), ((), ())), preferred_element_type=F32)
                s = s + bias_ref[variant, head]
                mx = jnp.max(s, axis=-1, keepdims=True)
                pe = jnp.exp(s - mx)
                den = jnp.sum(pe, axis=-1, keepdims=True)
                o = jnp.dot(pe.astype(BF16), vp, preferred_element_type=F32)
                pair.append(o * (1.0 / den))
                lse_tile = jnp.where(lane == head, mx + jnp.log(den), lse_tile)
            outs.append(jnp.where(low_half, pair[0], pair[1]))
        o_ref[0, 0, j * blk:(j + 1) * blk, :] = jnp.concatenate(outs, axis=-1)
        lse_ref[0, 0, j * blk:(j + 1) * blk, :] = lse_tile


def _attention_group(qkv, gi, bias, q_gain, k_gain, seg):
    b, dil, length, _ = qkv.shape
    blk = ATTN_BLOCK
    rows = min(ATTN_STEP_BLOCKS * blk, length)

    def col(c):
        return pl.BlockSpec((1, 1, rows, GROUP_WIDTH), lambda bi, r, i: (bi, r, i, c))

    const = lambda shape: pl.BlockSpec(shape, lambda bi, r, i: (0,) * len(shape))
    return pl.pallas_call(
        _attn_kernel,
        grid=(b, dil, length // rows),
        in_specs=[
            col(0), col(1), col(2),
            const((2, HEADS_PER_GROUP, blk, 2 * blk)),
            const((1, GROUP_WIDTH)),
            const((1, GROUP_WIDTH)),
            const((GROUP_WIDTH, GROUP_WIDTH)),
        ],
        out_specs=[
            pl.BlockSpec((1, 1, rows, GROUP_WIDTH), lambda bi, r, i: (bi, r, i, 0)),
            pl.BlockSpec((1, 1, rows, LANES), lambda bi, r, i: (bi, r, i, 0)),
        ],
        out_shape=[
            jax.ShapeDtypeStruct((b, dil, length, GROUP_WIDTH), F32),
            jax.ShapeDtypeStruct((b, dil, length, LANES), F32),
        ],
        scratch_shapes=[
            pltpu.VMEM((rows, GROUP_WIDTH), BF16),
            pltpu.VMEM((blk + rows, GROUP_WIDTH), BF16),
            pltpu.VMEM((blk + rows, GROUP_WIDTH), BF16),
        ],
        compiler_params=_params(3),
        name=f"attn_g{gi}",
    )(qkv, qkv, qkv, bias,
      jnp.tile(q_gain, HEADS_PER_GROUP).reshape(1, GROUP_WIDTH),
      jnp.tile(k_gain, HEADS_PER_GROUP).reshape(1, GROUP_WIDTH), seg)


HALF_STATE = SSM_GROUPS * SSM_STATE // 2
SCAN_COLS = 4
SSM_STEP = 64
TAB_STEP, TAB_CHUNK = 0, 1


def _ssm_tables(a_re, a_im, log_dt, b_re, b_im, c_re, c_im, chunk_len):
    lam = lax.complex(a_re.astype(F32), a_im.astype(F32))
    dt = jnp.exp(log_dt.astype(F32))[:, None]
    a_bar = jnp.exp(lam * dt)
    b_bar = ((a_bar - 1.0) / lam)[..., None] * lax.complex(b_re.astype(F32), b_im.astype(F32))

    tab = jnp.stack([a_bar.reshape(-1), jnp.exp(lam * dt * chunk_len).reshape(-1)])
    tab = jnp.stack([tab.real, tab.imag], axis=1).astype(F32)
    tab = jnp.broadcast_to(tab[:, :, None, :], (2, 2, SUBLANES, tab.shape[-1]))

    gh = SSM_GROUPS // 2
    eye = jnp.eye(gh, dtype=F32)

    def in_mat(m):
        return jnp.einsum('gnc,gh->gchn', m, eye).reshape(gh * SSM_GROUP, gh * SSM_STATE)

    def out_mat(m):
        return jnp.einsum('gcn,gh->gnhc', m, eye).reshape(gh * SSM_STATE, gh * SSM_GROUP)

    b_mats, c_mats = [], []
    for h in range(2):
        sl = slice(h * gh, (h + 1) * gh)
        b_mats.append(jnp.concatenate([in_mat(b_bar.real[sl]), in_mat(b_bar.imag[sl])], axis=1))
        c_mats.append(jnp.concatenate([out_mat(c_re.astype(F32)[sl]),
                                       -out_mat(c_im.astype(F32)[sl])], axis=0))
    return tab, jnp.stack(b_mats).astype(BF16), jnp.stack(c_mats).astype(BF16)


def _ssm_columns():
    half_w = 2 * HALF_STATE
    cols = []
    for c in range(2 * HALF_STATE // LANES):
        h, k = divmod(c, HALF_STATE // LANES)
        cols.append((c * LANES, h * half_w + k * LANES, h * half_w + HALF_STATE + k * LANES))
    return cols


def _ssm_step_major(u_ref, stage_ref):
    n_rows, tj = u_ref.shape[0], u_ref.shape[1]
    n_lane_blk = SSM_WIDTH // LANES
    for r in range(n_rows):
        for c in range(n_lane_blk):
            stage_ref[c, pl.ds(r, tj, stride=SUBLANES), :] = u_ref[r, :, c * LANES:(c + 1) * LANES]
    return jnp.concatenate([stage_ref[c] for c in range(n_lane_blk)], axis=-1)


def _ssm_scan(tab_ref, bu_ref, state_ref, n_steps, states_ref=None):
    cols = _ssm_columns()
    for c0 in range(0, len(cols), SCAN_COLS):
        grp = cols[c0:c0 + SCAN_COLS]
        consts = [(tab_ref[TAB_STEP, 0, :, tc:tc + LANES], tab_ref[TAB_STEP, 1, :, tc:tc + LANES])
                  for tc, _, _ in grp]

        def body(j, st, grp=grp, consts=consts):
            at_j = pl.ds(pl.multiple_of(j * SUBLANES, SUBLANES), SUBLANES)
            new = []
            for (_, rc, ic), (ar, ai), (xr, xi) in zip(grp, consts, st):
                nr = ar * xr - ai * xi + bu_ref[at_j, rc:rc + LANES]
                ni = ar * xi + ai * xr + bu_ref[at_j, ic:ic + LANES]
                if states_ref is not None:
                    states_ref[at_j, rc:rc + LANES] = nr
                    states_ref[at_j, ic:ic + LANES] = ni
                new.append((nr, ni))
            return tuple(new)

        init = tuple((state_ref[:, rc:rc + LANES], state_ref[:, ic:ic + LANES]) for _, rc, ic in grp)
        fin = lax.fori_loop(0, n_steps, body, init)
        for (_, rc, ic), (xr, xi) in zip(grp, fin):
            state_ref[:, rc:rc + LANES] = xr
            state_ref[:, ic:ic + LANES] = xi


def _ssm_ends_kernel(u_ref, tab_ref, bmat_ref, bu_ref, ends_ref, state_ref, stage_ref):
    i = pl.program_id(0)
    tj = u_ref.shape[1]
    half_w = 2 * HALF_STATE

    @pl.when(i == 0)
    def _():
        state_ref[...] = jnp.zeros_like(state_ref)

    ub = _ssm_step_major(u_ref, stage_ref).astype(BF16)
    hw = SSM_WIDTH // 2
    for h in range(2):
        bu_ref[:, h * half_w:(h + 1) * half_w] = jnp.dot(
            ub[:, h * hw:(h + 1) * hw], bmat_ref[h], preferred_element_type=F32)
    _ssm_scan(tab_ref, bu_ref, state_ref, tj)
    ends_ref[...] = state_ref[...]


def _ssm_out_kernel(u_ref, bu_ref, ends_ref, tab_ref, cmat_ref, d_ref, y_ref, xs_ref, state_ref,
                    stage_ref, *, chunks):
    i = pl.program_id(0)
    n_rows, tj = u_ref.shape[0], u_ref.shape[1]
    half_w = 2 * HALF_STATE
    n_lane_blk = SSM_WIDTH // LANES

    @pl.when(i == 0)
    def _():
        row = lax.broadcasted_iota(jnp.int32, (SUBLANES, LANES), 0)
        seq_start = (row % chunks) == 0
        for tc, rc, ic in _ssm_columns():
            er, ei = ends_ref[:, rc:rc + LANES], ends_ref[:, ic:ic + LANES]
            pr = tab_ref[TAB_CHUNK, 0, :, tc:tc + LANES]
            pi = tab_ref[TAB_CHUNK, 1, :, tc:tc + LANES]
            xr = jnp.zeros_like(er)
            xi = jnp.zeros_like(ei)
            for _ in range(chunks - 1):
                nr = pr * xr - pi * xi + er
                ni = pr * xi + pi * xr + ei
                xr = jnp.where(seq_start, 0.0, pltpu.roll(nr, 1, 0))
                xi = jnp.where(seq_start, 0.0, pltpu.roll(ni, 1, 0))
            state_ref[:, rc:rc + LANES] = xr
            state_ref[:, ic:ic + LANES] = xi

    _ssm_scan(tab_ref, bu_ref, state_ref, tj, states_ref=xs_ref)
    ys = []
    for h in range(2):
        xh = xs_ref[:, h * half_w:(h + 1) * half_w].astype(BF16)
        ys.append(jnp.dot(xh, cmat_ref[h], preferred_element_type=F32))
    y = jnp.concatenate(ys, axis=-1) + d_ref[...] * _ssm_step_major(u_ref, stage_ref)
    for c in range(n_lane_blk):
        stage_ref[c] = y[:, c * LANES:(c + 1) * LANES]
    for r in range(n_rows):
        y_ref[r] = jnp.concatenate(
            [stage_ref[c, pl.ds(r, tj, stride=SUBLANES), :] for c in range(n_lane_blk)], axis=-1)


def _ssm(rest, a_re, a_im, log_dt, b_re, b_im, c_re, c_im, d_skip):
    b, s, _ = rest.shape
    chunks = SUBLANES // b
    chunk_len = s // chunks
    tj = min(SSM_STEP, chunk_len)
    n_steps = chunk_len // tj
    n_state = 4 * HALF_STATE
    tab, bmat, cmat = _ssm_tables(a_re, a_im, log_dt, b_re, b_im, c_re, c_im, chunk_len)
    u_rows = rest.reshape(SUBLANES, chunk_len, REST_WIDTH)
    u_spec = pl.BlockSpec((SUBLANES, tj, SSM_WIDTH), lambda j: (0, j, U_BLOCK))
    bu_spec = pl.BlockSpec((SUBLANES * tj, n_state), lambda j: (j, 0))
    whole = lambda a: pl.BlockSpec(a.shape, lambda j: (0,) * a.ndim)
    ends_shape = jax.ShapeDtypeStruct((SUBLANES, n_state), F32)
    stage = pltpu.VMEM((SSM_WIDTH // LANES, SUBLANES * tj, LANES), F32)
    bu, ends = pl.pallas_call(
        _ssm_ends_kernel,
        grid=(n_steps,),
        in_specs=[u_spec, whole(tab), whole(bmat)],
        out_specs=[bu_spec, pl.BlockSpec((SUBLANES, n_state), lambda j: (0, 0))],
        out_shape=[jax.ShapeDtypeStruct((n_steps * SUBLANES * tj, n_state), F32), ends_shape],
        scratch_shapes=[pltpu.VMEM((SUBLANES, n_state), F32), stage],
        compiler_params=_params(1),
        name="ssm_ends",
    )(u_rows, tab, bmat)
    d2 = d_skip.reshape(1, SSM_WIDTH)
    return pl.pallas_call(
        functools.partial(_ssm_out_kernel, chunks=chunks),
        grid=(n_steps,),
        in_specs=[u_spec, bu_spec, whole(ends), whole(tab), whole(cmat), whole(d2)],
        out_specs=pl.BlockSpec((SUBLANES, tj, SSM_WIDTH), lambda j: (0, j, 0)),
        out_shape=jax.ShapeDtypeStruct((SUBLANES, chunk_len, SSM_WIDTH), F32),
        scratch_shapes=[pltpu.VMEM((SUBLANES * tj, n_state), F32),
                        pltpu.VMEM((SUBLANES, n_state), F32), stage],
        compiler_params=_params(1),
        name="ssm_out",
    )(u_rows, bu, ends, tab, cmat, d2).reshape(b * s, SSM_WIDTH)


ROUTE_IDS, ROUTE_GATES, ROUTE_SLOTS = 0, TOP_K, 2 * TOP_K
RUN_START, RUN_COUNT, RUN_OFFSET = 0, 1, 2
SLOTS_PER_TILE = ROW_TILE * TOP_K


def _mix_kernel(o0_ref, o1_ref, o2_ref, l0_ref, l1_ref, l2_ref, ys_ref, ga_ref, gb_ref, x_ref,
                expand_ref, wa_ref, wglu_ref, bglu_ref, ws_ref, wo_ref, gffn_ref, wr_ref, br_ref,
                x1_ref, h2_ref, route_ref, counts_ref, runs_ref, run_ref, ostage_ref, lstage_ref):
    tm = x_ref.shape[0]

    @pl.when(jnp.logical_and(pl.program_id(0) == 0, pl.program_id(1) == 0))
    def _():
        run_ref[...] = jnp.zeros_like(run_ref)

    for slot, (o_ref, l_ref) in enumerate(((o1_ref, l1_ref), (o2_ref, l2_ref))):
        dil = ATTN_GROUPS[slot + 1][1]
        for r in range(dil):
            for c in range(GROUP_WIDTH // LANES):
                ostage_ref[slot, c, pl.ds(r, tm // dil, stride=dil), :] = o_ref[
                    0, r, :, c * LANES:(c + 1) * LANES]
            lstage_ref[slot, pl.ds(r, tm // dil, stride=dil), :] = l_ref[0, r]

    def sigmoid(z):
        return 0.5 * jnp.tanh(0.5 * z) + 0.5

    expand = expand_ref[...]
    n_lane_blk = GROUP_WIDTH // LANES

    def mix_rows(lo, n):
        rows = slice(lo, lo + n)
        group_out = [o0_ref[0, 0, rows, :]] + [
            jnp.concatenate([ostage_ref[slot, c, rows, :] for c in range(n_lane_blk)], axis=-1)
            for slot in range(2)]
        lses = [l0_ref[0, 0, rows, :], lstage_ref[0, rows, :], lstage_ref[1, rows, :]]
        mx = jnp.maximum(jnp.maximum(lses[0], lses[1]), lses[2])
        es = [jnp.exp(l - mx) for l in lses]
        inv = 1.0 / (es[0] + es[1] + es[2])
        attn = jnp.zeros((n, GROUP_WIDTH), F32)
        for e, o_g in zip(es, group_out):
            attn = attn + jnp.dot((e * inv).astype(BF16), expand, preferred_element_type=F32) * o_g
        y_a = jnp.dot(attn.astype(BF16), wa_ref[...], preferred_element_type=F32)

        ys = ys_ref[rows, :]
        ys = 0.5 * ys * (1.0 + jnp.tanh(math.sqrt(2.0 / math.pi) * (ys + 0.044715 * (ys * ys * ys))))
        glu = jnp.dot(ys.astype(BF16), wglu_ref[...], preferred_element_type=F32) + bglu_ref[...]
        ys = ys * sigmoid(glu)
        y_b = jnp.dot(ys.astype(BF16), ws_ref[...], preferred_element_type=F32)

        mixed = sigmoid(ga_ref[rows, :]) * y_a + sigmoid(gb_ref[rows, :]) * y_b
        x1 = x_ref[rows, :] + jnp.dot(mixed.astype(BF16), wo_ref[...], preferred_element_type=F32)
        x1_ref[rows, :] = x1

        ms = jnp.mean(x1 * x1, axis=-1, keepdims=True)
        h2b = (x1 * lax.rsqrt(ms + EPS) * gffn_ref[...]).astype(BF16)
        h2_ref[rows, :] = h2b
        return jnp.dot(h2b, wr_ref[...], preferred_element_type=F32) + br_ref[...]

    logits = mix_rows(0, tm)

    lane = lax.broadcasted_iota(jnp.int32, (tm, LANES), 1)
    lane_f = lane.astype(F32)
    work = jnp.where(lane < N_EXPERTS, logits, -jnp.inf)
    sel_mask = jnp.zeros((tm, LANES), F32)
    route = jnp.zeros((tm, LANES), F32)
    vals, hots = [], []
    for k in range(TOP_K):
        v = jnp.max(work, axis=-1, keepdims=True)
        idx = jnp.min(jnp.where(work == v, lane_f, float(LANES)), axis=-1, keepdims=True)
        hot = lane_f == idx
        work = jnp.where(hot, -jnp.inf, work)
        sel_mask = jnp.where(hot, 1.0, sel_mask)
        route = jnp.where(lane == ROUTE_IDS + k, idx, route)
        vals.append(v)
        hots.append(hot)
    exps = [jnp.exp(v - vals[0]) for v in vals]
    inv_den = 1.0 / (exps[0] + exps[1] + exps[2] + exps[3])

    r_i = lax.broadcasted_iota(jnp.int32, (tm, tm), 0)
    c_i = lax.broadcasted_iota(jnp.int32, (tm, tm), 1)
    tri = jnp.where(c_i < r_i, 1.0, 0.0).astype(BF16)
    local = jnp.dot(tri, sel_mask.astype(BF16), preferred_element_type=F32)
    count = jnp.sum(sel_mask, axis=0, keepdims=True)
    e_r = lax.broadcasted_iota(jnp.int32, (LANES, LANES), 0)
    e_c = lax.broadcasted_iota(jnp.int32, (LANES, LANES), 1)
    upper = jnp.where(e_r < e_c, 1.0, 0.0).astype(BF16)
    offset = jnp.dot(jnp.broadcast_to(count, (SUBLANES, LANES)).astype(BF16), upper,
                     preferred_element_type=F32)
    slot_of = local + offset[0:1, :]
    for k in range(TOP_K):
        slot = jnp.sum(jnp.where(hots[k], slot_of, 0.0), axis=-1, keepdims=True)
        route = jnp.where(lane == ROUTE_GATES + k, exps[k] * inv_den, route)
        route = jnp.where(lane == ROUTE_SLOTS + k, slot, route)
    route_ref[...] = route
    row8 = lax.broadcasted_iota(jnp.int32, (SUBLANES, LANES), 0)
    runs = jnp.where(row8 == RUN_START, run_ref[...],
                     jnp.where(row8 == RUN_COUNT, jnp.broadcast_to(count, (SUBLANES, LANES)),
                               jnp.where(row8 == RUN_OFFSET, offset, 0.0)))
    runs_ref[0] = runs
    total = run_ref[0:1, :] + count
    run_ref[...] = jnp.broadcast_to(total, run_ref.shape)
    counts_ref[...] = jnp.broadcast_to(total, counts_ref.shape)


def _mix(os_, lses, ys, rest, x2d, w):
    b, s, _ = rest.shape
    t = b * s
    tm = ROW_TILE
    n_i = s // tm
    rest2d = rest.reshape(t, REST_WIDTH)
    row = lambda width, blk=0: pl.BlockSpec((tm, width), lambda bi, i: (bi * n_i + i, blk))
    const = lambda a: pl.BlockSpec(a.shape, lambda bi, i: (0,) * a.ndim)
    grouped = lambda width: [pl.BlockSpec((1, d, tm // d, width), lambda bi, i: (bi, 0, i, 0))
                             for _, d in ATTN_GROUPS]
    consts = [w["expand"], w["wa"], w["wglu"], w["bglu"], w["ws"], w["wo"], w["gffn"], w["wr"], w["br"]]
    n_dilated = len(ATTN_GROUPS) - 1
    return pl.pallas_call(
        _mix_kernel,
        grid=(b, n_i),
        in_specs=grouped(GROUP_WIDTH) + grouped(LANES) + [
            row(SSM_WIDTH),
            row(D_MODEL, GATE_A_BLOCK),
            row(D_MODEL, GATE_B_BLOCK),
            row(D_MODEL),
        ] + [const(a) for a in consts],
        out_specs=[
            row(D_MODEL),
            row(D_MODEL),
            row(LANES),
            pl.BlockSpec((SUBLANES, LANES), lambda bi, i: (0, 0)),
            pl.BlockSpec((1, SUBLANES, LANES), lambda bi, i: (bi * n_i + i, 0, 0)),
        ],
        out_shape=[
            jax.ShapeDtypeStruct((t, D_MODEL), F32),
            jax.ShapeDtypeStruct((t, D_MODEL), BF16),
            jax.ShapeDtypeStruct((t, LANES), F32),
            jax.ShapeDtypeStruct((SUBLANES, LANES), F32),
            jax.ShapeDtypeStruct((t // tm, SUBLANES, LANES), F32),
        ],
        scratch_shapes=[
            pltpu.VMEM((SUBLANES, LANES), F32),
            pltpu.VMEM((n_dilated, GROUP_WIDTH // LANES, tm, LANES), F32),
            pltpu.VMEM((n_dilated, tm, LANES), F32),
        ],
        compiler_params=_params(2),
        name="mix_router",
    )(*os_, *lses, ys, rest2d, rest2d, x2d, *consts)


def _n_moe_tiles(t):
    return t * TOP_K // MOE_TILE + N_EXPERTS


def _copy_run(src_ref, dst_ref, src_row, dst_row, n_rows, sem):
    size = ROW_TILE
    while size >= 1:
        @pl.when((n_rows & size) != 0)
        def _(size=size, src_row=src_row, dst_row=dst_row):
            pltpu.make_async_copy(
                src_ref.at[pl.ds(pl.multiple_of(src_row * SUBLANES, SUBLANES), size * SUBLANES), :],
                dst_ref.at[pl.ds(pl.multiple_of(dst_row * SUBLANES, SUBLANES), size * SUBLANES), :],
                sem).start()
        step = n_rows & size
        src_row = src_row + step
        dst_row = dst_row + step
        size //= 2


def _wait_tile(hbm_ref, vmem_ref, sem):
    pltpu.make_async_copy(hbm_ref.at[pl.ds(0, SLOTS_PER_TILE * SUBLANES), :], vmem_ref, sem).wait()


def _dispatch_kernel(counts_ref, start_ref, cnt_ref, toff_ref, h2_ref, route_ref,
                     xs_ref, texp_ref, off_ref, stage_ref, sems):
    i = pl.program_id(0)
    n_steps = pl.num_programs(0)
    n_tiles = texp_ref.shape[0] - 1

    @pl.when(i == 0)
    def _():
        def per_expert(e, start):
            off_ref[e] = start
            n_t = (counts_ref[e] + (MOE_TILE - 1)) // MOE_TILE

            def mark(ti, c):
                texp_ref[start // MOE_TILE + ti] = e
                return c

            lax.fori_loop(0, n_t, mark, 0)
            return start + n_t * MOE_TILE

        end = lax.fori_loop(0, N_EXPERTS, per_expert, 0)
        n_active = end // MOE_TILE
        texp_ref[n_tiles] = n_active

        def fill(ti, c):
            texp_ref[ti] = texp_ref[n_active - 1]
            return c

        lax.fori_loop(n_active, n_tiles, fill, 0)

    slots_t = route_ref[...].T
    slot_id = lax.broadcasted_iota(jnp.int32, (SLOTS_PER_TILE, ROW_TILE), 0).astype(F32)
    perm = jnp.zeros((SLOTS_PER_TILE, ROW_TILE), F32)
    for k in range(TOP_K):
        perm = jnp.where(slot_id == slots_t[ROUTE_SLOTS + k:ROUTE_SLOTS + k + 1, :], 1.0, perm)
    srt = jnp.dot(perm.astype(BF16), h2_ref[...], preferred_element_type=F32)
    buf = i % 2
    for s in range(D_MODEL // LANES):
        stage_ref[buf, pl.ds(s, SLOTS_PER_TILE, stride=SUBLANES), :] = srt[:, s * LANES:(s + 1) * LANES]

    @pl.when(i > 0)
    def _():
        _wait_tile(xs_ref, stage_ref.at[1 - buf], sems.at[1 - buf])

    def per_run(e, c):
        r = i * N_EXPERTS + e
        _copy_run(stage_ref.at[buf], xs_ref, toff_ref[r], off_ref[e] + start_ref[r], cnt_ref[r],
                  sems.at[buf])
        return c

    lax.fori_loop(0, N_EXPERTS, per_run, 0)

    @pl.when(i == n_steps - 1)
    def _():
        _wait_tile(xs_ref, stage_ref.at[buf], sems.at[buf])


def _dispatch(counts, runs, h2, route):
    t = h2.shape[0]
    tm = ROW_TILE
    n_tiles = _n_moe_tiles(t)
    grid_spec = pltpu.PrefetchScalarGridSpec(
        num_scalar_prefetch=4,
        grid=(t // tm,),
        in_specs=[
            pl.BlockSpec((tm, D_MODEL), lambda i, *_: (i, 0)),
            pl.BlockSpec((tm, LANES), lambda i, *_: (i, 0)),
        ],
        out_specs=[
            pl.BlockSpec(memory_space=pl.ANY),
            pl.BlockSpec((n_tiles + 1,), lambda i, *_: (0,), memory_space=pltpu.SMEM),
            pl.BlockSpec((N_EXPERTS,), lambda i, *_: (0,), memory_space=pltpu.SMEM),
        ],
        scratch_shapes=[
            pltpu.VMEM((2, SLOTS_PER_TILE * SUBLANES, LANES), F32),
            pltpu.SemaphoreType.DMA((2,)),
        ],
    )
    return pl.pallas_call(
        _dispatch_kernel,
        grid_spec=grid_spec,
        out_shape=[
            jax.ShapeDtypeStruct((n_tiles * MOE_TILE * SUBLANES, LANES), F32),
            jax.ShapeDtypeStruct((n_tiles + 1,), jnp.int32),
            jax.ShapeDtypeStruct((N_EXPERTS,), jnp.int32),
        ],
        compiler_params=_params(1),
        name="moe_dispatch",
    )(counts, *runs, h2, route)


def _ffn_kernel(texp_ref, xs_ref, wgu_ref, bgu_ref, wd_ref, bd_ref, out_ref, wgu_bf, wd_bf, last_ref):
    i = pl.program_id(0)
    tm = MOE_TILE
    n_active = texp_ref[pl.num_programs(0)]
    e = texp_ref[i]

    @pl.when(i == 0)
    def _():
        last_ref[0] = -1

    @pl.when(jnp.logical_and(i < n_active, e != last_ref[0]))
    def _():
        wgu_bf[...] = wgu_ref[0].astype(BF16)
        wd_bf[...] = wd_ref[0].astype(BF16)
        last_ref[0] = e

    @pl.when(i < n_active)
    def _():
        x = jnp.concatenate(
            [xs_ref[pl.ds(s, tm, stride=SUBLANES), :] for s in range(D_MODEL // LANES)], axis=-1)
        gu = jnp.dot(x.astype(BF16), wgu_bf[...], preferred_element_type=F32) + bgu_ref[0]
        x_glu = jnp.minimum(gu[:, :D_FF], SWIGLU_LIMIT)
        x_lin = jnp.clip(gu[:, D_FF:], -SWIGLU_LIMIT, SWIGLU_LIMIT)
        act = x_glu * jax.nn.sigmoid(SWIGLU_ALPHA * x_glu) * (x_lin + 1.0)
        out = jnp.dot(act.astype(BF16), wd_bf[...], preferred_element_type=F32) + bd_ref[0]
        for s in range(D_MODEL // LANES):
            out_ref[pl.ds(s, tm, stride=SUBLANES), :] = out[:, s * LANES:(s + 1) * LANES]


def _expert_ffn(texp, xs_tiles, w_gu, b_gu, w_down, b_down):
    n_tiles = texp.shape[0] - 1
    tm = MOE_TILE
    grid_spec = pltpu.PrefetchScalarGridSpec(
        num_scalar_prefetch=1,
        grid=(n_tiles,),
        in_specs=[
            pl.BlockSpec((tm * SUBLANES, LANES), lambda i, te: (jnp.minimum(i, te[n_tiles] - 1), 0)),
            pl.BlockSpec((1, D_MODEL, 2 * D_FF), lambda i, te: (te[i], 0, 0)),
            pl.BlockSpec((1, 1, 2 * D_FF), lambda i, te: (te[i], 0, 0)),
            pl.BlockSpec((1, D_FF, D_MODEL), lambda i, te: (te[i], 0, 0)),
            pl.BlockSpec((1, 1, D_MODEL), lambda i, te: (te[i], 0, 0)),
        ],
        out_specs=pl.BlockSpec((tm * SUBLANES, LANES), lambda i, te: (i, 0)),
        scratch_shapes=[
            pltpu.VMEM((D_MODEL, 2 * D_FF), BF16),
            pltpu.VMEM((D_FF, D_MODEL), BF16),
            pltpu.SMEM((1,), jnp.int32),
        ],
    )
    return pl.pallas_call(
        _ffn_kernel,
        grid_spec=grid_spec,
        out_shape=jax.ShapeDtypeStruct((n_tiles * tm * SUBLANES, LANES), F32),
        compiler_params=_params(1),
        name="moe_ffn",
    )(texp, xs_tiles, w_gu, b_gu.reshape(N_EXPERTS, 1, 2 * D_FF), w_down,
      b_down.reshape(N_EXPERTS, 1, D_MODEL))


def _combine_kernel(off_ref, start_ref, cnt_ref, toff_ref, route_ref, x1_ref, ys_ref, out_ref,
                    buf_ref, sems):
    i = pl.program_id(0)
    n_steps = pl.num_programs(0)
    tm = x1_ref.shape[0]

    def fetch(tile, buf):
        def per_run(e, c):
            r = tile * N_EXPERTS + e
            _copy_run(ys_ref, buf_ref.at[buf], off_ref[e] + start_ref[r], toff_ref[r], cnt_ref[r],
                      sems.at[buf])
            return c

        lax.fori_loop(0, N_EXPERTS, per_run, 0)

    @pl.when(i == 0)
    def _():
        fetch(0, 0)

    @pl.when(i + 1 < n_steps)
    def _():
        fetch(i + 1, (i + 1) % 2)

    buf = i % 2
    _wait_tile(ys_ref, buf_ref.at[buf], sems.at[buf])

    rows = jnp.concatenate(
        [buf_ref[buf, pl.ds(s, SLOTS_PER_TILE, stride=SUBLANES), :] for s in range(D_MODEL // LANES)],
        axis=-1).astype(BF16)
    route = route_ref[...]
    slot_id = lax.broadcasted_iota(jnp.int32, (tm, SLOTS_PER_TILE), 1).astype(F32)
    weights = jnp.zeros((tm, SLOTS_PER_TILE), F32)
    for k in range(TOP_K):
        slot = route[:, ROUTE_SLOTS + k:ROUTE_SLOTS + k + 1]
        gate = route[:, ROUTE_GATES + k:ROUTE_GATES + k + 1]
        weights = jnp.where(slot_id == slot, gate, weights)
    out_ref[...] = x1_ref[...] + jnp.dot(weights.astype(BF16), rows, preferred_element_type=F32)


def _combine(off, runs, route, x1, ys_tiles):
    t = x1.shape[0]
    tm = ROW_TILE
    grid_spec = pltpu.PrefetchScalarGridSpec(
        num_scalar_prefetch=4,
        grid=(t // tm,),
        in_specs=[
            pl.BlockSpec((tm, LANES), lambda i, *_: (i, 0)),
            pl.BlockSpec((tm, D_MODEL), lambda i, *_: (i, 0)),
            pl.BlockSpec(memory_space=pl.ANY),
        ],
        out_specs=pl.BlockSpec((tm, D_MODEL), lambda i, *_: (i, 0)),
        scratch_shapes=[
            pltpu.VMEM((2, SLOTS_PER_TILE * SUBLANES, LANES), F32),
            pltpu.SemaphoreType.DMA((2,)),
        ],
    )
    return pl.pallas_call(
        _combine_kernel,
        grid_spec=grid_spec,
        out_shape=jax.ShapeDtypeStruct((t, D_MODEL), F32),
        compiler_params=_params(1),
        name="moe_combine",
    )(off, *runs, route, x1, ys_tiles)


def _layer(x, g_mix, w_in, q_gain, k_gain, rel_bias, ssm_a_re, ssm_a_im, ssm_log_dt, ssm_b_re,
           ssm_b_im, ssm_c_re, ssm_c_im, ssm_d, w_glu, b_glu, w_attn_proj, w_ssm_proj, w_out,
           g_ffn, w_router, b_router, w_gate_up, b_gate_up, w_down, b_down):
    b, s, _ = x.shape
    x2d = x.reshape(b * s, D_MODEL)

    w_in_b = w_in.astype(BF16)
    *qkvs, rest = _inproj(x, g_mix, jnp.concatenate([w_in_b[:, a:b] for a, b in _inproj_columns()], axis=1))

    head_of = np.arange(GROUP_WIDTH) // HEAD_DIM
    seg = jnp.asarray(head_of[:, None] == head_of[None, :], BF16)
    os_, lses = [], []
    for gi, (window, dil) in enumerate(ATTN_GROUPS):
        table = rel_bias[:, gi * HEADS_PER_GROUP:(gi + 1) * HEADS_PER_GROUP]
        o, lse = _attention_group(qkvs[gi], gi, _attn_bias(table, window, dil), q_gain[gi], k_gain[gi], seg)
        os_.append(o)
        lses.append(lse)

    ys = _ssm(rest, ssm_a_re, ssm_a_im, ssm_log_dt, ssm_b_re, ssm_b_im, ssm_c_re, ssm_c_im, ssm_d)

    expand = jnp.asarray(np.arange(LANES)[:, None] == head_of[None, :], BF16)
    pad_e = LANES - N_EXPERTS
    weights = dict(
        expand=expand, wa=w_attn_proj.astype(BF16), wglu=w_glu.astype(BF16),
        bglu=b_glu.reshape(1, SSM_WIDTH), ws=w_ssm_proj.astype(BF16), wo=w_out.astype(BF16),
        gffn=g_ffn.reshape(1, D_MODEL), wr=jnp.pad(w_router, ((0, 0), (0, pad_e))).astype(BF16),
        br=jnp.pad(b_router, (0, pad_e)).reshape(1, LANES))
    x1, h2, route, counts, run_tab = _mix(os_, lses, ys, rest, x2d, weights)

    counts_i = counts[0, :N_EXPERTS].astype(jnp.int32)
    run_table = run_tab[:, :, :N_EXPERTS].astype(jnp.int32)
    runs = [run_table[:, row].reshape(-1) for row in (RUN_START, RUN_COUNT, RUN_OFFSET)]
    xs_tiles, texp, off = _dispatch(counts_i, runs, h2, route)
    ys_tiles = _expert_ffn(texp, xs_tiles, w_gate_up, b_gate_up, w_down, b_down)
    out = _combine(off, runs, route, x1, ys_tiles)
    return out.reshape(b, s, D_MODEL)


_layer_jit = jax.jit(_layer)


def kernel(x, g_mix, w_in, q_gain, k_gain, rel_bias, ssm_a_re, ssm_a_im, ssm_log_dt, ssm_b_re, ssm_b_im, ssm_c_re, ssm_c_im, ssm_d, w_glu, b_glu, w_attn_proj, w_ssm_proj, w_out, g_ffn, w_router, b_router, w_gate_up, b_gate_up, w_down, b_down):
    return _layer_jit(x, g_mix[0], w_in[0], q_gain[0], k_gain[0], rel_bias, ssm_a_re[0], ssm_a_im[0],
                      ssm_log_dt[0], ssm_b_re[0], ssm_b_im[0], ssm_c_re[0], ssm_c_im[0], ssm_d[0],
                      w_glu[0], b_glu[0], w_attn_proj[0], w_ssm_proj[0], w_out[0], g_ffn[0],
                      w_router[0], b_router[0], w_gate_up[0], b_gate_up[0], w_down[0], b_down[0])
```

```python
import functools
import math

import jax
import jax.numpy as jnp
import numpy as np
from jax import lax
from jax.experimental import pallas as pl
from jax.experimental.pallas import tpu as pltpu

F32 = jnp.float32
BF16 = jnp.bfloat16

D_MODEL = 1024
HEAD_DIM = 64
ATTN_GROUPS = ((128, 1), (512, 4), (2048, 16))
HEADS_PER_GROUP = 8
GROUP_WIDTH = HEADS_PER_GROUP * HEAD_DIM
N_ATTN_HEADS = len(ATTN_GROUPS) * HEADS_PER_GROUP
QKV_WIDTH = 3 * N_ATTN_HEADS * HEAD_DIM
ATTN_BLOCK = 128
ATTN_STEP_BLOCKS = 8
REL_BUCKETS = 32
REL_MAX_DIST = 2048
SSM_WIDTH = 512
SSM_GROUP = 16
SSM_GROUPS = 32
SSM_STATE = 64
IN_WIDTH = QKV_WIDTH + SSM_WIDTH + 2 * D_MODEL
N_EXPERTS = 32
TOP_K = 4
D_FF = 1024
SWIGLU_LIMIT = 7.0
SWIGLU_ALPHA = 1.702
EPS = 1e-6

SUBLANES = 8
LANES = 128
MASK_VALUE = -1e30
VMEM_LIMIT = 56 * 1024 * 1024

QKV_GROUP_WIDTH = 3 * GROUP_WIDTH
REST_WIDTH = 2 * D_MODEL + SSM_WIDTH
GATE_A_BLOCK = 0
GATE_B_BLOCK = 1
U_BLOCK = 2 * D_MODEL // SSM_WIDTH

ROW_TILE = 256
MOE_TILE = 512


def _params(n_axes, vmem=VMEM_LIMIT):
    return pltpu.CompilerParams(dimension_semantics=("arbitrary",) * n_axes, vmem_limit_bytes=vmem)


def _inproj_kernel(x_ref, g_ref, w_ref, q0_ref, q1_ref, q2_ref, rest_ref, stage_ref):
    x = x_ref[0]
    tm = x.shape[0]
    ms = jnp.mean(x * x, axis=-1, keepdims=True)
    h = x * lax.rsqrt(ms + EPS) * g_ref[...]
    res = jnp.dot(h.astype(BF16), w_ref[...], preferred_element_type=F32)
    n_g = len(ATTN_GROUPS)
    n_blk = GROUP_WIDTH // LANES

    def part(kind, gi):
        start = (kind * n_g + gi) * GROUP_WIDTH
        return res[:, start:start + GROUP_WIDTH]

    for kind in range(3):
        q0_ref[0, 0, :, kind * GROUP_WIDTH:(kind + 1) * GROUP_WIDTH] = part(kind, 0)
    rest_ref[0, :, 0:2 * D_MODEL] = res[:, QKV_WIDTH + SSM_WIDTH:]
    rest_ref[0, :, 2 * D_MODEL:] = res[:, QKV_WIDTH:QKV_WIDTH + SSM_WIDTH]
    for out_ref, gi in ((q1_ref, 1), (q2_ref, 2)):
        dil = ATTN_GROUPS[gi][1]
        for kind in range(3):
            src = part(kind, gi)
            base = ((gi - 1) * 3 + kind) * n_blk
            for c in range(n_blk):
                stage_ref[base + c] = src[:, c * LANES:(c + 1) * LANES]
            for r in range(dil):
                for c in range(n_blk):
                    lane0 = kind * GROUP_WIDTH + c * LANES
                    out_ref[0, r, :, lane0:lane0 + LANES] = stage_ref[
                        base + c, pl.ds(r, tm // dil, stride=dil), :]


def _inproj(x, g, w_bf16):
    b, s, _ = x.shape
    tm = ROW_TILE
    dils = [d for _, d in ATTN_GROUPS]
    out_specs = [pl.BlockSpec((1, d, tm // d, QKV_GROUP_WIDTH), lambda bi, i: (bi, 0, i, 0)) for d in dils]
    out_shape = [jax.ShapeDtypeStruct((b, d, s // d, QKV_GROUP_WIDTH), F32) for d in dils]
    return pl.pallas_call(
        _inproj_kernel,
        grid=(b, s // tm),
        in_specs=[
            pl.BlockSpec((1, tm, D_MODEL), lambda bi, i: (bi, i, 0)),
            pl.BlockSpec((1, D_MODEL), lambda bi, i: (0, 0)),
            pl.BlockSpec((D_MODEL, IN_WIDTH), lambda bi, i: (0, 0), pipeline_mode=pl.Buffered(1)),
        ],
        out_specs=out_specs + [pl.BlockSpec((1, tm, REST_WIDTH), lambda bi, i: (bi, i, 0))],
        out_shape=out_shape + [jax.ShapeDtypeStruct((b, s, REST_WIDTH), F32)],
        scratch_shapes=[pltpu.VMEM((2 * QKV_GROUP_WIDTH // LANES, tm, LANES), F32)],
        compiler_params=_params(2),
        name="inproj",
    )(x, g.reshape(1, D_MODEL), w_bf16)


def _rel_bucket_np(dist):
    max_exact = REL_BUCKETS // 2
    d = np.maximum(dist, 1).astype(np.float32)
    large = max_exact + (np.log(d / max_exact) / math.log(REL_MAX_DIST / max_exact)
                         * (REL_BUCKETS - max_exact)).astype(np.int32)
    large = np.minimum(large, REL_BUCKETS - 1)
    return np.where(dist < max_exact, dist, large).astype(np.int32)


def _attn_bias(table, window, dilation):
    blk = ATTN_BLOCK
    span = window // dilation
    qi = np.arange(blk)[:, None]
    ki = np.arange(2 * blk)[None, :]
    steps = qi + blk - ki
    band = (steps >= 0) & (steps <= span)
    bucket = _rel_bucket_np(np.clip(steps, 0, None) * dilation)
    onehot = np.eye(REL_BUCKETS, dtype=np.float32)[bucket]
    bias = jnp.einsum('qkb,bh->hqk', onehot, table.astype(F32), precision=lax.Precision.HIGHEST)
    b0 = jnp.where(band[None], bias, MASK_VALUE)
    b1 = jnp.where((band & (ki >= blk))[None], bias, MASK_VALUE)
    return jnp.stack([b0, b1], axis=0)


def _attn_kernel(q_ref, k_ref, v_ref, bias_ref, qg_ref, kg_ref, seg_ref, o_ref, lse_ref,
                 qbuf_ref, kbuf_ref, vbuf_ref):
    blk = ATTN_BLOCK
    rows = q_ref.shape[2]
    i = pl.program_id(2)
    seg = seg_ref[...]

    def head_norm(t, gain):
        ss = jnp.dot((t * t).astype(BF16), seg, preferred_element_type=F32)
        return t * lax.rsqrt(ss * (1.0 / HEAD_DIM) + EPS) * gain

    qbuf_ref[...] = (head_norm(q_ref[0, 0], qg_ref[...]) * (HEAD_DIM ** -0.5)).astype(BF16)

    @pl.when(i == 0)
    def _():
        kbuf_ref[0:blk, :] = jnp.zeros((blk, GROUP_WIDTH), BF16)
        vbuf_ref[0:blk, :] = jnp.zeros((blk, GROUP_WIDTH), BF16)

    @pl.when(i > 0)
    def _():
        kbuf_ref[0:blk, :] = kbuf_ref[rows:rows + blk, :]
        vbuf_ref[0:blk, :] = vbuf_ref[rows:rows + blk, :]

    kbuf_ref[blk:blk + rows, :] = head_norm(k_ref[0, 0], kg_ref[...]).astype(BF16)
    vbuf_ref[blk:blk + rows, :] = v_ref[0, 0].astype(BF16)
    first = (i == 0).astype(jnp.int32)

    lane = lax.broadcasted_iota(jnp.int32, (blk, LANES), 1)
    low_half = lane < HEAD_DIM
    for j in range(rows // blk):
        variant = first if j == 0 else 0
        lse_tile = jnp.zeros((blk, LANES), F32)
        outs = []
        for p in range(GROUP_WIDTH // LANES):
            qp = qbuf_ref[j * blk:(j + 1) * blk, p * LANES:(p + 1) * LANES]
            kp = kbuf_ref[j * blk:(j + 2) * blk, p * LANES:(p + 1) * LANES]
            vp = vbuf_ref[j * blk:(j + 2) * blk, p * LANES:(p + 1) * LANES]
            pair = []
            for hh in range(2):
                head = 2 * p + hh
                sel = low_half if hh == 0 else jnp.logical_not(low_half)
                qm = jnp.where(sel, qp, jnp.zeros_like(qp))
                s = lax.dot_general(qm, kp, (((1,), (1,)), ((), ())), preferred_element_type=F32)
                s = s + bias_ref[variant, head]
                mx = jnp.max(s, axis=-1, keepdims=True)
                pe = jnp.exp(s - mx)
                den = jnp.sum(pe, axis=-1, keepdims=True)
                o = jnp.dot(pe.astype(BF16), vp, preferred_element_type=F32)
                pair.append(o * (1.0 / den))
                lse_tile = jnp.where(lane == head, mx + jnp.log(den), lse_tile)
            outs.append(jnp.where(low_half, pair[0], pair[1]))
        o_ref[0, 0, j * blk:(j + 1) * blk, :] = jnp.concatenate(outs, axis=-1)
        lse_ref[0, 0, j * blk:(j + 1) * blk, :] = lse_tile


def _attention_group(qkv, gi, bias, q_gain, k_gain, seg):
    b, dil, length, _ = qkv.shape
    blk = ATTN_BLOCK
    rows = min(ATTN_STEP_BLOCKS * blk, length)

    def col(c):
        return pl.BlockSpec((1, 1, rows, GROUP_WIDTH), lambda bi, r, i: (bi, r, i, c))

    const = lambda shape: pl.BlockSpec(shape, lambda bi, r, i: (0,) * len(shape))
    return pl.pallas_call(
        _attn_kernel,
        grid=(b, dil, length // rows),
        in_specs=[
            col(0), col(1), col(2),
            const((2, HEADS_PER_GROUP, blk, 2 * blk)),
            const((1, GROUP_WIDTH)),
            const((1, GROUP_WIDTH)),
            const((GROUP_WIDTH, GROUP_WIDTH)),
        ],
        out_specs=[
            pl.BlockSpec((1, 1, rows, GROUP_WIDTH), lambda bi, r, i: (bi, r, i, 0)),
            pl.BlockSpec((1, 1, rows, LANES), lambda bi, r, i: (bi, r, i, 0)),
        ],
        out_shape=[
            jax.ShapeDtypeStruct((b, dil, length, GROUP_WIDTH), F32),
            jax.ShapeDtypeStruct((b, dil, length, LANES), F32),
        ],
        scratch_shapes=[
            pltpu.VMEM((rows, GROUP_WIDTH), BF16),
            pltpu.VMEM((blk + rows, GROUP_WIDTH), BF16),
            pltpu.VMEM((blk + rows, GROUP_WIDTH), BF16),
        ],
        compiler_params=_params(3),
        name=f"attn_g{gi}",
    )(qkv, qkv, qkv, bias,
      jnp.tile(q_gain, HEADS_PER_GROUP).reshape(1, GROUP_WIDTH),
      jnp.tile(k_gain, HEADS_PER_GROUP).reshape(1, GROUP_WIDTH), seg)


HALF_STATE = SSM_GROUPS * SSM_STATE // 2
SCAN_COLS = 4
SSM_STEP = 64
TAB_STEP, TAB_CHUNK = 0, 1


def _ssm_tables(a_re, a_im, log_dt, b_re, b_im, c_re, c_im, chunk_len):
    lam = lax.complex(a_re.astype(F32), a_im.astype(F32))
    dt = jnp.exp(log_dt.astype(F32))[:, None]
    a_bar = jnp.exp(lam * dt)
    b_bar = ((a_bar - 1.0) / lam)[..., None] * lax.complex(b_re.astype(F32), b_im.astype(F32))

    tab = jnp.stack([a_bar.reshape(-1), jnp.exp(lam * dt * chunk_len).reshape(-1)])
    tab = jnp.stack([tab.real, tab.imag], axis=1).astype(F32)
    tab = jnp.broadcast_to(tab[:, :, None, :], (2, 2, SUBLANES, tab.shape[-1]))

    gh = SSM_GROUPS // 2
    eye = jnp.eye(gh, dtype=F32)
    b_parts = jnp.stack([b_bar.real, b_bar.imag]).reshape(2, 2, gh, SSM_STATE, SSM_GROUP)
    bmat = jnp.einsum('rhgnc,gk->hgcrkn', b_parts, eye).reshape(2, gh * SSM_GROUP, 2 * gh * SSM_STATE)
    c_parts = jnp.stack([c_re.astype(F32), -c_im.astype(F32)]).reshape(2, 2, gh, SSM_GROUP, SSM_STATE)
    cmat = jnp.einsum('rhgcn,gk->hrgnkc', c_parts, eye).reshape(2, 2 * gh * SSM_STATE, gh * SSM_GROUP)
    return tab, bmat.astype(BF16), cmat.astype(BF16)


def _ssm_columns():
    half_w = 2 * HALF_STATE
    cols = []
    for c in range(2 * HALF_STATE // LANES):
        h, k = divmod(c, HALF_STATE // LANES)
        cols.append((c * LANES, h * half_w + k * LANES, h * half_w + HALF_STATE + k * LANES))
    return cols


def _ssm_step_major(u_ref, stage_ref):
    n_rows, tj = u_ref.shape[0], u_ref.shape[1]
    n_lane_blk = SSM_WIDTH // LANES
    for r in range(n_rows):
        for c in range(n_lane_blk):
            stage_ref[c, pl.ds(r, tj, stride=SUBLANES), :] = u_ref[r, :, c * LANES:(c + 1) * LANES]
    return jnp.concatenate([stage_ref[c] for c in range(n_lane_blk)], axis=-1)


def _ssm_scan(tab_ref, bu_ref, state_ref, n_steps, states_ref=None):
    cols = _ssm_columns()
    for c0 in range(0, len(cols), SCAN_COLS):
        grp = cols[c0:c0 + SCAN_COLS]
        consts = [(tab_ref[TAB_STEP, 0, :, tc:tc + LANES], tab_ref[TAB_STEP, 1, :, tc:tc + LANES])
                  for tc, _, _ in grp]

        def body(j, st, grp=grp, consts=consts):
            at_j = pl.ds(pl.multiple_of(j * SUBLANES, SUBLANES), SUBLANES)
            new = []
            for (_, rc, ic), (ar, ai), (xr, xi) in zip(grp, consts, st):
                nr = ar * xr - ai * xi + bu_ref[at_j, rc:rc + LANES]
                ni = ar * xi + ai * xr + bu_ref[at_j, ic:ic + LANES]
                if states_ref is not None:
                    states_ref[at_j, rc:rc + LANES] = nr
                    states_ref[at_j, ic:ic + LANES] = ni
                new.append((nr, ni))
            return tuple(new)

        init = tuple((state_ref[:, rc:rc + LANES], state_ref[:, ic:ic + LANES]) for _, rc, ic in grp)
        fin = lax.fori_loop(0, n_steps, body, init)
        for (_, rc, ic), (xr, xi) in zip(grp, fin):
            state_ref[:, rc:rc + LANES] = xr
            state_ref[:, ic:ic + LANES] = xi


def _ssm_ends_kernel(u_ref, tab_ref, bmat_ref, bu_ref, ends_ref, state_ref, stage_ref):
    i = pl.program_id(0)
    tj = u_ref.shape[1]
    half_w = 2 * HALF_STATE

    @pl.when(i == 0)
    def _():
        state_ref[...] = jnp.zeros_like(state_ref)

    ub = _ssm_step_major(u_ref, stage_ref).astype(BF16)
    hw = SSM_WIDTH // 2
    for h in range(2):
        bu_ref[:, h * half_w:(h + 1) * half_w] = jnp.dot(
            ub[:, h * hw:(h + 1) * hw], bmat_ref[h], preferred_element_type=F32)
    _ssm_scan(tab_ref, bu_ref, state_ref, tj)
    ends_ref[...] = state_ref[...]


def _ssm_out_kernel(u_ref, bu_ref, ends_ref, tab_ref, cmat_ref, d_ref, y_ref, xs_ref, state_ref,
                    stage_ref, *, chunks):
    i = pl.program_id(0)
    n_rows, tj = u_ref.shape[0], u_ref.shape[1]
    half_w = 2 * HALF_STATE
    n_lane_blk = SSM_WIDTH // LANES

    @pl.when(i == 0)
    def _():
        row = lax.broadcasted_iota(jnp.int32, (SUBLANES, LANES), 0)
        seq_start = (row % chunks) == 0
        for tc, rc, ic in _ssm_columns():
            er, ei = ends_ref[:, rc:rc + LANES], ends_ref[:, ic:ic + LANES]
            pr = tab_ref[TAB_CHUNK, 0, :, tc:tc + LANES]
            pi = tab_ref[TAB_CHUNK, 1, :, tc:tc + LANES]
            xr = jnp.zeros_like(er)
            xi = jnp.zeros_like(ei)
            for _ in range(chunks - 1):
                nr = pr * xr - pi * xi + er
                ni = pr * xi + pi * xr + ei
                xr = jnp.where(seq_start, 0.0, pltpu.roll(nr, 1, 0))
                xi = jnp.where(seq_start, 0.0, pltpu.roll(ni, 1, 0))
            state_ref[:, rc:rc + LANES] = xr
            state_ref[:, ic:ic + LANES] = xi

    _ssm_scan(tab_ref, bu_ref, state_ref, tj, states_ref=xs_ref)
    ys = []
    for h in range(2):
        xh = xs_ref[:, h * half_w:(h + 1) * half_w].astype(BF16)
        ys.append(jnp.dot(xh, cmat_ref[h], preferred_element_type=F32))
    y = jnp.concatenate(ys, axis=-1) + d_ref[...] * _ssm_step_major(u_ref, stage_ref)
    for c in range(n_lane_blk):
        stage_ref[c] = y[:, c * LANES:(c + 1) * LANES]
    for r in range(n_rows):
        y_ref[r] = jnp.concatenate(
            [stage_ref[c, pl.ds(r, tj, stride=SUBLANES), :] for c in range(n_lane_blk)], axis=-1)


def _ssm(rest, a_re, a_im, log_dt, b_re, b_im, c_re, c_im, d_skip):
    b, s, _ = rest.shape
    chunks = SUBLANES // b
    chunk_len = s // chunks
    tj = min(SSM_STEP, chunk_len)
    n_steps = chunk_len // tj
    n_state = 4 * HALF_STATE
    tab, bmat, cmat = _ssm_tables(a_re, a_im, log_dt, b_re, b_im, c_re, c_im, chunk_len)
    u_rows = rest.reshape(SUBLANES, chunk_len, REST_WIDTH)
    u_spec = pl.BlockSpec((SUBLANES, tj, SSM_WIDTH), lambda j: (0, j, U_BLOCK))
    bu_spec = pl.BlockSpec((SUBLANES * tj, n_state), lambda j: (j, 0))
    whole = lambda a: pl.BlockSpec(a.shape, lambda j: (0,) * a.ndim)
    ends_shape = jax.ShapeDtypeStruct((SUBLANES, n_state), F32)
    stage = pltpu.VMEM((SSM_WIDTH // LANES, SUBLANES * tj, LANES), F32)
    bu, ends = pl.pallas_call(
        _ssm_ends_kernel,
        grid=(n_steps,),
        in_specs=[u_spec, whole(tab), whole(bmat)],
        out_specs=[bu_spec, pl.BlockSpec((SUBLANES, n_state), lambda j: (0, 0))],
        out_shape=[jax.ShapeDtypeStruct((n_steps * SUBLANES * tj, n_state), F32), ends_shape],
        scratch_shapes=[pltpu.VMEM((SUBLANES, n_state), F32), stage],
        compiler_params=_params(1),
        name="ssm_ends",
    )(u_rows, tab, bmat)
    d2 = d_skip.reshape(1, SSM_WIDTH)
    return pl.pallas_call(
        functools.partial(_ssm_out_kernel, chunks=chunks),
        grid=(n_steps,),
        in_specs=[u_spec, bu_spec, whole(ends), whole(tab), whole(cmat), whole(d2)],
        out_specs=pl.BlockSpec((SUBLANES, tj, SSM_WIDTH), lambda j: (0, j, 0)),
        out_shape=jax.ShapeDtypeStruct((SUBLANES, chunk_len, SSM_WIDTH), F32),
        scratch_shapes=[pltpu.VMEM((SUBLANES * tj, n_state), F32),
                        pltpu.VMEM((SUBLANES, n_state), F32), stage],
        compiler_params=_params(1),
        name="ssm_out",
    )(u_rows, bu, ends, tab, cmat, d2).reshape(b * s, SSM_WIDTH)


ROUTE_IDS, ROUTE_GATES, ROUTE_SLOTS = 0, TOP_K, 2 * TOP_K
RUN_START, RUN_COUNT, RUN_OFFSET = 0, 1, 2
SLOTS_PER_TILE = ROW_TILE * TOP_K


def _mix_kernel(o0_ref, o1_ref, o2_ref, l0_ref, l1_ref, l2_ref, ys_ref, ga_ref, gb_ref, x_ref,
                expand_ref, wa_ref, wglu_ref, bglu_ref, ws_ref, wo_ref, gffn_ref, wr_ref, br_ref,
                x1_ref, h2_ref, route_ref, counts_ref, runs_ref, run_ref, ostage_ref, lstage_ref):
    tm = x_ref.shape[0]

    @pl.when(jnp.logical_and(pl.program_id(0) == 0, pl.program_id(1) == 0))
    def _():
        run_ref[...] = jnp.zeros_like(run_ref)

    for slot, (o_ref, l_ref) in enumerate(((o1_ref, l1_ref), (o2_ref, l2_ref))):
        dil = ATTN_GROUPS[slot + 1][1]
        for r in range(dil):
            for c in range(GROUP_WIDTH // LANES):
                ostage_ref[slot, c, pl.ds(r, tm // dil, stride=dil), :] = o_ref[
                    0, r, :, c * LANES:(c + 1) * LANES]
            lstage_ref[slot, pl.ds(r, tm // dil, stride=dil), :] = l_ref[0, r]

    def sigmoid(z):
        return 0.5 * jnp.tanh(0.5 * z) + 0.5

    expand = expand_ref[...]
    n_lane_blk = GROUP_WIDTH // LANES

    def mix_rows(lo, n):
        rows = slice(lo, lo + n)
        group_out = [o0_ref[0, 0, rows, :]] + [
            jnp.concatenate([ostage_ref[slot, c, rows, :] for c in range(n_lane_blk)], axis=-1)
            for slot in range(2)]
        lses = [l0_ref[0, 0, rows, :], lstage_ref[0, rows, :], lstage_ref[1, rows, :]]
        mx = jnp.maximum(jnp.maximum(lses[0], lses[1]), lses[2])
        es = [jnp.exp(l - mx) for l in lses]
        inv = 1.0 / (es[0] + es[1] + es[2])
        attn = jnp.zeros((n, GROUP_WIDTH), F32)
        for e, o_g in zip(es, group_out):
            attn = attn + jnp.dot((e * inv).astype(BF16), expand, preferred_element_type=F32) * o_g
        y_a = jnp.dot(attn.astype(BF16), wa_ref[...], preferred_element_type=F32)

        ys = ys_ref[rows, :]
        ys = 0.5 * ys * (1.0 + jnp.tanh(math.sqrt(2.0 / math.pi) * (ys + 0.044715 * (ys * ys * ys))))
        glu = jnp.dot(ys.astype(BF16), wglu_ref[...], preferred_element_type=F32) + bglu_ref[...]
        ys = ys * sigmoid(glu)
        y_b = jnp.dot(ys.astype(BF16), ws_ref[...], preferred_element_type=F32)

        mixed = sigmoid(ga_ref[rows, :]) * y_a + sigmoid(gb_ref[rows, :]) * y_b
        x1 = x_ref[rows, :] + jnp.dot(mixed.astype(BF16), wo_ref[...], preferred_element_type=F32)
        x1_ref[rows, :] = x1

        ms = jnp.mean(x1 * x1, axis=-1, keepdims=True)
        h2b = (x1 * lax.rsqrt(ms + EPS) * gffn_ref[...]).astype(BF16)
        h2_ref[rows, :] = h2b
        return jnp.dot(h2b, wr_ref[...], preferred_element_type=F32) + br_ref[...]

    logits = mix_rows(0, tm)

    lane = lax.broadcasted_iota(jnp.int32, (tm, LANES), 1)
    lane_f = lane.astype(F32)
    work = jnp.where(lane < N_EXPERTS, logits, -jnp.inf)
    sel_mask = jnp.zeros((tm, LANES), F32)
    route = jnp.zeros((tm, LANES), F32)
    vals, hots = [], []
    for k in range(TOP_K):
        v = jnp.max(work, axis=-1, keepdims=True)
        idx = jnp.min(jnp.where(work == v, lane_f, float(LANES)), axis=-1, keepdims=True)
        hot = lane_f == idx
        work = jnp.where(hot, -jnp.inf, work)
        sel_mask = jnp.where(hot, 1.0, sel_mask)
        route = jnp.where(lane == ROUTE_IDS + k, idx, route)
        vals.append(v)
        hots.append(hot)
    exps = [jnp.exp(v - vals[0]) for v in vals]
    inv_den = 1.0 / (exps[0] + exps[1] + exps[2] + exps[3])

    r_i = lax.broadcasted_iota(jnp.int32, (tm, tm), 0)
    c_i = lax.broadcasted_iota(jnp.int32, (tm, tm), 1)
    tri = jnp.where(c_i < r_i, 1.0, 0.0).astype(BF16)
    local = jnp.dot(tri, sel_mask.astype(BF16), preferred_element_type=F32)
    count = jnp.sum(sel_mask, axis=0, keepdims=True)
    e_r = lax.broadcasted_iota(jnp.int32, (LANES, LANES), 0)
    e_c = lax.broadcasted_iota(jnp.int32, (LANES, LANES), 1)
    upper = jnp.where(e_r < e_c, 1.0, 0.0).astype(BF16)
    offset = jnp.dot(jnp.broadcast_to(count, (SUBLANES, LANES)).astype(BF16), upper,
                     preferred_element_type=F32)
    slot_of = local + offset[0:1, :]
    for k in range(TOP_K):
        slot = jnp.sum(jnp.where(hots[k], slot_of, 0.0), axis=-1, keepdims=True)
        route = jnp.where(lane == ROUTE_GATES + k, exps[k] * inv_den, route)
        route = jnp.where(lane == ROUTE_SLOTS + k, slot, route)
    route_ref[...] = route
    row8 = lax.broadcasted_iota(jnp.int32, (SUBLANES, LANES), 0)
    runs = jnp.where(row8 == RUN_START, run_ref[...],
                     jnp.where(row8 == RUN_COUNT, jnp.broadcast_to(count, (SUBLANES, LANES)),
                               jnp.where(row8 == RUN_OFFSET, offset, 0.0)))
    runs_ref[0] = runs
    total = run_ref[0:1, :] + count
    run_ref[...] = jnp.broadcast_to(total, run_ref.shape)
    counts_ref[...] = jnp.broadcast_to(total, counts_ref.shape)


def _mix(os_, lses, ys, rest, x2d, w):
    b, s, _ = rest.shape
    t = b * s
    tm = ROW_TILE
    n_i = s // tm
    rest2d = rest.reshape(t, REST_WIDTH)
    row = lambda width, blk=0: pl.BlockSpec((tm, width), lambda bi, i: (bi * n_i + i, blk))
    const = lambda a: pl.BlockSpec(a.shape, lambda bi, i: (0,) * a.ndim)
    grouped = lambda width: [pl.BlockSpec((1, d, tm // d, width), lambda bi, i: (bi, 0, i, 0))
                             for _, d in ATTN_GROUPS]
    consts = [w["expand"], w["wa"], w["wglu"], w["bglu"], w["ws"], w["wo"], w["gffn"], w["wr"], w["br"]]
    n_dilated = len(ATTN_GROUPS) - 1
    return pl.pallas_call(
        _mix_kernel,
        grid=(b, n_i),
        in_specs=grouped(GROUP_WIDTH) + grouped(LANES) + [
            row(SSM_WIDTH),
            row(D_MODEL, GATE_A_BLOCK),
            row(D_MODEL, GATE_B_BLOCK),
            row(D_MODEL),
        ] + [const(a) for a in consts],
        out_specs=[
            row(D_MODEL),
            row(D_MODEL),
            row(LANES),
            pl.BlockSpec((SUBLANES, LANES), lambda bi, i: (0, 0)),
            pl.BlockSpec((1, SUBLANES, LANES), lambda bi, i: (bi * n_i + i, 0, 0)),
        ],
        out_shape=[
            jax.ShapeDtypeStruct((t, D_MODEL), F32),
            jax.ShapeDtypeStruct((t, D_MODEL), BF16),
            jax.ShapeDtypeStruct((t, LANES), F32),
            jax.ShapeDtypeStruct((SUBLANES, LANES), F32),
            jax.ShapeDtypeStruct((t // tm, SUBLANES, LANES), F32),
        ],
        scratch_shapes=[
            pltpu.VMEM((SUBLANES, LANES), F32),
            pltpu.VMEM((n_dilated, GROUP_WIDTH // LANES, tm, LANES), F32),
            pltpu.VMEM((n_dilated, tm, LANES), F32),
        ],
        compiler_params=_params(2),
        name="mix_router",
    )(*os_, *lses, ys, rest2d, rest2d, x2d, *consts)


def _n_moe_tiles(t):
    return t * TOP_K // MOE_TILE + N_EXPERTS


def _copy_run(src_ref, dst_ref, src_row, dst_row, n_rows, sem):
    size = ROW_TILE
    priority = 0
    while size >= 1:
        @pl.when((n_rows & size) != 0)
        def _(size=size, src_row=src_row, dst_row=dst_row, priority=priority):
            pltpu.make_async_copy(
                src_ref.at[pl.ds(pl.multiple_of(src_row * SUBLANES, SUBLANES), size * SUBLANES), :],
                dst_ref.at[pl.ds(pl.multiple_of(dst_row * SUBLANES, SUBLANES), size * SUBLANES), :],
                sem).start(priority=priority)
        step = n_rows & size
        src_row = src_row + step
        dst_row = dst_row + step
        size //= 2
        priority = 1 - priority


def _wait_tile(hbm_ref, vmem_ref, sem):
    pltpu.make_async_copy(hbm_ref.at[pl.ds(0, SLOTS_PER_TILE * SUBLANES), :], vmem_ref, sem).wait()


def _dispatch_kernel(counts_ref, start_ref, cnt_ref, toff_ref, h2_ref, route_ref,
                     xs_ref, texp_ref, off_ref, stage_ref, sems):
    i = pl.program_id(0)
    n_steps = pl.num_programs(0)
    n_tiles = texp_ref.shape[0] - 1

    @pl.when(i == 0)
    def _():
        def per_expert(e, start):
            off_ref[e] = start
            n_t = (counts_ref[e] + (MOE_TILE - 1)) // MOE_TILE

            def mark(ti, c):
                texp_ref[start // MOE_TILE + ti] = e
                return c

            lax.fori_loop(0, n_t, mark, 0)
            return start + n_t * MOE_TILE

        end = lax.fori_loop(0, N_EXPERTS, per_expert, 0)
        n_active = end // MOE_TILE
        texp_ref[n_tiles] = n_active

        def fill(ti, c):
            texp_ref[ti] = texp_ref[n_active - 1]
            return c

        lax.fori_loop(n_active, n_tiles, fill, 0)

    slots_t = route_ref[...].T
    slot_id = lax.broadcasted_iota(jnp.int32, (SLOTS_PER_TILE, ROW_TILE), 0).astype(F32)
    perm = jnp.zeros((SLOTS_PER_TILE, ROW_TILE), F32)
    for k in range(TOP_K):
        perm = jnp.where(slot_id == slots_t[ROUTE_SLOTS + k:ROUTE_SLOTS + k + 1, :], 1.0, perm)
    srt = jnp.dot(perm.astype(BF16), h2_ref[...], preferred_element_type=F32)
    buf = i % 2
    for s in range(D_MODEL // LANES):
        stage_ref[buf, pl.ds(s, SLOTS_PER_TILE, stride=SUBLANES), :] = srt[:, s * LANES:(s + 1) * LANES]

    @pl.when(i > 0)
    def _():
        _wait_tile(xs_ref, stage_ref.at[1 - buf], sems.at[1 - buf])

    def per_run(e, c):
        r = i * N_EXPERTS + e
        _copy_run(stage_ref.at[buf], xs_ref, toff_ref[r], off_ref[e] + start_ref[r], cnt_ref[r],
                  sems.at[buf])
        return c

    lax.fori_loop(0, N_EXPERTS, per_run, 0)

    @pl.when(i == n_steps - 1)
    def _():
        _wait_tile(xs_ref, stage_ref.at[buf], sems.at[buf])


def _dispatch(counts, runs, h2, route):
    t = h2.shape[0]
    tm = ROW_TILE
    n_tiles = _n_moe_tiles(t)
    grid_spec = pltpu.PrefetchScalarGridSpec(
        num_scalar_prefetch=4,
        grid=(t // tm,),
        in_specs=[
            pl.BlockSpec((tm, D_MODEL), lambda i, *_: (i, 0)),
            pl.BlockSpec((tm, LANES), lambda i, *_: (i, 0)),
        ],
        out_specs=[
            pl.BlockSpec(memory_space=pl.ANY),
            pl.BlockSpec((n_tiles + 1,), lambda i, *_: (0,), memory_space=pltpu.SMEM),
            pl.BlockSpec((N_EXPERTS,), lambda i, *_: (0,), memory_space=pltpu.SMEM),
        ],
        scratch_shapes=[
            pltpu.VMEM((2, SLOTS_PER_TILE * SUBLANES, LANES), F32),
            pltpu.SemaphoreType.DMA((2,)),
        ],
    )
    return pl.pallas_call(
        _dispatch_kernel,
        grid_spec=grid_spec,
        out_shape=[
            jax.ShapeDtypeStruct((n_tiles * MOE_TILE * SUBLANES, LANES), F32),
            jax.ShapeDtypeStruct((n_tiles + 1,), jnp.int32),
            jax.ShapeDtypeStruct((N_EXPERTS,), jnp.int32),
        ],
        compiler_params=_params(1),
        name="moe_dispatch",
    )(counts, *runs, h2, route)


def _ffn_kernel(texp_ref, xs_ref, wgu_ref, bgu_ref, wd_ref, bd_ref, out_ref, wgu_bf, wd_bf, last_ref):
    i = pl.program_id(0)
    tm = MOE_TILE
    n_active = texp_ref[pl.num_programs(0)]
    e = texp_ref[i]

    @pl.when(i == 0)
    def _():
        last_ref[0] = -1

    @pl.when(jnp.logical_and(i < n_active, e != last_ref[0]))
    def _():
        wgu_bf[...] = wgu_ref[0].astype(BF16)
        wd_bf[...] = wd_ref[0].astype(BF16)
        last_ref[0] = e

    @pl.when(i < n_active)
    def _():
        x = jnp.concatenate(
            [xs_ref[pl.ds(s, tm, stride=SUBLANES), :] for s in range(D_MODEL // LANES)], axis=-1)
        gu = jnp.dot(x.astype(BF16), wgu_bf[...], preferred_element_type=F32) + bgu_ref[0]
        x_glu = jnp.minimum(gu[:, :D_FF], SWIGLU_LIMIT)
        x_lin = jnp.clip(gu[:, D_FF:], -SWIGLU_LIMIT, SWIGLU_LIMIT)
        act = x_glu * jax.nn.sigmoid(SWIGLU_ALPHA * x_glu) * (x_lin + 1.0)
        out = jnp.dot(act.astype(BF16), wd_bf[...], preferred_element_type=F32) + bd_ref[0]
        for s in range(D_MODEL // LANES):
            out_ref[pl.ds(s, tm, stride=SUBLANES), :] = out[:, s * LANES:(s + 1) * LANES]


def _expert_ffn(texp, xs_tiles, w_gu, b_gu, w_down, b_down):
    n_tiles = texp.shape[0] - 1
    tm = MOE_TILE
    grid_spec = pltpu.PrefetchScalarGridSpec(
        num_scalar_prefetch=1,
        grid=(n_tiles,),
        in_specs=[
            pl.BlockSpec((tm * SUBLANES, LANES), lambda i, te: (jnp.minimum(i, te[n_tiles] - 1), 0)),
            pl.BlockSpec((1, D_MODEL, 2 * D_FF), lambda i, te: (te[i], 0, 0)),
            pl.BlockSpec((1, 1, 2 * D_FF), lambda i, te: (te[i], 0, 0)),
            pl.BlockSpec((1, D_FF, D_MODEL), lambda i, te: (te[i], 0, 0)),
            pl.BlockSpec((1, 1, D_MODEL), lambda i, te: (te[i], 0, 0)),
        ],
        out_specs=pl.BlockSpec((tm * SUBLANES, LANES), lambda i, te: (i, 0)),
        scratch_shapes=[
            pltpu.VMEM((D_MODEL, 2 * D_FF), BF16),
            pltpu.VMEM((D_FF, D_MODEL), BF16),
            pltpu.SMEM((1,), jnp.int32),
        ],
    )
    return pl.pallas_call(
        _ffn_kernel,
        grid_spec=grid_spec,
        out_shape=jax.ShapeDtypeStruct((n_tiles * tm * SUBLANES, LANES), F32),
        compiler_params=_params(1),
        name="moe_ffn",
    )(texp, xs_tiles, w_gu, b_gu.reshape(N_EXPERTS, 1, 2 * D_FF), w_down,
      b_down.reshape(N_EXPERTS, 1, D_MODEL))


def _combine_kernel(off_ref, start_ref, cnt_ref, toff_ref, route_ref, x1_ref, ys_ref, out_ref,
                    buf_ref, sems):
    i = pl.program_id(0)
    n_steps = pl.num_programs(0)
    tm = x1_ref.shape[0]

    def fetch(tile, buf):
        def per_run(e, c):
            r = tile * N_EXPERTS + e
            _copy_run(ys_ref, buf_ref.at[buf], off_ref[e] + start_ref[r], toff_ref[r], cnt_ref[r],
                      sems.at[buf])
            return c

        lax.fori_loop(0, N_EXPERTS, per_run, 0)

    @pl.when(i == 0)
    def _():
        fetch(0, 0)

    @pl.when(i + 1 < n_steps)
    def _():
        fetch(i + 1, (i + 1) % 2)

    buf = i % 2
    _wait_tile(ys_ref, buf_ref.at[buf], sems.at[buf])

    rows = jnp.concatenate(
        [buf_ref[buf, pl.ds(s, SLOTS_PER_TILE, stride=SUBLANES), :] for s in range(D_MODEL // LANES)],
        axis=-1).astype(BF16)
    route = route_ref[...]
    slot_id = lax.broadcasted_iota(jnp.int32, (tm, SLOTS_PER_TILE), 1).astype(F32)
    weights = jnp.zeros((tm, SLOTS_PER_TILE), F32)
    for k in range(TOP_K):
        slot = route[:, ROUTE_SLOTS + k:ROUTE_SLOTS + k + 1]
        gate = route[:, ROUTE_GATES + k:ROUTE_GATES + k + 1]
        weights = jnp.where(slot_id == slot, gate, weights)
    out_ref[...] = x1_ref[...] + jnp.dot(weights.astype(BF16), rows, preferred_element_type=F32)


def _combine(off, runs, route, x1, ys_tiles):
    t = x1.shape[0]
    tm = ROW_TILE
    grid_spec = pltpu.PrefetchScalarGridSpec(
        num_scalar_prefetch=4,
        grid=(t // tm,),
        in_specs=[
            pl.BlockSpec((tm, LANES), lambda i, *_: (i, 0)),
            pl.BlockSpec((tm, D_MODEL), lambda i, *_: (i, 0)),
            pl.BlockSpec(memory_space=pl.ANY),
        ],
        out_specs=pl.BlockSpec((tm, D_MODEL), lambda i, *_: (i, 0)),
        scratch_shapes=[
            pltpu.VMEM((2, SLOTS_PER_TILE * SUBLANES, LANES), F32),
            pltpu.SemaphoreType.DMA((2,)),
        ],
    )
    return pl.pallas_call(
        _combine_kernel,
        grid_spec=grid_spec,
        out_shape=jax.ShapeDtypeStruct((t, D_MODEL), F32),
        compiler_params=_params(1),
        name="moe_combine",
    )(off, *runs, route, x1, ys_tiles)


def _layer(x, g_mix, w_in, q_gain, k_gain, rel_bias, ssm_a_re, ssm_a_im, ssm_log_dt, ssm_b_re,
           ssm_b_im, ssm_c_re, ssm_c_im, ssm_d, w_glu, b_glu, w_attn_proj, w_ssm_proj, w_out,
           g_ffn, w_router, b_router, w_gate_up, b_gate_up, w_down, b_down):
    b, s, _ = x.shape
    x2d = x.reshape(b * s, D_MODEL)

    *qkvs, rest = _inproj(x, g_mix, w_in.astype(BF16))

    head_of = np.arange(GROUP_WIDTH) // HEAD_DIM
    seg = jnp.asarray(head_of[:, None] == head_of[None, :], BF16)
    os_, lses = [], []
    for gi, (window, dil) in enumerate(ATTN_GROUPS):
        table = rel_bias[:, gi * HEADS_PER_GROUP:(gi + 1) * HEADS_PER_GROUP]
        o, lse = _attention_group(qkvs[gi], gi, _attn_bias(table, window, dil), q_gain[gi], k_gain[gi], seg)
        os_.append(o)
        lses.append(lse)

    ys = _ssm(rest, ssm_a_re, ssm_a_im, ssm_log_dt, ssm_b_re, ssm_b_im, ssm_c_re, ssm_c_im, ssm_d)

    expand = jnp.asarray(np.arange(LANES)[:, None] == head_of[None, :], BF16)
    pad_e = LANES - N_EXPERTS
    weights = dict(
        expand=expand, wa=w_attn_proj.astype(BF16), wglu=w_glu.astype(BF16),
        bglu=b_glu.reshape(1, SSM_WIDTH), ws=w_ssm_proj.astype(BF16), wo=w_out.astype(BF16),
        gffn=g_ffn.reshape(1, D_MODEL), wr=jnp.pad(w_router, ((0, 0), (0, pad_e))).astype(BF16),
        br=jnp.pad(b_router, (0, pad_e)).reshape(1, LANES))
    x1, h2, route, counts, run_tab = _mix(os_, lses, ys, rest, x2d, weights)

    counts_i = counts[0, :N_EXPERTS].astype(jnp.int32)
    run_table = run_tab[:, :, :N_EXPERTS].astype(jnp.int32)
    runs = [run_table[:, row].reshape(-1) for row in (RUN_START, RUN_COUNT, RUN_OFFSET)]
    xs_tiles, texp, off = _dispatch(counts_i, runs, h2, route)
    ys_tiles = _expert_ffn(texp, xs_tiles, w_gate_up, b_gate_up, w_down, b_down)
    out = _combine(off, runs, route, x1, ys_tiles)
    return out.reshape(b, s, D_MODEL)


_layer_jit = jax.jit(_layer)


def kernel(x, g_mix, w_in, q_gain, k_gain, rel_bias, ssm_a_re, ssm_a_im, ssm_log_dt, ssm_b_re, ssm_b_im, ssm_c_re, ssm_c_im, ssm_d, w_glu, b_glu, w_attn_proj, w_ssm_proj, w_out, g_ffn, w_router, b_router, w_gate_up, b_gate_up, w_down, b_down):
    return _layer_jit(x, g_mix[0], w_in[0], q_gain[0], k_gain[0], rel_bias, ssm_a_re[0], ssm_a_im[0],
                      ssm_log_dt[0], ssm_b_re[0], ssm_b_im[0], ssm_c_re[0], ssm_c_im[0], ssm_d[0],
                      w_glu[0], b_glu[0], w_attn_proj[0], w_ssm_proj[0], w_out[0], g_ffn[0],
                      w_router[0], b_router[0], w_gate_up[0], b_gate_up[0], w_down[0], b_down[0])
```

```python
import functools
import math

import jax
import jax.numpy as jnp
import numpy as np
from jax import lax
from jax.experimental import pallas as pl
from jax.experimental.pallas import tpu as pltpu

F32 = jnp.float32
BF16 = jnp.bfloat16

D_MODEL = 1024
HEAD_DIM = 64
ATTN_GROUPS = ((128, 1), (512, 4), (2048, 16))
HEADS_PER_GROUP = 8
GROUP_WIDTH = HEADS_PER_GROUP * HEAD_DIM
N_ATTN_HEADS = len(ATTN_GROUPS) * HEADS_PER_GROUP
QKV_WIDTH = 3 * N_ATTN_HEADS * HEAD_DIM
ATTN_BLOCK = 128
ATTN_STEP_BLOCKS = 8
REL_BUCKETS = 32
REL_MAX_DIST = 2048
SSM_WIDTH = 512
SSM_GROUP = 16
SSM_GROUPS = 32
SSM_STATE = 64
IN_WIDTH = QKV_WIDTH + SSM_WIDTH + 2 * D_MODEL
N_EXPERTS = 32
TOP_K = 4
D_FF = 1024
SWIGLU_LIMIT = 7.0
SWIGLU_ALPHA = 1.702
EPS = 1e-6

SUBLANES = 8
LANES = 128
MASK_VALUE = -1e30
VMEM_LIMIT = 56 * 1024 * 1024

QKV_GROUP_WIDTH = 3 * GROUP_WIDTH
REST_WIDTH = 2 * D_MODEL + SSM_WIDTH
GATE_A_BLOCK = 0
GATE_B_BLOCK = 1
U_BLOCK = 2 * D_MODEL // SSM_WIDTH

ROW_TILE = 256
MOE_TILE = 512


def _params(n_axes, vmem=VMEM_LIMIT):
    return pltpu.CompilerParams(dimension_semantics=("arbitrary",) * n_axes, vmem_limit_bytes=vmem)


def _inproj_kernel(x_ref, g_ref, w_ref, q0_ref, q1_ref, q2_ref, rest_ref, stage_ref):
    x = x_ref[0]
    tm = x.shape[0]
    ms = jnp.mean(x * x, axis=-1, keepdims=True)
    h = x * lax.rsqrt(ms + EPS) * g_ref[...]
    res = jnp.dot(h.astype(BF16), w_ref[...], preferred_element_type=F32)
    n_g = len(ATTN_GROUPS)
    n_blk = GROUP_WIDTH // LANES

    def part(kind, gi):
        start = (kind * n_g + gi) * GROUP_WIDTH
        return res[:, start:start + GROUP_WIDTH]

    for kind in range(3):
        q0_ref[0, 0, :, kind * GROUP_WIDTH:(kind + 1) * GROUP_WIDTH] = part(kind, 0)
    rest_ref[0, :, 0:2 * D_MODEL] = res[:, QKV_WIDTH + SSM_WIDTH:]
    rest_ref[0, :, 2 * D_MODEL:] = res[:, QKV_WIDTH:QKV_WIDTH + SSM_WIDTH]
    for out_ref, gi in ((q1_ref, 1), (q2_ref, 2)):
        dil = ATTN_GROUPS[gi][1]
        for kind in range(3):
            src = part(kind, gi)
            base = ((gi - 1) * 3 + kind) * n_blk
            for c in range(n_blk):
                stage_ref[base + c] = src[:, c * LANES:(c + 1) * LANES]
            for r in range(dil):
                for c in range(n_blk):
                    lane0 = kind * GROUP_WIDTH + c * LANES
                    out_ref[0, r, :, lane0:lane0 + LANES] = stage_ref[
                        base + c, pl.ds(r, tm // dil, stride=dil), :]


def _inproj(x, g, w_bf16):
    b, s, _ = x.shape
    tm = ROW_TILE
    dils = [d for _, d in ATTN_GROUPS]
    out_specs = [pl.BlockSpec((1, d, tm // d, QKV_GROUP_WIDTH), lambda bi, i: (bi, 0, i, 0)) for d in dils]
    out_shape = [jax.ShapeDtypeStruct((b, d, s // d, QKV_GROUP_WIDTH), F32) for d in dils]
    return pl.pallas_call(
        _inproj_kernel,
        grid=(b, s // tm),
        in_specs=[
            pl.BlockSpec((1, tm, D_MODEL), lambda bi, i: (bi, i, 0)),
            pl.BlockSpec((1, D_MODEL), lambda bi, i: (0, 0)),
            pl.BlockSpec((D_MODEL, IN_WIDTH), lambda bi, i: (0, 0), pipeline_mode=pl.Buffered(1)),
        ],
        out_specs=out_specs + [pl.BlockSpec((1, tm, REST_WIDTH), lambda bi, i: (bi, i, 0))],
        out_shape=out_shape + [jax.ShapeDtypeStruct((b, s, REST_WIDTH), F32)],
        scratch_shapes=[pltpu.VMEM((2 * QKV_GROUP_WIDTH // LANES, tm, LANES), F32)],
        compiler_params=_params(2),
        name="inproj",
    )(x, g.reshape(1, D_MODEL), w_bf16)


def _rel_bucket_np(dist):
    max_exact = REL_BUCKETS // 2
    d = np.maximum(dist, 1).astype(np.float32)
    large = max_exact + (np.log(d / max_exact) / math.log(REL_MAX_DIST / max_exact)
                         * (REL_BUCKETS - max_exact)).astype(np.int32)
    large = np.minimum(large, REL_BUCKETS - 1)
    return np.where(dist < max_exact, dist, large).astype(np.int32)


def _attn_bias(table, window, dilation):
    blk = ATTN_BLOCK
    span = window // dilation
    qi = np.arange(blk)[:, None]
    ki = np.arange(2 * blk)[None, :]
    steps = qi + blk - ki
    band = (steps >= 0) & (steps <= span)
    bucket = _rel_bucket_np(np.clip(steps, 0, None) * dilation)
    onehot = np.eye(REL_BUCKETS, dtype=np.float32)[bucket]
    bias = jnp.einsum('qkb,bh->hqk', onehot, table.astype(F32), precision=lax.Precision.HIGHEST)
    b0 = jnp.where(band[None], bias, MASK_VALUE)
    b1 = jnp.where((band & (ki >= blk))[None], bias, MASK_VALUE)
    return jnp.stack([b0, b1], axis=0)


def _attn_kernel(q_ref, k_ref, v_ref, bias_ref, qg_ref, kg_ref, seg_ref, o_ref, lse_ref,
                 qbuf_ref, kbuf_ref, vbuf_ref):
    blk = ATTN_BLOCK
    rows = q_ref.shape[2]
    i = pl.program_id(2)
    seg = seg_ref[...]

    def head_norm(t, gain):
        ss = jnp.dot((t * t).astype(BF16), seg, preferred_element_type=F32)
        return t * lax.rsqrt(ss * (1.0 / HEAD_DIM) + EPS) * gain

    qbuf_ref[...] = (head_norm(q_ref[0, 0], qg_ref[...]) * (HEAD_DIM ** -0.5)).astype(BF16)

    @pl.when(i == 0)
    def _():
        kbuf_ref[0:blk, :] = jnp.zeros((blk, GROUP_WIDTH), BF16)
        vbuf_ref[0:blk, :] = jnp.zeros((blk, GROUP_WIDTH), BF16)

    @pl.when(i > 0)
    def _():
        kbuf_ref[0:blk, :] = kbuf_ref[rows:rows + blk, :]
        vbuf_ref[0:blk, :] = vbuf_ref[rows:rows + blk, :]

    kbuf_ref[blk:blk + rows, :] = head_norm(k_ref[0, 0], kg_ref[...]).astype(BF16)
    vbuf_ref[blk:blk + rows, :] = v_ref[0, 0].astype(BF16)
    first = (i == 0).astype(jnp.int32)

    lane = lax.broadcasted_iota(jnp.int32, (blk, LANES), 1)
    low_half = lane < HEAD_DIM
    for j in range(rows // blk):
        variant = first if j == 0 else 0
        lse_tile = jnp.zeros((blk, LANES), F32)
        outs = []
        for p in range(GROUP_WIDTH // LANES):
            qp = qbuf_ref[j * blk:(j + 1) * blk, p * LANES:(p + 1) * LANES]
            kp = kbuf_ref[j * blk:(j + 2) * blk, p * LANES:(p + 1) * LANES]
            vp = vbuf_ref[j * blk:(j + 2) * blk, p * LANES:(p + 1) * LANES]
            pair = []
            for hh in range(2):
                head = 2 * p + hh
                sel = low_half if hh == 0 else jnp.logical_not(low_half)
                qm = jnp.where(sel, qp, jnp.zeros_like(qp))
                s = lax.dot_general(qm, kp, (((1,), (1,)), ((), ())), preferred_element_type=F32)
                s = s + bias_ref[variant, head]
                mx = jnp.max(s, axis=-1, keepdims=True)
                pe = jnp.exp(s - mx)
                den = jnp.sum(pe, axis=-1, keepdims=True)
                o = jnp.dot(pe.astype(BF16), vp, preferred_element_type=F32)
                pair.append(o * (1.0 / den))
                lse_tile = jnp.where(lane == head, mx + jnp.log(den), lse_tile)
            outs.append(jnp.where(low_half, pair[0], pair[1]))
        o_ref[0, 0, j * blk:(j + 1) * blk, :] = jnp.concatenate(outs, axis=-1)
        lse_ref[0, 0, j * blk:(j + 1) * blk, :] = lse_tile


def _attention_group(qkv, gi, bias, q_gain, k_gain, seg):
    b, dil, length, _ = qkv.shape
    blk = ATTN_BLOCK
    rows = min(ATTN_STEP_BLOCKS * blk, length)

    def col(c):
        return pl.BlockSpec((1, 1, rows, GROUP_WIDTH), lambda bi, r, i: (bi, r, i, c))

    const = lambda shape: pl.BlockSpec(shape, lambda bi, r, i: (0,) * len(shape))
    return pl.pallas_call(
        _attn_kernel,
        grid=(b, dil, length // rows),
        in_specs=[
            col(0), col(1), col(2),
            const((2, HEADS_PER_GROUP, blk, 2 * blk)),
            const((1, GROUP_WIDTH)),
            const((1, GROUP_WIDTH)),
            const((GROUP_WIDTH, GROUP_WIDTH)),
        ],
        out_specs=[
            pl.BlockSpec((1, 1, rows, GROUP_WIDTH), lambda bi, r, i: (bi, r, i, 0)),
            pl.BlockSpec((1, 1, rows, LANES), lambda bi, r, i: (bi, r, i, 0)),
        ],
        out_shape=[
            jax.ShapeDtypeStruct((b, dil, length, GROUP_WIDTH), F32),
            jax.ShapeDtypeStruct((b, dil, length, LANES), F32),
        ],
        scratch_shapes=[
            pltpu.VMEM((rows, GROUP_WIDTH), BF16),
            pltpu.VMEM((blk + rows, GROUP_WIDTH), BF16),
            pltpu.VMEM((blk + rows, GROUP_WIDTH), BF16),
        ],
        compiler_params=_params(3),
        name=f"attn_g{gi}",
    )(qkv, qkv, qkv, bias,
      jnp.tile(q_gain, HEADS_PER_GROUP).reshape(1, GROUP_WIDTH),
      jnp.tile(k_gain, HEADS_PER_GROUP).reshape(1, GROUP_WIDTH), seg)


HALF_STATE = SSM_GROUPS * SSM_STATE // 2
SCAN_COLS = 4
SSM_STEP = 64
TAB_STEP, TAB_CHUNK = 0, 1


def _ssm_tables(a_re, a_im, log_dt, b_re, b_im, c_re, c_im, chunk_len):
    lam = lax.complex(a_re.astype(F32), a_im.astype(F32))
    dt = jnp.exp(log_dt.astype(F32))[:, None]
    a_bar = jnp.exp(lam * dt)
    b_bar = ((a_bar - 1.0) / lam)[..., None] * lax.complex(b_re.astype(F32), b_im.astype(F32))

    tab = jnp.stack([a_bar.reshape(-1), jnp.exp(lam * dt * chunk_len).reshape(-1)])
    tab = jnp.stack([tab.real, tab.imag], axis=1).astype(F32)
    tab = jnp.broadcast_to(tab[:, :, None, :], (2, 2, SUBLANES, tab.shape[-1]))

    gh = SSM_GROUPS // 2
    eye = jnp.eye(gh, dtype=F32)
    b_parts = jnp.stack([b_bar.real, b_bar.imag]).reshape(2, 2, gh, SSM_STATE, SSM_GROUP)
    bmat = jnp.einsum('rhgnc,gk->hgcrkn', b_parts, eye).reshape(2, gh * SSM_GROUP, 2 * gh * SSM_STATE)
    c_parts = jnp.stack([c_re.astype(F32), -c_im.astype(F32)]).reshape(2, 2, gh, SSM_GROUP, SSM_STATE)
    cmat = jnp.einsum('rhgcn,gk->hrgnkc', c_parts, eye).reshape(2, 2 * gh * SSM_STATE, gh * SSM_GROUP)
    return tab, bmat.astype(BF16), cmat.astype(BF16)


def _ssm_columns():
    half_w = 2 * HALF_STATE
    cols = []
    for c in range(2 * HALF_STATE // LANES):
        h, k = divmod(c, HALF_STATE // LANES)
        cols.append((c * LANES, h * half_w + k * LANES, h * half_w + HALF_STATE + k * LANES))
    return cols


def _ssm_step_major(u_ref, stage_ref):
    n_rows, tj = u_ref.shape[0], u_ref.shape[1]
    n_lane_blk = SSM_WIDTH // LANES
    for r in range(n_rows):
        for c in range(n_lane_blk):
            stage_ref[c, pl.ds(r, tj, stride=SUBLANES), :] = u_ref[r, :, c * LANES:(c + 1) * LANES]
    return jnp.concatenate([stage_ref[c] for c in range(n_lane_blk)], axis=-1)


def _ssm_scan(tab_ref, bu_ref, state_ref, n_steps, states_ref=None):
    cols = _ssm_columns()
    for c0 in range(0, len(cols), SCAN_COLS):
        grp = cols[c0:c0 + SCAN_COLS]
        consts = [(tab_ref[TAB_STEP, 0, :, tc:tc + LANES], tab_ref[TAB_STEP, 1, :, tc:tc + LANES])
                  for tc, _, _ in grp]

        def body(j, st, grp=grp, consts=consts):
            at_j = pl.ds(pl.multiple_of(j * SUBLANES, SUBLANES), SUBLANES)
            new = []
            for (_, rc, ic), (ar, ai), (xr, xi) in zip(grp, consts, st):
                nr = ar * xr - ai * xi + bu_ref[at_j, rc:rc + LANES]
                ni = ar * xi + ai * xr + bu_ref[at_j, ic:ic + LANES]
                if states_ref is not None:
                    states_ref[at_j, rc:rc + LANES] = nr
                    states_ref[at_j, ic:ic + LANES] = ni
                new.append((nr, ni))
            return tuple(new)

        init = tuple((state_ref[:, rc:rc + LANES], state_ref[:, ic:ic + LANES]) for _, rc, ic in grp)
        fin = lax.fori_loop(0, n_steps, body, init)
        for (_, rc, ic), (xr, xi) in zip(grp, fin):
            state_ref[:, rc:rc + LANES] = xr
            state_ref[:, ic:ic + LANES] = xi


def _ssm_ends_kernel(u_ref, tab_ref, bmat_ref, bu_ref, ends_ref, state_ref, stage_ref):
    i = pl.program_id(0)
    tj = u_ref.shape[1]
    half_w = 2 * HALF_STATE

    @pl.when(i == 0)
    def _():
        state_ref[...] = jnp.zeros_like(state_ref)

    ub = _ssm_step_major(u_ref, stage_ref).astype(BF16)
    hw = SSM_WIDTH // 2
    for h in range(2):
        bu_ref[:, h * half_w:(h + 1) * half_w] = jnp.dot(
            ub[:, h * hw:(h + 1) * hw], bmat_ref[h], preferred_element_type=F32)
    _ssm_scan(tab_ref, bu_ref, state_ref, tj)
    ends_ref[...] = state_ref[...]


def _ssm_out_kernel(u_ref, bu_ref, ends_ref, tab_ref, cmat_ref, d_ref, y_ref, xs_ref, state_ref,
                    stage_ref, *, chunks):
    i = pl.program_id(0)
    n_rows, tj = u_ref.shape[0], u_ref.shape[1]
    half_w = 2 * HALF_STATE
    n_lane_blk = SSM_WIDTH // LANES

    @pl.when(i == 0)
    def _():
        row = lax.broadcasted_iota(jnp.int32, (SUBLANES, LANES), 0)
        seq_start = (row % chunks) == 0
        for tc, rc, ic in _ssm_columns():
            er, ei = ends_ref[:, rc:rc + LANES], ends_ref[:, ic:ic + LANES]
            pr = tab_ref[TAB_CHUNK, 0, :, tc:tc + LANES]
            pi = tab_ref[TAB_CHUNK, 1, :, tc:tc + LANES]
            xr = jnp.zeros_like(er)
            xi = jnp.zeros_like(ei)
            for _ in range(chunks - 1):
                nr = pr * xr - pi * xi + er
                ni = pr * xi + pi * xr + ei
                xr = jnp.where(seq_start, 0.0, pltpu.roll(nr, 1, 0))
                xi = jnp.where(seq_start, 0.0, pltpu.roll(ni, 1, 0))
            state_ref[:, rc:rc + LANES] = xr
            state_ref[:, ic:ic + LANES] = xi

    _ssm_scan(tab_ref, bu_ref, state_ref, tj, states_ref=xs_ref)
    ys = []
    for h in range(2):
        xh = xs_ref[:, h * half_w:(h + 1) * half_w].astype(BF16)
        ys.append(jnp.dot(xh, cmat_ref[h], preferred_element_type=F32))
    y = jnp.concatenate(ys, axis=-1) + d_ref[...] * _ssm_step_major(u_ref, stage_ref)
    for c in range(n_lane_blk):
        stage_ref[c] = y[:, c * LANES:(c + 1) * LANES]
    for r in range(n_rows):
        y_ref[r] = jnp.concatenate(
            [stage_ref[c, pl.ds(r, tj, stride=SUBLANES), :] for c in range(n_lane_blk)], axis=-1)


def _ssm(rest, a_re, a_im, log_dt, b_re, b_im, c_re, c_im, d_skip):
    b, s, _ = rest.shape
    chunks = SUBLANES // b
    chunk_len = s // chunks
    tj = min(SSM_STEP, chunk_len)
    n_steps = chunk_len // tj
    n_state = 4 * HALF_STATE
    tab, bmat, cmat = _ssm_tables(a_re, a_im, log_dt, b_re, b_im, c_re, c_im, chunk_len)
    u_rows = rest.reshape(SUBLANES, chunk_len, REST_WIDTH)
    u_spec = pl.BlockSpec((SUBLANES, tj, SSM_WIDTH), lambda j: (0, j, U_BLOCK))
    bu_spec = pl.BlockSpec((SUBLANES * tj, n_state), lambda j: (j, 0))
    whole = lambda a: pl.BlockSpec(a.shape, lambda j: (0,) * a.ndim)
    ends_shape = jax.ShapeDtypeStruct((SUBLANES, n_state), F32)
    stage = pltpu.VMEM((SSM_WIDTH // LANES, SUBLANES * tj, LANES), F32)
    bu, ends = pl.pallas_call(
        _ssm_ends_kernel,
        grid=(n_steps,),
        in_specs=[u_spec, whole(tab), whole(bmat)],
        out_specs=[bu_spec, pl.BlockSpec((SUBLANES, n_state), lambda j: (0, 0))],
        out_shape=[jax.ShapeDtypeStruct((n_steps * SUBLANES * tj, n_state), F32), ends_shape],
        scratch_shapes=[pltpu.VMEM((SUBLANES, n_state), F32), stage],
        compiler_params=_params(1),
        name="ssm_ends",
    )(u_rows, tab, bmat)
    d2 = d_skip.reshape(1, SSM_WIDTH)
    return pl.pallas_call(
        functools.partial(_ssm_out_kernel, chunks=chunks),
        grid=(n_steps,),
        in_specs=[u_spec, bu_spec, whole(ends), whole(tab), whole(cmat), whole(d2)],
        out_specs=pl.BlockSpec((SUBLANES, tj, SSM_WIDTH), lambda j: (0, j, 0)),
        out_shape=jax.ShapeDtypeStruct((SUBLANES, chunk_len, SSM_WIDTH), F32),
        scratch_shapes=[pltpu.VMEM((SUBLANES * tj, n_state), F32),
                        pltpu.VMEM((SUBLANES, n_state), F32), stage],
        compiler_params=_params(1),
        name="ssm_out",
    )(u_rows, bu, ends, tab, cmat, d2).reshape(b * s, SSM_WIDTH)


ROUTE_IDS, ROUTE_GATES, ROUTE_SLOTS = 0, TOP_K, 2 * TOP_K
RUN_START, RUN_COUNT, RUN_OFFSET = 0, 1, 2
SLOTS_PER_TILE = ROW_TILE * TOP_K


def _mix_kernel(o0_ref, o1_ref, o2_ref, l0_ref, l1_ref, l2_ref, ys_ref, ga_ref, gb_ref, x_ref,
                expand_ref, wa_ref, wglu_ref, bglu_ref, ws_ref, wo_ref, gffn_ref, wr_ref, br_ref,
                x1_ref, h2_ref, route_ref, counts_ref, runs_ref, run_ref, ostage_ref, lstage_ref):
    tm = x_ref.shape[0]

    @pl.when(jnp.logical_and(pl.program_id(0) == 0, pl.program_id(1) == 0))
    def _():
        run_ref[...] = jnp.zeros_like(run_ref)

    for slot, (o_ref, l_ref) in enumerate(((o1_ref, l1_ref), (o2_ref, l2_ref))):
        dil = ATTN_GROUPS[slot + 1][1]
        for r in range(dil):
            for c in range(GROUP_WIDTH // LANES):
                ostage_ref[slot, c, pl.ds(r, tm // dil, stride=dil), :] = o_ref[
                    0, r, :, c * LANES:(c + 1) * LANES]
            lstage_ref[slot, pl.ds(r, tm // dil, stride=dil), :] = l_ref[0, r]

    def sigmoid(z):
        return 0.5 * jnp.tanh(0.5 * z) + 0.5

    expand = expand_ref[...]
    n_lane_blk = GROUP_WIDTH // LANES

    def mix_rows(lo, n):
        rows = slice(lo, lo + n)
        group_out = [o0_ref[0, 0, rows, :]] + [
            jnp.concatenate([ostage_ref[slot, c, rows, :] for c in range(n_lane_blk)], axis=-1)
            for slot in range(2)]
        lses = [l0_ref[0, 0, rows, :], lstage_ref[0, rows, :], lstage_ref[1, rows, :]]
        mx = jnp.maximum(jnp.maximum(lses[0], lses[1]), lses[2])
        es = [jnp.exp(l - mx) for l in lses]
        inv = 1.0 / (es[0] + es[1] + es[2])
        attn = jnp.zeros((n, GROUP_WIDTH), F32)
        for e, o_g in zip(es, group_out):
            attn = attn + jnp.dot((e * inv).astype(BF16), expand, preferred_element_type=F32) * o_g
        y_a = jnp.dot(attn.astype(BF16), wa_ref[...], preferred_element_type=F32)

        ys = ys_ref[rows, :]
        ys = 0.5 * ys * (1.0 + jnp.tanh(math.sqrt(2.0 / math.pi) * (ys + 0.044715 * (ys * ys * ys))))
        glu = jnp.dot(ys.astype(BF16), wglu_ref[...], preferred_element_type=F32) + bglu_ref[...]
        ys = ys * sigmoid(glu)
        y_b = jnp.dot(ys.astype(BF16), ws_ref[...], preferred_element_type=F32)

        mixed = sigmoid(ga_ref[rows, :]) * y_a + sigmoid(gb_ref[rows, :]) * y_b
        x1 = x_ref[rows, :] + jnp.dot(mixed.astype(BF16), wo_ref[...], preferred_element_type=F32)
        x1_ref[rows, :] = x1

        ms = jnp.mean(x1 * x1, axis=-1, keepdims=True)
        h2b = (x1 * lax.rsqrt(ms + EPS) * gffn_ref[...]).astype(BF16)
        h2_ref[rows, :] = h2b
        return jnp.dot(h2b, wr_ref[...], preferred_element_type=F32) + br_ref[...]

    logits = mix_rows(0, tm)

    lane = lax.broadcasted_iota(jnp.int32, (tm, LANES), 1)
    lane_f = lane.astype(F32)
    work = jnp.where(lane < N_EXPERTS, logits, -jnp.inf)
    sel_mask = jnp.zeros((tm, LANES), F32)
    route = jnp.zeros((tm, LANES), F32)
    vals, hots = [], []
    for k in range(TOP_K):
        v = jnp.max(work, axis=-1, keepdims=True)
        idx = jnp.min(jnp.where(work == v, lane_f, float(LANES)), axis=-1, keepdims=True)
        hot = lane_f == idx
        work = jnp.where(hot, -jnp.inf, work)
        sel_mask = jnp.where(hot, 1.0, sel_mask)
        route = jnp.where(lane == ROUTE_IDS + k, idx, route)
        vals.append(v)
        hots.append(hot)
    exps = [jnp.exp(v - vals[0]) for v in vals]
    inv_den = 1.0 / (exps[0] + exps[1] + exps[2] + exps[3])

    r_i = lax.broadcasted_iota(jnp.int32, (tm, tm), 0)
    c_i = lax.broadcasted_iota(jnp.int32, (tm, tm), 1)
    tri = jnp.where(c_i < r_i, 1.0, 0.0).astype(BF16)
    local = jnp.dot(tri, sel_mask.astype(BF16), preferred_element_type=F32)
    count = jnp.sum(sel_mask, axis=0, keepdims=True)
    e_r = lax.broadcasted_iota(jnp.int32, (LANES, LANES), 0)
    e_c = lax.broadcasted_iota(jnp.int32, (LANES, LANES), 1)
    upper = jnp.where(e_r < e_c, 1.0, 0.0).astype(BF16)
    offset = jnp.dot(jnp.broadcast_to(count, (SUBLANES, LANES)).astype(BF16), upper,
                     preferred_element_type=F32)
    slot_of = local + offset[0:1, :]
    for k in range(TOP_K):
        slot = jnp.sum(jnp.where(hots[k], slot_of, 0.0), axis=-1, keepdims=True)
        route = jnp.where(lane == ROUTE_GATES + k, exps[k] * inv_den, route)
        route = jnp.where(lane == ROUTE_SLOTS + k, slot, route)
    route_ref[...] = route
    row8 = lax.broadcasted_iota(jnp.int32, (SUBLANES, LANES), 0)
    runs = jnp.where(row8 == RUN_START, run_ref[...],
                     jnp.where(row8 == RUN_COUNT, jnp.broadcast_to(count, (SUBLANES, LANES)),
                               jnp.where(row8 == RUN_OFFSET, offset, 0.0)))
    runs_ref[0] = runs
    total = run_ref[0:1, :] + count
    run_ref[...] = jnp.broadcast_to(total, run_ref.shape)
    counts_ref[...] = jnp.broadcast_to(total, counts_ref.shape)


def _mix(os_, lses, ys, rest, x2d, w):
    b, s, _ = rest.shape
    t = b * s
    tm = ROW_TILE
    n_i = s // tm
    rest2d = rest.reshape(t, REST_WIDTH)
    row = lambda width, blk=0: pl.BlockSpec((tm, width), lambda bi, i: (bi * n_i + i, blk))
    const = lambda a: pl.BlockSpec(a.shape, lambda bi, i: (0,) * a.ndim)
    grouped = lambda width: [pl.BlockSpec((1, d, tm // d, width), lambda bi, i: (bi, 0, i, 0))
                             for _, d in ATTN_GROUPS]
    consts = [w["expand"], w["wa"], w["wglu"], w["bglu"], w["ws"], w["wo"], w["gffn"], w["wr"], w["br"]]
    n_dilated = len(ATTN_GROUPS) - 1
    return pl.pallas_call(
        _mix_kernel,
        grid=(b, n_i),
        in_specs=grouped(GROUP_WIDTH) + grouped(LANES) + [
            row(SSM_WIDTH),
            row(D_MODEL, GATE_A_BLOCK),
            row(D_MODEL, GATE_B_BLOCK),
            row(D_MODEL),
        ] + [const(a) for a in consts],
        out_specs=[
            row(D_MODEL),
            row(D_MODEL),
            row(LANES),
            pl.BlockSpec((SUBLANES, LANES), lambda bi, i: (0, 0)),
            pl.BlockSpec((1, SUBLANES, LANES), lambda bi, i: (bi * n_i + i, 0, 0)),
        ],
        out_shape=[
            jax.ShapeDtypeStruct((t, D_MODEL), F32),
            jax.ShapeDtypeStruct((t, D_MODEL), BF16),
            jax.ShapeDtypeStruct((t, LANES), F32),
            jax.ShapeDtypeStruct((SUBLANES, LANES), F32),
            jax.ShapeDtypeStruct((t // tm, SUBLANES, LANES), F32),
        ],
        scratch_shapes=[
            pltpu.VMEM((SUBLANES, LANES), F32),
            pltpu.VMEM((n_dilated, GROUP_WIDTH // LANES, tm, LANES), F32),
            pltpu.VMEM((n_dilated, tm, LANES), F32),
        ],
        compiler_params=_params(2),
        name="mix_router",
    )(*os_, *lses, ys, rest2d, rest2d, x2d, *consts)


def _n_moe_tiles(t):
    return t * TOP_K // MOE_TILE + N_EXPERTS


def _copy_run(src_ref, dst_ref, src_row, dst_row, n_rows, sem):
    size = ROW_TILE
    priority = 0
    while size >= 1:
        @pl.when((n_rows & size) != 0)
        def _(size=size, src_row=src_row, dst_row=dst_row, priority=priority):
            pltpu.make_async_copy(
                src_ref.at[pl.ds(pl.multiple_of(src_row * SUBLANES, SUBLANES), size * SUBLANES), :],
                dst_ref.at[pl.ds(pl.multiple_of(dst_row * SUBLANES, SUBLANES), size * SUBLANES), :],
                sem).start(priority=priority)
        step = n_rows & size
        src_row = src_row + step
        dst_row = dst_row + step
        size //= 2
        priority = 1 - priority


def _wait_tile(hbm_ref, vmem_ref, sem):
    pltpu.make_async_copy(hbm_ref.at[pl.ds(0, SLOTS_PER_TILE * SUBLANES), :], vmem_ref, sem).wait()


def _dispatch_kernel(counts_ref, start_ref, cnt_ref, toff_ref, h2_ref, route_ref,
                     xs_ref, texp_ref, off_ref, stage_ref, sems):
    i = pl.program_id(0)
    n_steps = pl.num_programs(0)
    n_tiles = texp_ref.shape[0] - 1

    @pl.when(i == 0)
    def _():
        def per_expert(e, start):
            off_ref[e] = start
            n_t = (counts_ref[e] + (MOE_TILE - 1)) // MOE_TILE

            def mark(ti, c):
                texp_ref[start // MOE_TILE + ti] = e
                return c

            lax.fori_loop(0, n_t, mark, 0)
            return start + n_t * MOE_TILE

        end = lax.fori_loop(0, N_EXPERTS, per_expert, 0)
        n_active = end // MOE_TILE
        texp_ref[n_tiles] = n_active

        def fill(ti, c):
            texp_ref[ti] = texp_ref[n_active - 1]
            return c

        lax.fori_loop(n_active, n_tiles, fill, 0)

    slots_t = route_ref[...].T
    slot_id = lax.broadcasted_iota(jnp.int32, (SLOTS_PER_TILE, ROW_TILE), 0).astype(F32)
    perm = jnp.zeros((SLOTS_PER_TILE, ROW_TILE), F32)
    for k in range(TOP_K):
        perm = jnp.where(slot_id == slots_t[ROUTE_SLOTS + k:ROUTE_SLOTS + k + 1, :], 1.0, perm)
    srt = jnp.dot(perm.astype(BF16), h2_ref[...], preferred_element_type=F32)
    buf = i % 2
    for s in range(D_MODEL // LANES):
        stage_ref[buf, pl.ds(s, SLOTS_PER_TILE, stride=SUBLANES), :] = srt[:, s * LANES:(s + 1) * LANES]

    @pl.when(i > 0)
    def _():
        _wait_tile(xs_ref, stage_ref.at[1 - buf], sems.at[1 - buf])

    def per_run(e, c):
        r = i * N_EXPERTS + e
        _copy_run(stage_ref.at[buf], xs_ref, toff_ref[r], off_ref[e] + start_ref[r], cnt_ref[r],
                  sems.at[buf])
        return c

    lax.fori_loop(0, N_EXPERTS, per_run, 0)

    @pl.when(i == n_steps - 1)
    def _():
        _wait_tile(xs_ref, stage_ref.at[buf], sems.at[buf])


def _dispatch(counts, runs, h2, route):
    t = h2.shape[0]
    tm = ROW_TILE
    n_tiles = _n_moe_tiles(t)
    grid_spec = pltpu.PrefetchScalarGridSpec(
        num_scalar_prefetch=4,
        grid=(t // tm,),
        in_specs=[
            pl.BlockSpec((tm, D_MODEL), lambda i, *_: (i, 0)),
            pl.BlockSpec((tm, LANES), lambda i, *_: (i, 0)),
        ],
        out_specs=[
            pl.BlockSpec(memory_space=pl.ANY),
            pl.BlockSpec((n_tiles + 1,), lambda i, *_: (0,), memory_space=pltpu.SMEM),
            pl.BlockSpec((N_EXPERTS,), lambda i, *_: (0,), memory_space=pltpu.SMEM),
        ],
        scratch_shapes=[
            pltpu.VMEM((2, SLOTS_PER_TILE * SUBLANES, LANES), F32),
            pltpu.SemaphoreType.DMA((2,)),
        ],
    )
    return pl.pallas_call(
        _dispatch_kernel,
        grid_spec=grid_spec,
        out_shape=[
            jax.ShapeDtypeStruct((n_tiles * MOE_TILE * SUBLANES, LANES), F32),
            jax.ShapeDtypeStruct((n_tiles + 1,), jnp.int32),
            jax.ShapeDtypeStruct((N_EXPERTS,), jnp.int32),
        ],
        compiler_params=_params(1),
        name="moe_dispatch",
    )(counts, *runs, h2, route)


def _ffn_kernel(texp_ref, xs_ref, wgu_hbm, bgu_ref, wd_hbm, bd_ref, out_ref,
                wgu_f32, wd_f32, wgu_bf, wd_bf, state_ref, sems):
    i = pl.program_id(0)
    tm = MOE_TILE
    n_active = texp_ref[pl.num_programs(0)]
    e = texp_ref[i]

    def fetch(expert, slot):
        return (pltpu.make_async_copy(wgu_hbm.at[expert], wgu_f32.at[slot], sems.at[slot, 0]),
                pltpu.make_async_copy(wd_hbm.at[expert], wd_f32.at[slot], sems.at[slot, 1]))

    @pl.when(i == 0)
    def _():
        state_ref[0] = -1
        state_ref[1] = 0
        for cp in fetch(e, 0):
            cp.start()

    @pl.when(jnp.logical_and(i < n_active, e != state_ref[0]))
    def _():
        slot = state_ref[1]
        for cp in fetch(e, slot):
            cp.wait()
        wgu_bf[...] = wgu_f32[slot].astype(BF16)
        wd_bf[...] = wd_f32[slot].astype(BF16)
        state_ref[0] = e
        state_ref[1] = 1 - slot
        nxt = lax.while_loop(lambda j: jnp.logical_and(j < n_active, texp_ref[j] == e),
                             lambda j: j + 1, i + 1)

        @pl.when(nxt < n_active)
        def _():
            for cp in fetch(texp_ref[nxt], 1 - slot):
                cp.start()

    @pl.when(i < n_active)
    def _():
        x = jnp.concatenate(
            [xs_ref[pl.ds(s, tm, stride=SUBLANES), :] for s in range(D_MODEL // LANES)], axis=-1)
        gu = jnp.dot(x.astype(BF16), wgu_bf[...], preferred_element_type=F32) + bgu_ref[0]
        x_glu = jnp.minimum(gu[:, :D_FF], SWIGLU_LIMIT)
        x_lin = jnp.clip(gu[:, D_FF:], -SWIGLU_LIMIT, SWIGLU_LIMIT)
        act = x_glu * jax.nn.sigmoid(SWIGLU_ALPHA * x_glu) * (x_lin + 1.0)
        out = jnp.dot(act.astype(BF16), wd_bf[...], preferred_element_type=F32) + bd_ref[0]
        for s in range(D_MODEL // LANES):
            out_ref[pl.ds(s, tm, stride=SUBLANES), :] = out[:, s * LANES:(s + 1) * LANES]


def _expert_ffn(texp, xs_tiles, w_gu, b_gu, w_down, b_down):
    n_tiles = texp.shape[0] - 1
    tm = MOE_TILE
    grid_spec = pltpu.PrefetchScalarGridSpec(
        num_scalar_prefetch=1,
        grid=(n_tiles,),
        in_specs=[
            pl.BlockSpec((tm * SUBLANES, LANES), lambda i, te: (jnp.minimum(i, te[n_tiles] - 1), 0)),
            pl.BlockSpec(memory_space=pl.ANY),
            pl.BlockSpec((1, 1, 2 * D_FF), lambda i, te: (te[i], 0, 0)),
            pl.BlockSpec(memory_space=pl.ANY),
            pl.BlockSpec((1, 1, D_MODEL), lambda i, te: (te[i], 0, 0)),
        ],
        out_specs=pl.BlockSpec((tm * SUBLANES, LANES), lambda i, te: (i, 0)),
        scratch_shapes=[
            pltpu.VMEM((2, D_MODEL, 2 * D_FF), F32),
            pltpu.VMEM((2, D_FF, D_MODEL), F32),
            pltpu.VMEM((D_MODEL, 2 * D_FF), BF16),
            pltpu.VMEM((D_FF, D_MODEL), BF16),
            pltpu.SMEM((2,), jnp.int32),
            pltpu.SemaphoreType.DMA((2, 2)),
        ],
    )
    return pl.pallas_call(
        _ffn_kernel,
        grid_spec=grid_spec,
        out_shape=jax.ShapeDtypeStruct((n_tiles * tm * SUBLANES, LANES), F32),
        compiler_params=_params(1),
        name="moe_ffn",
    )(texp, xs_tiles, w_gu, b_gu.reshape(N_EXPERTS, 1, 2 * D_FF), w_down,
      b_down.reshape(N_EXPERTS, 1, D_MODEL))


def _combine_kernel(off_ref, start_ref, cnt_ref, toff_ref, route_ref, x1_ref, ys_ref, out_ref,
                    buf_ref, sems):
    i = pl.program_id(0)
    n_steps = pl.num_programs(0)
    tm = x1_ref.shape[0]

    def fetch(tile, buf):
        def per_run(e, c):
            r = tile * N_EXPERTS + e
            _copy_run(ys_ref, buf_ref.at[buf], off_ref[e] + start_ref[r], toff_ref[r], cnt_ref[r],
                      sems.at[buf])
            return c

        lax.fori_loop(0, N_EXPERTS, per_run, 0)

    @pl.when(i == 0)
    def _():
        fetch(0, 0)

    @pl.when(i + 1 < n_steps)
    def _():
        fetch(i + 1, (i + 1) % 2)

    buf = i % 2
    _wait_tile(ys_ref, buf_ref.at[buf], sems.at[buf])

    rows = jnp.concatenate(
        [buf_ref[buf, pl.ds(s, SLOTS_PER_TILE, stride=SUBLANES), :] for s in range(D_MODEL // LANES)],
        axis=-1).astype(BF16)
    route = route_ref[...]
    slot_id = lax.broadcasted_iota(jnp.int32, (tm, SLOTS_PER_TILE), 1).astype(F32)
    weights = jnp.zeros((tm, SLOTS_PER_TILE), F32)
    for k in range(TOP_K):
        slot = route[:, ROUTE_SLOTS + k:ROUTE_SLOTS + k + 1]
        gate = route[:, ROUTE_GATES + k:ROUTE_GATES + k + 1]
        weights = jnp.where(slot_id == slot, gate, weights)
    out_ref[...] = x1_ref[...] + jnp.dot(weights.astype(BF16), rows, preferred_element_type=F32)


def _combine(off, runs, route, x1, ys_tiles):
    t = x1.shape[0]
    tm = ROW_TILE
    grid_spec = pltpu.PrefetchScalarGridSpec(
        num_scalar_prefetch=4,
        grid=(t // tm,),
        in_specs=[
            pl.BlockSpec((tm, LANES), lambda i, *_: (i, 0)),
            pl.BlockSpec((tm, D_MODEL), lambda i, *_: (i, 0)),
            pl.BlockSpec(memory_space=pl.ANY),
        ],
        out_specs=pl.BlockSpec((tm, D_MODEL), lambda i, *_: (i, 0)),
        scratch_shapes=[
            pltpu.VMEM((2, SLOTS_PER_TILE * SUBLANES, LANES), F32),
            pltpu.SemaphoreType.DMA((2,)),
        ],
    )
    return pl.pallas_call(
        _combine_kernel,
        grid_spec=grid_spec,
        out_shape=jax.ShapeDtypeStruct((t, D_MODEL), F32),
        compiler_params=_params(1),
        name="moe_combine",
    )(off, *runs, route, x1, ys_tiles)


def _layer(x, g_mix, w_in, q_gain, k_gain, rel_bias, ssm_a_re, ssm_a_im, ssm_log_dt, ssm_b_re,
           ssm_b_im, ssm_c_re, ssm_c_im, ssm_d, w_glu, b_glu, w_attn_proj, w_ssm_proj, w_out,
           g_ffn, w_router, b_router, w_gate_up, b_gate_up, w_down, b_down):
    b, s, _ = x.shape
    x2d = x.reshape(b * s, D_MODEL)

    *qkvs, rest = _inproj(x, g_mix, w_in.astype(BF16))

    head_of = np.arange(GROUP_WIDTH) // HEAD_DIM
    seg = jnp.asarray(head_of[:, None] == head_of[None, :], BF16)
    os_, lses = [], []
    for gi, (window, dil) in enumerate(ATTN_GROUPS):
        table = rel_bias[:, gi * HEADS_PER_GROUP:(gi + 1) * HEADS_PER_GROUP]
        o, lse = _attention_group(qkvs[gi], gi, _attn_bias(table, window, dil), q_gain[gi], k_gain[gi], seg)
        os_.append(o)
        lses.append(lse)

    ys = _ssm(rest, ssm_a_re, ssm_a_im, ssm_log_dt, ssm_b_re, ssm_b_im, ssm_c_re, ssm_c_im, ssm_d)

    expand = jnp.asarray(np.arange(LANES)[:, None] == head_of[None, :], BF16)
    pad_e = LANES - N_EXPERTS
    weights = dict(
        expand=expand, wa=w_attn_proj.astype(BF16), wglu=w_glu.astype(BF16),
        bglu=b_glu.reshape(1, SSM_WIDTH), ws=w_ssm_proj.astype(BF16), wo=w_out.astype(BF16),
        gffn=g_ffn.reshape(1, D_MODEL), wr=jnp.pad(w_router, ((0, 0), (0, pad_e))).astype(BF16),
        br=jnp.pad(b_router, (0, pad_e)).reshape(1, LANES))
    x1, h2, route, counts, run_tab = _mix(os_, lses, ys, rest, x2d, weights)

    counts_i = counts[0, :N_EXPERTS].astype(jnp.int32)
    run_table = run_tab[:, :, :N_EXPERTS].astype(jnp.int32)
    runs = [run_table[:, row].reshape(-1) for row in (RUN_START, RUN_COUNT, RUN_OFFSET)]
    xs_tiles, texp, off = _dispatch(counts_i, runs, h2, route)
    ys_tiles = _expert_ffn(texp, xs_tiles, w_gate_up, b_gate_up, w_down, b_down)
    out = _combine(off, runs, route, x1, ys_tiles)
    return out.reshape(b, s, D_MODEL)


_layer_jit = jax.jit(_layer)


def kernel(x, g_mix, w_in, q_gain, k_gain, rel_bias, ssm_a_re, ssm_a_im, ssm_log_dt, ssm_b_re, ssm_b_im, ssm_c_re, ssm_c_im, ssm_d, w_glu, b_glu, w_attn_proj, w_ssm_proj, w_out, g_ffn, w_router, b_router, w_gate_up, b_gate_up, w_down, b_down):
    return _layer_jit(x, g_mix[0], w_in[0], q_gain[0], k_gain[0], rel_bias, ssm_a_re[0], ssm_a_im[0],
                      ssm_log_dt[0], ssm_b_re[0], ssm_b_im[0], ssm_c_re[0], ssm_c_im[0], ssm_d[0],
                      w_glu[0], b_glu[0], w_attn_proj[0], w_ssm_proj[0], w_out[0], g_ffn[0],
                      w_router[0], b_router[0], w_gate_up[0], b_gate_up[0], w_down[0], b_down[0])
```
